```python
import math
import jax, jax.numpy as jnp
from jax import lax
import numpy as np

D_MODEL = 1024
BATCH = 8
SEQ = 2048
DEPTH = 2
DEC_BATCH = 128
DEC_SEQ = 1
PAST_LEN = 16384
PAGE_SIZE = 128

MLSTM_HEADS = 4
HEAD_DIM = 128
MLSTM_WIDTH = MLSTM_HEADS * HEAD_DIM
CONV_CH = D_MODEL - MLSTM_WIDTH
CONV_WIDTH = 31
D_FF = -(-8 * D_MODEL // (3 * 256)) * 256
CHUNK = 128
PROJ = 4 * MLSTM_WIDTH + 2 * MLSTM_HEADS + 2 * CONV_CH
EPS = 1e-6

kernel_name = 'hymba_mlstm_conformer_adaln_step'


def _rmsnorm(x, g):
    x32 = x.astype(jnp.float32)
    y = x32 * lax.rsqrt(jnp.mean(x32 * x32, axis=-1, keepdims=True) + EPS) * g.astype(jnp.float32)
    return y.astype(x.dtype)


def _mlstm_chunkwise(q, k, v, i_log, logf, C0, n0, m0):
    B, S, H, DH = q.shape
    L = math.gcd(S, CHUNK)
    NC = S // L

    def chunks(a):
        return jnp.moveaxis(a.reshape((B, NC, L) + a.shape[2:]), 1, 0)

    mask = jnp.tril(jnp.ones((L, L), dtype=bool))

    def step(carry, xs):
        C, n, m = carry
        qb, kb, vb, ib, fb = xs
        F = jnp.cumsum(fb, axis=1).transpose(0, 2, 1)
        it = ib.transpose(0, 2, 1)
        g = F[..., -1]
        Dlog = F[..., :, None] - F[..., None, :] + it[..., None, :]
        Dlog = jnp.where(mask, Dlog, -jnp.inf)
        inter = F + m[..., None]
        m_t = jnp.maximum(inter, jnp.max(Dlog, axis=-1))
        Dw = jnp.exp(Dlog - m_t[..., None])
        a_inter = jnp.exp(inter - m_t)
        s = jnp.einsum('bthd,bjhd->bhtj', qb, kb) * Dw
        num = (jnp.einsum('bhtj,bjhe->bthe', s, vb)
               + jnp.einsum('bthd,bhde->bthe', qb, C) * a_inter.transpose(0, 2, 1)[..., None])
        den = jnp.sum(s, axis=-1) + a_inter * jnp.einsum('bthd,bhd->bht', qb, n)
        bound = jnp.maximum(jnp.abs(den), jnp.exp(-m_t)).transpose(0, 2, 1)[..., None]
        h = num / bound
        wlog = g[..., None] - F + it
        m_new = jnp.maximum(g + m, jnp.max(wlog, axis=-1))
        wj = jnp.exp(wlog - m_new[..., None]).transpose(0, 2, 1)[..., None]
        decay = jnp.exp(g + m - m_new)
        kw = kb * wj
        C_new = decay[..., None, None] * C + jnp.einsum('bjhd,bjhe->bhde', kw, vb)
        n_new = decay[..., None] * n + jnp.sum(kw, axis=1)
        return (C_new, n_new, m_new), h

    (C1, n1, m1), hs = lax.scan(step, (C0, n0, m0),
                                (chunks(q), chunks(k), chunks(v), chunks(i_log), chunks(logf)))
    h = jnp.moveaxis(hs, 0, 1).reshape(B, S, H, DH)
    return h, C1, n1, m1


def _layer(x, c, C0, n0, m0, buf, w_ada, b_ada, g_mix, w_in, b_gates, g_head,
           w_dw, b_dw, ln_g, ln_b, w_out, g_ffn, w_ffn13, w_ffn2):
    f32 = jnp.float32
    Bn, S, _ = x.shape
    mod = (c @ w_ada + b_ada)[:, None, :]
    sh1, sc1, ga1, sh2, sc2, ga2 = jnp.split(mod, 6, axis=-1)
    h = _rmsnorm(x, g_mix) * (1 + sc1) + sh1
    p = h @ w_in
    M, H = MLSTM_WIDTH, MLSTM_HEADS
    q, k, v, o, gates, glu = jnp.split(p, [M, 2 * M, 3 * M, 4 * M, 4 * M + 2 * H], axis=-1)
    q = q.astype(f32).reshape(Bn, S, H, HEAD_DIM)
    k = k.astype(f32).reshape(Bn, S, H, HEAD_DIM) * (HEAD_DIM ** -0.5)
    v = v.astype(f32).reshape(Bn, S, H, HEAD_DIM)
    gates = gates.astype(f32) + b_gates.astype(f32)
    i_log, f_pre = jnp.split(gates, 2, axis=-1)
    logf = jax.nn.log_sigmoid(f_pre)
    hm, C1, n1, m1 = _mlstm_chunkwise(q, k, v, i_log, logf,
                                      C0.astype(f32), n0.astype(f32), m0.astype(f32))
    hm = hm * lax.rsqrt(jnp.mean(hm * hm, axis=-1, keepdims=True) + EPS) * g_head.astype(f32)
    hm = hm * jax.nn.sigmoid(o.astype(f32)).reshape(Bn, S, H, HEAD_DIM)
    hm = hm.reshape(Bn, S, M).astype(x.dtype)
    a, b = jnp.split(glu, 2, axis=-1)
    u = a * jax.nn.sigmoid(b)
    ucat = jnp.concatenate([buf.astype(u.dtype), u], axis=1)
    z = lax.conv_general_dilated(ucat, w_dw.astype(u.dtype)[:, None, :], window_strides=(1,),
                                 padding='VALID', dimension_numbers=('NWC', 'WIO', 'NWC'),
                                 feature_group_count=CONV_CH) + b_dw
    z32 = z.astype(f32)
    mu = jnp.mean(z32, axis=-1, keepdims=True)
    var = jnp.mean(jnp.square(z32 - mu), axis=-1, keepdims=True)
    zn = (z32 - mu) * lax.rsqrt(var + EPS) * ln_g.astype(f32) + ln_b.astype(f32)
    zc = jax.nn.silu(zn).astype(x.dtype)
    new_buf = ucat[:, -(CONV_WIDTH - 1):]
    mix = jnp.concatenate([hm, zc], axis=-1) @ w_out
    x = x + ga1 * mix
    h2 = _rmsnorm(x, g_ffn) * (1 + sc2) + sh2
    gg, uu = jnp.split(h2 @ w_ffn13, 2, axis=-1)
    x = x + ga2 * ((jax.nn.silu(gg) * uu) @ w_ffn2)
    return x, C1, n1, m1, new_buf


def setup_inputs(seed: int = 0) -> dict:
    key = jax.random.key(seed)
    ks = jax.random.split(key, 24)
    f32 = jnp.float32

    def nrm(k, shape, scale):
        return jax.random.normal(k, shape, f32) * scale

    H = MLSTM_HEADS
    b_i = nrm(ks[17], (DEPTH, H), 0.1)
    b_f = 3.0 + nrm(ks[18], (DEPTH, H), 0.1)
    return {
        'x_prompt': nrm(ks[0], (BATCH, SEQ, D_MODEL), 1.0),
        'x_sample': nrm(ks[1], (DEC_BATCH, DEC_SEQ, D_MODEL), 1.0),
        'c_prompt': nrm(ks[2], (BATCH, D_MODEL), 1.0),
        'c_sample': nrm(ks[3], (DEC_BATCH, D_MODEL), 1.0),
        'state_C': nrm(ks[4], (DEPTH, DEC_BATCH, H, HEAD_DIM, HEAD_DIM), 0.05),
        'state_n': nrm(ks[5], (DEPTH, DEC_BATCH, H, HEAD_DIM), 0.1),
        'state_m': nrm(ks[6], (DEPTH, DEC_BATCH, H), 1.0),
        'state_conv': nrm(ks[7], (DEPTH, DEC_BATCH, CONV_WIDTH - 1, CONV_CH), 1.0),
        'w_ada': nrm(ks[8], (DEPTH, D_MODEL, 6 * D_MODEL), 0.5 * D_MODEL ** -0.5),
        'b_ada': nrm(ks[9], (DEPTH, 6 * D_MODEL), 0.01),
        'g_mix': 1.0 + nrm(ks[10], (DEPTH, D_MODEL), 0.01),
        'w_in': nrm(ks[11], (DEPTH, D_MODEL, PROJ), D_MODEL ** -0.5),
        'b_gates': jnp.concatenate([b_i, b_f], axis=-1),
        'g_head': 1.0 + nrm(ks[12], (DEPTH, H, HEAD_DIM), 0.01),
        'w_dw': nrm(ks[13], (DEPTH, CONV_WIDTH, CONV_CH), CONV_WIDTH ** -0.5),
        'b_dw': nrm(ks[14], (DEPTH, CONV_CH), 0.01),
        'ln_g': 1.0 + nrm(ks[15], (DEPTH, CONV_CH), 0.01),
        'ln_b': nrm(ks[16], (DEPTH, CONV_CH), 0.01),
        'w_out': nrm(ks[19], (DEPTH, MLSTM_WIDTH + CONV_CH, D_MODEL), (MLSTM_WIDTH + CONV_CH) ** -0.5),
        'g_ffn': 1.0 + nrm(ks[20], (DEPTH, D_MODEL), 0.01),
        'w_ffn13': nrm(ks[21], (DEPTH, D_MODEL, 2 * D_FF), D_MODEL ** -0.5),
        'w_ffn2': nrm(ks[22], (DEPTH, D_FF, D_MODEL), D_FF ** -0.5),
        'g_final': 1.0 + nrm(ks[23], (D_MODEL,), 0.01),
    }


def reference(x_prompt, x_sample, c_prompt, c_sample, state_C, state_n, state_m, state_conv,
              w_ada, b_ada, g_mix, w_in, b_gates, g_head, w_dw, b_dw, ln_g, ln_b, w_out,
              g_ffn, w_ffn13, w_ffn2, g_final):
    f32 = jnp.float32
    Bp = x_prompt.shape[0]
    H = MLSTM_HEADS
    xp, xs = x_prompt, x_sample
    Cp_l, np_l, mp_l, bp_l = [], [], [], []
    Cs_l, ns_l, ms_l, bs_l = [], [], [], []
    for l in range(DEPTH):
        wl = (w_ada[l], b_ada[l], g_mix[l], w_in[l], b_gates[l], g_head[l], w_dw[l], b_dw[l],
              ln_g[l], ln_b[l], w_out[l], g_ffn[l], w_ffn13[l], w_ffn2[l])
        xp, C1, n1, m1, b1 = _layer(
            xp, c_prompt,
            jnp.zeros((Bp, H, HEAD_DIM, HEAD_DIM), f32), jnp.zeros((Bp, H, HEAD_DIM), f32),
            jnp.zeros((Bp, H), f32), jnp.zeros((Bp, CONV_WIDTH - 1, CONV_CH), xp.dtype), *wl)
        Cp_l.append(C1); np_l.append(n1); mp_l.append(m1); bp_l.append(b1)
        xs, C2, n2, m2, b2 = _layer(xs, c_sample, state_C[l], state_n[l], state_m[l],
                                    state_conv[l], *wl)
        Cs_l.append(C2); ns_l.append(n2); ms_l.append(m2); bs_l.append(b2)
    y_prompt = _rmsnorm(xp, g_final)
    y_sample = _rmsnorm(xs, g_final)
    C_prompt = jnp.stack(Cp_l); n_prompt = jnp.stack(np_l)
    m_prompt = jnp.stack(mp_l); conv_prompt = jnp.stack(bp_l)
    C_sample = jnp.stack(Cs_l); n_sample = jnp.stack(ns_l)
    m_sample = jnp.stack(ms_l); conv_sample = jnp.stack(bs_l)
    return (y_prompt, y_sample, C_prompt, n_prompt, m_prompt, conv_prompt,
            C_sample, n_sample, m_sample, conv_sample)
```

```python
import functools

import jax
import jax.numpy as jnp
from jax import lax
from jax.experimental import pallas as pl
from jax.experimental.pallas import tpu as pltpu

F32 = jnp.float32
BF16 = jnp.bfloat16

D_MODEL = 1024
HEADS = 4
HEAD_DIM = 128
MLSTM_WIDTH = HEADS * HEAD_DIM
CONV_CH = D_MODEL - MLSTM_WIDTH
CONV_WIDTH = 31
HIST = CONV_WIDTH - 1
D_FF = 2816
CHUNK = 128
EPS = 1e-6

LANES = 128
SUBLANES = 8
MXU_N = 256

COL_Q = 0
COL_K = MLSTM_WIDTH
COL_V = 2 * MLSTM_WIDTH
COL_O = 3 * MLSTM_WIDTH
COL_A = 4 * MLSTM_WIDTH
COL_B = COL_A + CONV_CH
COL_G = COL_B + CONV_CH
PROJ_PAD = COL_G + LANES
HIST_PAD = 32
VMEM_LIMIT = 60 * 1024 * 1024


def _log_sigmoid(x):
    return jnp.minimum(x, 0.0) - jnp.log1p(jnp.exp(-jnp.abs(x)))


def _sigmoid(x):
    return 1.0 / (1.0 + jnp.exp(-x))


def _bdot(a, b):
    return jnp.dot(a, b, preferred_element_type=F32)


def _bdot_nt(a, b):
    return lax.dot_general(a, b, (((1,), (1,)), ((), ())), preferred_element_type=F32)


def _rms_mod(x, g, sc, sh):
    ms = jnp.mean(x * x, axis=-1, keepdims=True)
    return (x * lax.rsqrt(ms + EPS) * g) * (1.0 + sc) + sh


def _mod_kernel(c_ref, w_ref, b_ref, o_ref):
    o_ref[...] = _bdot(c_ref[...].astype(BF16), w_ref[...].astype(BF16)) + b_ref[...]


def _modulation(c_all, w_ada, b_ada):
    depth, d, n6 = w_ada.shape
    rows = c_all.shape[0]
    tn = 1024
    return pl.pallas_call(
        _mod_kernel,
        grid=(depth, n6 // tn),
        in_specs=[
            pl.BlockSpec((rows, d), lambda l, n: (0, 0)),
            pl.BlockSpec((None, d, tn), lambda l, n: (l, 0, n)),
            pl.BlockSpec((None, 1, tn), lambda l, n: (l, 0, n)),
        ],
        out_specs=pl.BlockSpec((None, rows, tn), lambda l, n: (l, 0, n)),
        out_shape=jax.ShapeDtypeStruct((depth, rows, n6), F32),
        compiler_params=pltpu.CompilerParams(
            dimension_semantics=("arbitrary", "arbitrary"), vmem_limit_bytes=VMEM_LIMIT),
        name="adaln_mod",
    )(c_all, w_ada, b_ada.reshape(depth, 1, n6))


def _in_proj(h_ref, win_ref, p_ref):
    for n0 in range(0, PROJ_PAD, MXU_N):
        n1 = min(n0 + MXU_N, PROJ_PAD)
        p_ref[:, n0:n1] = _bdot(h_ref[...], win_ref[:, n0:n1])


def _out_proj_ffn(x_ref, mix_ref, mod_rows, wout_ref, gffn_ref, w13_ref, w2_ref, gfin_ref,
                  y_ref, x1_ref, h2_ref, act_ref, *, rows, row_chunk, final):
    for n0 in range(0, D_MODEL, MXU_N):
        cs = slice(n0, n0 + MXU_N)
        mx = _bdot(mix_ref[...], wout_ref[:, cs])
        x1_ref[:, cs] = x_ref[:, cs] + mod_rows("ga1", cs) * mx

    def norm2(i, carry):
        r = pl.ds(pl.multiple_of(i * row_chunk, row_chunk), row_chunk)
        h2 = _rms_mod(x1_ref[r, :], gffn_ref[...], mod_rows("sc2", slice(None), r),
                      mod_rows("sh2", slice(None), r))
        h2_ref[r, :] = h2.astype(BF16)
        return carry

    lax.fori_loop(0, rows // row_chunk, norm2, 0)

    for n0 in range(0, D_FF, MXU_N):
        gg = _bdot(h2_ref[...], w13_ref[:, n0:n0 + MXU_N])
        uu = _bdot(h2_ref[...], w13_ref[:, D_FF + n0:D_FF + n0 + MXU_N])
        act_ref[:, n0:n0 + MXU_N] = (gg * _sigmoid(gg) * uu).astype(BF16)

    dst = x1_ref if final else y_ref
    for n0 in range(0, D_MODEL, MXU_N):
        cs = slice(n0, n0 + MXU_N)
        ff = _bdot(act_ref[...], w2_ref[:, cs])
        dst[:, cs] = x1_ref[:, cs] + mod_rows("ga2", cs) * ff

    if final:
        def fin(i, carry):
            r = pl.ds(pl.multiple_of(i * row_chunk, row_chunk), row_chunk)
            xb = x1_ref[r, :]
            ms = jnp.mean(xb * xb, axis=-1, keepdims=True)
            y_ref[r, :] = xb * lax.rsqrt(ms + EPS) * gfin_ref[...]
            return carry

        lax.fori_loop(0, rows // row_chunk, fin, 0)


def _layernorm_silu(z, lng, lnb):
    mu = jnp.mean(z, axis=-1, keepdims=True)
    zc = z - mu
    var = jnp.mean(zc * zc, axis=-1, keepdims=True)
    zn = zc * lax.rsqrt(var + EPS) * lng + lnb
    return zn * _sigmoid(zn)


_MOD_ROW = {"sh1": 0, "sc1": 1, "ga1": 2, "sh2": 3, "sc2": 4, "ga2": 5}


def _cumsum_chunk(tril_bf, x):
    hi = x.astype(BF16)
    r1 = x - hi.astype(F32)
    mid = r1.astype(BF16)
    lo = (r1 - mid.astype(F32)).astype(BF16)
    return _bdot(tril_bf, hi) + _bdot(tril_bf, mid) + _bdot(tril_bf, lo)


def _mlstm_chunk(p_ref, r0, bg_ref, ghead_ref, mix_ref, C_ref, n_ref, m_ref, tril, tril_bf):
    rs = slice(r0, r0 + CHUNK)
    gates = p_ref[rs, COL_G:COL_G + LANES] + bg_ref[...]
    fcum = _cumsum_chunk(tril_bf, _log_sigmoid(gates))
    gates_t = gates.T
    fcum_t = fcum.T
    for h in range(HEADS):
        hs = slice(h * HEAD_DIM, (h + 1) * HEAD_DIM)
        q = p_ref[rs, COL_Q + h * HEAD_DIM:COL_Q + (h + 1) * HEAD_DIM]
        k = p_ref[rs, COL_K + h * HEAD_DIM:COL_K + (h + 1) * HEAD_DIM] * (HEAD_DIM ** -0.5)
        v = p_ref[rs, COL_V + h * HEAD_DIM:COL_V + (h + 1) * HEAD_DIM]
        o = p_ref[rs, COL_O + h * HEAD_DIM:COL_O + (h + 1) * HEAD_DIM]
        q_bf = q.astype(BF16)
        v_bf = v.astype(BF16)
        f_col = fcum[:, HEADS + h:HEADS + h + 1]
        i_col = gates[:, h:h + 1]
        f_row = fcum_t[HEADS + h:HEADS + h + 1, :]
        i_row = gates_t[h:h + 1, :]
        m_prev = m_ref[h:h + 1, 0:1]
        n_row = n_ref[h:h + 1, :]
        c_mat = C_ref[h]

        g = f_row[:, CHUNK - 1:CHUNK]
        dlog = jnp.where(tril, f_col - f_row + i_row, -jnp.inf)
        inter = f_col + m_prev
        m_t = jnp.maximum(inter, jnp.max(dlog, axis=-1, keepdims=True))
        dw = jnp.exp(dlog - m_t)
        a_inter = jnp.exp(inter - m_t)
        s = _bdot_nt(q_bf, k.astype(BF16)) * dw
        num = _bdot(s.astype(BF16), v_bf) + _bdot(q_bf, c_mat.astype(BF16)) * a_inter
        qn = jnp.sum(q * n_row, axis=-1, keepdims=True)
        den = jnp.sum(s, axis=-1, keepdims=True) + a_inter * qn
        bound = jnp.maximum(jnp.abs(den), jnp.exp(-m_t))
        hh = num * (1.0 / bound)

        wmax = jnp.max(g - f_row + i_row, axis=-1, keepdims=True)
        m_new = jnp.maximum(g + m_prev, wmax)
        wj = jnp.exp(g - f_col + i_col - m_new)
        decay = jnp.exp(g + m_prev - m_new)
        kw = k * wj
        C_ref[h] = decay * c_mat + _bdot(kw.T.astype(BF16), v_bf)
        n_ref[h:h + 1, :] = decay * n_row + jnp.sum(kw, axis=0, keepdims=True)
        m_ref[h:h + 1, :] = jnp.broadcast_to(m_new, (1, LANES))

        hn = hh * lax.rsqrt(jnp.mean(hh * hh, axis=-1, keepdims=True) + EPS) * ghead_ref[:, hs]
        mix_ref[rs, hs] = (hn * _sigmoid(o)).astype(BF16)


def _prompt_layer_kernel(x_ref, mod_ref, gmix_ref, win_ref, bg_ref, ghead_ref, wdw_ref, bdw_ref,
                         lng_ref, lnb_ref, wout_ref, gffn_ref, w13_ref, w2_ref, gfin_ref,
                         y_ref, c_out, n_out, m_out, conv_out,
                         h_ref, p_ref, mix_ref, x1_ref, act_ref, ubuf_ref, C_ref, n_ref, m_ref,
                         *, tm, final):
    j = pl.program_id(1)
    last = pl.num_programs(1) - 1
    row_chunk = 32

    @pl.when(j == 0)
    def _():
        C_ref[...] = jnp.zeros_like(C_ref)
        n_ref[...] = jnp.zeros_like(n_ref)
        m_ref[...] = jnp.zeros_like(m_ref)
        ubuf_ref[0:HIST_PAD, :] = jnp.zeros((HIST_PAD, CONV_CH), F32)

    @pl.when(j > 0)
    def _():
        ubuf_ref[0:HIST_PAD, :] = ubuf_ref[tm:tm + HIST_PAD, :]

    def mod_rows(name, cs, r=None):
        i = _MOD_ROW[name]
        return mod_ref[i:i + 1, cs]

    def norm1(i, carry):
        r = pl.ds(pl.multiple_of(i * row_chunk, row_chunk), row_chunk)
        h = _rms_mod(x_ref[r, :], gmix_ref[...], mod_rows("sc1", slice(None)),
                     mod_rows("sh1", slice(None)))
        h_ref[r, :] = h.astype(BF16)
        return carry

    lax.fori_loop(0, tm // row_chunk, norm1, 0)
    _in_proj(h_ref, win_ref, p_ref)

    row_id = lax.broadcasted_iota(jnp.int32, (CHUNK, CHUNK), 0)
    col_id = lax.broadcasted_iota(jnp.int32, (CHUNK, CHUNK), 1)
    tril = row_id >= col_id
    tril_bf = jnp.where(tril, 1.0, 0.0).astype(BF16)
    for c in range(tm // CHUNK):
        _mlstm_chunk(p_ref, c * CHUNK, bg_ref, ghead_ref, mix_ref, C_ref, n_ref, m_ref,
                     tril, tril_bf)

    def glu(i, carry):
        r = pl.ds(pl.multiple_of(i * row_chunk, row_chunk), row_chunk)
        ro = pl.ds(pl.multiple_of(i * row_chunk, row_chunk) + HIST_PAD, row_chunk)
        ubuf_ref[ro, :] = p_ref[r, COL_A:COL_A + CONV_CH] * _sigmoid(p_ref[r, COL_B:COL_B + CONV_CH])
        return carry

    lax.fori_loop(0, tm // row_chunk, glu, 0)

    for t0 in range(0, tm, row_chunk):
        acc = jnp.broadcast_to(bdw_ref[...], (row_chunk, CONV_CH))
        for w in range(CONV_WIDTH):
            s0 = t0 + HIST_PAD - HIST + w
            acc = acc + ubuf_ref[s0:s0 + row_chunk, :] * wdw_ref[w:w + 1, :]
        zc = _layernorm_silu(acc, lng_ref[...], lnb_ref[...])
        mix_ref[t0:t0 + row_chunk, MLSTM_WIDTH:] = zc.astype(BF16)

    @pl.when(j == last)
    def _():
        c_out[...] = C_ref[...]
        n_out[...] = n_ref[0:HEADS, :]
        m_out[...] = m_ref[0:HEADS, :]
        conv_out[...] = ubuf_ref[tm + HIST_PAD - HIST:tm + HIST_PAD, :]

    _out_proj_ffn(x_ref, mix_ref, mod_rows, wout_ref, gffn_ref, w13_ref, w2_ref, gfin_ref,
                  y_ref, x1_ref, h_ref, act_ref, rows=tm, row_chunk=row_chunk, final=final)


def _resident(shape):
    nd = len(shape)
    return pl.BlockSpec(shape, lambda b, j: (0,) * nd, pipeline_mode=pl.Buffered(1))


def _prompt_layer(x, mod, wl, g_final, *, tm, final):
    bsz, seq, d = x.shape
    assert seq % tm == 0 and tm % CHUNK == 0 and tm >= HIST_PAD
    kern = functools.partial(_prompt_layer_kernel, tm=tm, final=final)
    in_specs = [
        pl.BlockSpec((None, tm, d), lambda b, j: (b, j, 0)),
        pl.BlockSpec((None, 6, d), lambda b, j: (b, 0, 0)),
        _resident((1, d)),
        _resident((d, PROJ_PAD)),
        _resident((1, LANES)),
        _resident((1, MLSTM_WIDTH)),
        _resident((CONV_WIDTH, CONV_CH)),
        _resident((1, CONV_CH)),
        _resident((1, CONV_CH)),
        _resident((1, CONV_CH)),
        _resident((d, d)),
        _resident((1, d)),
        _resident((d, 2 * D_FF)),
        _resident((D_FF, d)),
        _resident((1, d)),
    ]
    out_specs = [
        pl.BlockSpec((None, tm, d), lambda b, j: (b, j, 0)),
        pl.BlockSpec((None, HEADS, HEAD_DIM, HEAD_DIM), lambda b, j: (b, 0, 0, 0)),
        pl.BlockSpec((None, HEADS, HEAD_DIM), lambda b, j: (b, 0, 0)),
        pl.BlockSpec((None, HEADS, LANES), lambda b, j: (b, 0, 0)),
        pl.BlockSpec((None, HIST, CONV_CH), lambda b, j: (b, 0, 0)),
    ]
    out_shape = [
        jax.ShapeDtypeStruct((bsz, seq, d), F32),
        jax.ShapeDtypeStruct((bsz, HEADS, HEAD_DIM, HEAD_DIM), F32),
        jax.ShapeDtypeStruct((bsz, HEADS, HEAD_DIM), F32),
        jax.ShapeDtypeStruct((bsz, HEADS, LANES), F32),
        jax.ShapeDtypeStruct((bsz, HIST, CONV_CH), F32),
    ]
    scratch = [
        pltpu.VMEM((tm, d), BF16),
        pltpu.VMEM((tm, PROJ_PAD), F32),
        pltpu.VMEM((tm, d), BF16),
        pltpu.VMEM((tm, d), F32),
        pltpu.VMEM((tm, D_FF), BF16),
        pltpu.VMEM((tm + HIST_PAD, CONV_CH), F32),
        pltpu.VMEM((HEADS, HEAD_DIM, HEAD_DIM), F32),
        pltpu.VMEM((SUBLANES, HEAD_DIM), F32),
        pltpu.VMEM((SUBLANES, LANES), F32),
    ]
    return pl.pallas_call(
        kern,
        grid=(bsz, seq // tm),
        in_specs=in_specs,
        out_specs=out_specs,
        out_shape=out_shape,
        scratch_shapes=scratch,
        compiler_params=pltpu.CompilerParams(
            dimension_semantics=("arbitrary", "arbitrary"), vmem_limit_bytes=VMEM_LIMIT),
        name="prompt_layer",
    )(x, mod, wl["g_mix"], wl["w_in"], wl["b_gates"], wl["g_head"], wl["w_dw"], wl["b_dw"],
      wl["ln_g"], wl["ln_b"], wl["w_out"], wl["g_ffn"], wl["w_ffn13"], wl["w_ffn2"], g_final)


def _sample_in_kernel(x_ref, mod_ref, gmix_ref, win_ref, p_ref, h_ref):
    h = _rms_mod(x_ref[...], gmix_ref[...], mod_ref[:, D_MODEL:2 * D_MODEL], mod_ref[:, 0:D_MODEL])
    h_ref[...] = h.astype(BF16)
    _in_proj(h_ref, win_ref, p_ref)


def _sample_in_proj(x, mod, wl):
    rows = x.shape[0]
    return pl.pallas_call(
        _sample_in_kernel,
        out_shape=jax.ShapeDtypeStruct((rows, PROJ_PAD), F32),
        scratch_shapes=[pltpu.VMEM((rows, D_MODEL), BF16)],
        compiler_params=pltpu.CompilerParams(vmem_limit_bytes=VMEM_LIMIT),
        name="sample_in_proj",
    )(x, mod, wl["g_mix"], wl["w_in"])


def _sample_step_kernel(p3_ref, p2_ref, c_ref, n_ref, m_ref, conv_ref, bg_ref, ghead_ref,
                        wdw_ref, bdw_ref, lng_ref, lnb_ref,
                        hm_ref, zc_ref, c_out, n_out, m_out, conv_out, *, bt):
    scale = HEAD_DIM ** -0.5
    for b in range(bt):
        qk_t = p3_ref[b, 0:2 * HEADS, :].T
        gates = p3_ref[b, COL_G // LANES:COL_G // LANES + 1, :] + bg_ref[...]
        logf = _log_sigmoid(gates)
        m_new_row = jnp.zeros((1, LANES), F32)
        for h in range(HEADS):
            q_col = qk_t[:, h:h + 1]
            k_col = qk_t[:, HEADS + h:HEADS + h + 1] * scale
            q_row = p3_ref[b, h:h + 1, :]
            k_row = p3_ref[b, HEADS + h:HEADS + h + 1, :] * scale
            v_row = p3_ref[b, 2 * HEADS + h:2 * HEADS + h + 1, :]
            o_row = p3_ref[b, 3 * HEADS + h:3 * HEADS + h + 1, :]
            i_log = gates[:, h:h + 1]
            f_log = logf[:, HEADS + h:HEADS + h + 1]
            m_prev = m_ref[b, :, h:h + 1]
            n_row = n_ref[b, h:h + 1, :]
            c_mat = c_ref[b, h]

            inter = f_log + m_prev
            m_t = jnp.maximum(inter, i_log)
            dw = jnp.exp(i_log - m_t)
            a_inter = jnp.exp(inter - m_t)
            s = jnp.sum(q_row * k_row, axis=-1, keepdims=True) * dw
            qc = jnp.sum(q_col * c_mat, axis=0, keepdims=True)
            num = s * v_row + qc * a_inter
            den = s + a_inter * jnp.sum(q_row * n_row, axis=-1, keepdims=True)
            bound = jnp.maximum(jnp.abs(den), jnp.exp(-m_t))
            hh = num * (1.0 / bound)
            c_out[b, h] = a_inter * c_mat + (k_col * dw) * v_row
            n_out[b, h:h + 1, :] = a_inter * n_row + k_row * dw
            lane = lax.broadcasted_iota(jnp.int32, (1, LANES), 1)
            m_new_row = jnp.where(lane == h, m_t, m_new_row)
            hn = hh * lax.rsqrt(jnp.mean(hh * hh, axis=-1, keepdims=True) + EPS)
            hn = hn * ghead_ref[:, h * HEAD_DIM:(h + 1) * HEAD_DIM]
            hm_ref[b, h:h + 1, :] = hn * _sigmoid(o_row)
        m_out[b] = m_new_row

        u = p2_ref[b:b + 1, COL_A:COL_A + CONV_CH] * _sigmoid(p2_ref[b:b + 1, COL_B:COL_B + CONV_CH])
        hist = conv_ref[b]
        z = (jnp.sum(hist * wdw_ref[0:HIST, :], axis=0, keepdims=True)
             + u * wdw_ref[HIST:HIST + 1, :] + bdw_ref[...])
        zc_ref[b:b + 1, :] = _layernorm_silu(z, lng_ref[...], lnb_ref[...])
        conv_out[b, 0:HIST - 1, :] = hist[1:HIST, :]
        conv_out[b, HIST - 1:HIST, :] = u


def _sample_step(p, state_c, state_n, state_m, state_conv, wl, *, bt=8):
    rows = p.shape[0]
    assert rows % bt == 0
    p3 = p.reshape(rows, PROJ_PAD // LANES, LANES)
    m3 = jnp.pad(state_m, ((0, 0), (0, LANES - HEADS))).reshape(rows, 1, LANES)
    blk = lambda *s: pl.BlockSpec((bt,) + s, lambda i: (i,) + (0,) * len(s))
    const = lambda *s: pl.BlockSpec(s, lambda i: (0,) * len(s))
    out_shape = [
        jax.ShapeDtypeStruct((rows, HEADS, HEAD_DIM), F32),
        jax.ShapeDtypeStruct((rows, CONV_CH), F32),
        jax.ShapeDtypeStruct(state_c.shape, F32),
        jax.ShapeDtypeStruct(state_n.shape, F32),
        jax.ShapeDtypeStruct((rows, 1, LANES), F32),
        jax.ShapeDtypeStruct(state_conv.shape, F32),
    ]
    hm, zc, c_new, n_new, m_new, conv_new = pl.pallas_call(
        functools.partial(_sample_step_kernel, bt=bt),
        grid=(rows // bt,),
        in_specs=[
            blk(PROJ_PAD // LANES, LANES), blk(PROJ_PAD),
            blk(HEADS, HEAD_DIM, HEAD_DIM), blk(HEADS, HEAD_DIM), blk(1, LANES),
            blk(HIST, CONV_CH),
            const(1, LANES), const(1, MLSTM_WIDTH), const(CONV_WIDTH, CONV_CH),
            const(1, CONV_CH), const(1, CONV_CH), const(1, CONV_CH),
        ],
        out_specs=[
            blk(HEADS, HEAD_DIM), blk(CONV_CH), blk(HEADS, HEAD_DIM, HEAD_DIM),
            blk(HEADS, HEAD_DIM), blk(1, LANES), blk(HIST, CONV_CH),
        ],
        out_shape=out_shape,
        compiler_params=pltpu.CompilerParams(
            dimension_semantics=("arbitrary",), vmem_limit_bytes=VMEM_LIMIT),
        name="sample_step",
    )(p3, p, state_c, state_n, m3, state_conv, wl["b_gates"], wl["g_head"], wl["w_dw"],
      wl["b_dw"], wl["ln_g"], wl["ln_b"])
    return hm.reshape(rows, MLSTM_WIDTH), zc, c_new, n_new, m_new[:, 0, :HEADS], conv_new


def _sample_out_kernel(x_ref, hm_ref, zc_ref, mod_ref, wout_ref, gffn_ref, w13_ref, w2_ref,
                       gfin_ref, y_ref, mix_ref, x1_ref, h2_ref, act_ref, *, rows, final):
    mix_ref[:, 0:MLSTM_WIDTH] = hm_ref[...].astype(BF16)
    mix_ref[:, MLSTM_WIDTH:] = zc_ref[...].astype(BF16)

    def mod_rows(name, cs, r=slice(None)):
        i = _MOD_ROW[name]
        if isinstance(cs, slice) and cs == slice(None):
            cs = slice(0, D_MODEL)
        return mod_ref[r, i * D_MODEL + cs.start:i * D_MODEL + cs.stop]

    _out_proj_ffn(x_ref, mix_ref, mod_rows, wout_ref, gffn_ref, w13_ref, w2_ref, gfin_ref,
                  y_ref, x1_ref, h2_ref, act_ref, rows=rows, row_chunk=32, final=final)


def _sample_out(x, hm, zc, mod, wl, g_final, *, final):
    rows = x.shape[0]
    return pl.pallas_call(
        functools.partial(_sample_out_kernel, rows=rows, final=final),
        out_shape=jax.ShapeDtypeStruct((rows, D_MODEL), F32),
        scratch_shapes=[
            pltpu.VMEM((rows, D_MODEL), BF16),
            pltpu.VMEM((rows, D_MODEL), F32),
            pltpu.VMEM((rows, D_MODEL), BF16),
            pltpu.VMEM((rows, D_FF), BF16),
        ],
        compiler_params=pltpu.CompilerParams(vmem_limit_bytes=VMEM_LIMIT),
        name="sample_out",
    )(x, hm, zc, mod, wl["w_out"], wl["g_ffn"], wl["w_ffn13"], wl["w_ffn2"], g_final)


def _prep_layer_weights(l, g_mix, w_in, b_gates, g_head, w_dw, b_dw, ln_g, ln_b, w_out, g_ffn,
                        w_ffn13, w_ffn2):
    d = w_in.shape[1]
    n_gate = 2 * HEADS
    gate0 = 4 * MLSTM_WIDTH
    w = w_in[l]
    w_r = jnp.concatenate(
        [w[:, :gate0], w[:, gate0 + n_gate:], w[:, gate0:gate0 + n_gate],
         jnp.zeros((d, LANES - n_gate), w.dtype)], axis=1).astype(BF16)
    return {
        "g_mix": g_mix[l].reshape(1, d),
        "w_in": w_r,
        "b_gates": jnp.pad(b_gates[l], (0, LANES - n_gate)).reshape(1, LANES),
        "g_head": g_head[l].reshape(1, MLSTM_WIDTH),
        "w_dw": w_dw[l],
        "b_dw": b_dw[l].reshape(1, CONV_CH),
        "ln_g": ln_g[l].reshape(1, CONV_CH),
        "ln_b": ln_b[l].reshape(1, CONV_CH),
        "w_out": w_out[l].astype(BF16),
        "g_ffn": g_ffn[l].reshape(1, d),
        "w_ffn13": w_ffn13[l].astype(BF16),
        "w_ffn2": w_ffn2[l].astype(BF16),
    }


def _forward(x_prompt, x_sample, c_prompt, c_sample, state_C, state_n, state_m, state_conv,
             w_ada, b_ada, g_mix, w_in, b_gates, g_head, w_dw, b_dw, ln_g, ln_b, w_out,
             g_ffn, w_ffn13, w_ffn2, g_final, *, tm):
    depth = w_ada.shape[0]
    bp = x_prompt.shape[0]
    d = x_prompt.shape[-1]
    mod = _modulation(jnp.concatenate([c_prompt, c_sample], axis=0), w_ada, b_ada)
    gfin = g_final.reshape(1, d)
    xp = x_prompt
    xs = x_sample.reshape(x_sample.shape[0], d)
    outs_p = [[], [], [], []]
    outs_s = [[], [], [], []]
    for l in range(depth):
        wl = _prep_layer_weights(l, g_mix, w_in, b_gates, g_head, w_dw, b_dw, ln_g, ln_b, w_out,
                                 g_ffn, w_ffn13, w_ffn2)
        final = l == depth - 1
        mod_p = mod[l, :bp].reshape(bp, 6, d)
        mod_s = mod[l, bp:]
        xp, c1, n1, m1, b1 = _prompt_layer(xp, mod_p, wl, gfin, tm=tm, final=final)
        for acc, val in zip(outs_p, (c1, n1, m1[:, :, 0], b1)):
            acc.append(val)
        p = _sample_in_proj(xs, mod_s, wl)
        hm, zc, c2, n2, m2, b2 = _sample_step(p, state_C[l], state_n[l], state_m[l],
                                              state_conv[l], wl)
        xs = _sample_out(xs, hm, zc, mod_s, wl, gfin, final=final)
        for acc, val in zip(outs_s, (c2, n2, m2, b2)):
            acc.append(val)
    y_sample = xs.reshape(x_sample.shape)
    return (xp, y_sample) + tuple(jnp.stack(a) for a in outs_p) + tuple(jnp.stack(a) for a in outs_s)


def kernel(x_prompt, x_sample, c_prompt, c_sample, state_C, state_n, state_m, state_conv, w_ada, b_ada, g_mix, w_in, b_gates, g_head, w_dw, b_dw, ln_g, ln_b, w_out, g_ffn, w_ffn13, w_ffn2, g_final):
    return _forward(x_prompt, x_sample, c_prompt, c_sample, state_C, state_n, state_m, state_conv,
                    w_ada, b_ada, g_mix, w_in, b_gates, g_head, w_dw, b_dw, ln_g, ln_b, w_out,
                    g_ffn, w_ffn13, w_ffn2, g_final, tm=256)
```

```python
import functools

import jax
import jax.numpy as jnp
from jax import lax
from jax.experimental import pallas as pl
from jax.experimental.pallas import tpu as pltpu

F32 = jnp.float32
BF16 = jnp.bfloat16

D_MODEL = 1024
HEADS = 4
HEAD_DIM = 128
MLSTM_WIDTH = HEADS * HEAD_DIM
CONV_CH = D_MODEL - MLSTM_WIDTH
CONV_WIDTH = 31
HIST = CONV_WIDTH - 1
D_FF = 2816
CHUNK = 128
EPS = 1e-6

LANES = 128
SUBLANES = 8
MXU_N = 256

COL_Q = 0
COL_K = MLSTM_WIDTH
COL_V = 2 * MLSTM_WIDTH
COL_O = 3 * MLSTM_WIDTH
COL_A = 4 * MLSTM_WIDTH
COL_B = COL_A + CONV_CH
COL_G = COL_B + CONV_CH
PROJ_PAD = COL_G + LANES
HIST_PAD = 32
CONV_SLABS = CONV_CH // LANES
CONV_STRIDE = 2
VMEM_LIMIT = 60 * 1024 * 1024


def _log_sigmoid(x):
    return jnp.minimum(x, 0.0) - jnp.log1p(jnp.exp(-jnp.abs(x)))


def _sigmoid(x):
    return 1.0 / (1.0 + jnp.exp(-x))


def _bdot(a, b):
    return jnp.dot(a, b, preferred_element_type=F32)


def _bdot_nt(a, b):
    return lax.dot_general(a, b, (((1,), (1,)), ((), ())), preferred_element_type=F32)


def _row_slices(rows, chunk):
    chunk = min(chunk, rows)
    assert rows % chunk == 0
    return [slice(r0, r0 + chunk) for r0 in range(0, rows, chunk)]


def _rms_mod(x, g, sc, sh):
    ms = jnp.mean(x * x, axis=-1, keepdims=True)
    return (x * lax.rsqrt(ms + EPS) * g) * (1.0 + sc) + sh


def _mod_kernel(c_ref, w_ref, b_ref, o_ref):
    o_ref[...] = _bdot(c_ref[...].astype(BF16), w_ref[...].astype(BF16)) + b_ref[...]


def _modulation(c_all, w_ada, b_ada):
    depth, d, n6 = w_ada.shape
    rows = c_all.shape[0]
    tn = 1024
    return pl.pallas_call(
        _mod_kernel,
        grid=(depth, n6 // tn),
        in_specs=[
            pl.BlockSpec((rows, d), lambda l, n: (0, 0)),
            pl.BlockSpec((None, d, tn), lambda l, n: (l, 0, n)),
            pl.BlockSpec((None, 1, tn), lambda l, n: (l, 0, n)),
        ],
        out_specs=pl.BlockSpec((None, rows, tn), lambda l, n: (l, 0, n)),
        out_shape=jax.ShapeDtypeStruct((depth, rows, n6), F32),
        compiler_params=pltpu.CompilerParams(
            dimension_semantics=("arbitrary", "arbitrary"), vmem_limit_bytes=VMEM_LIMIT),
        name="adaln_mod",
    )(c_all, w_ada, b_ada.reshape(depth, 1, n6))


def _in_proj(h_ref, win_ref, p_ref):
    for n0 in range(0, PROJ_PAD, MXU_N):
        n1 = min(n0 + MXU_N, PROJ_PAD)
        p_ref[:, n0:n1] = _bdot(h_ref[...], win_ref[:, n0:n1])


def _out_proj_ffn(x_ref, mix_ref, mod_rows, wout_ref, gffn_ref, w13_ref, w2_ref, gfin_ref,
                  y_ref, x1_ref, h2_ref, act_ref, *, rows, row_chunk, final):
    for n0 in range(0, D_MODEL, MXU_N):
        cs = slice(n0, n0 + MXU_N)
        mx = _bdot(mix_ref[...], wout_ref[:, cs])
        x1_ref[:, cs] = x_ref[:, cs] + mod_rows("ga1", cs) * mx

    for r in _row_slices(rows, row_chunk):
        h2 = _rms_mod(x1_ref[r, :], gffn_ref[...], mod_rows("sc2", slice(None), r),
                      mod_rows("sh2", slice(None), r))
        h2_ref[r, :] = h2.astype(BF16)

    for n0 in range(0, D_FF, MXU_N):
        gg = _bdot(h2_ref[...], w13_ref[:, n0:n0 + MXU_N])
        uu = _bdot(h2_ref[...], w13_ref[:, D_FF + n0:D_FF + n0 + MXU_N])
        act_ref[:, n0:n0 + MXU_N] = (gg * _sigmoid(gg) * uu).astype(BF16)

    dst = x1_ref if final else y_ref
    for n0 in range(0, D_MODEL, MXU_N):
        cs = slice(n0, n0 + MXU_N)
        ff = _bdot(act_ref[...], w2_ref[:, cs])
        dst[:, cs] = x1_ref[:, cs] + mod_rows("ga2", cs) * ff

    if final:
        for r in _row_slices(rows, row_chunk):
            xb = x1_ref[r, :]
            ms = jnp.mean(xb * xb, axis=-1, keepdims=True)
            y_ref[r, :] = xb * lax.rsqrt(ms + EPS) * gfin_ref[...]


def _layernorm_silu(z, lng, lnb):
    mu = jnp.mean(z, axis=-1, keepdims=True)
    zc = z - mu
    var = jnp.mean(zc * zc, axis=-1, keepdims=True)
    zn = zc * lax.rsqrt(var + EPS) * lng + lnb
    return zn * _sigmoid(zn)


_MOD_ROW = {"sh1": 0, "sc1": 1, "ga1": 2, "sh2": 3, "sc2": 4, "ga2": 5}


def _cumsum_chunk(tril_bf, x):
    hi = x.astype(BF16)
    r1 = x - hi.astype(F32)
    mid = r1.astype(BF16)
    lo = (r1 - mid.astype(F32)).astype(BF16)
    return _bdot(tril_bf, hi) + _bdot(tril_bf, mid) + _bdot(tril_bf, lo)


def _mlstm_chunk(p_ref, r0, bg_ref, ghead_ref, mix_ref, C_ref, n_ref, m_ref, tril, tril_bf):
    rs = slice(r0, r0 + CHUNK)
    gates = p_ref[rs, COL_G:COL_G + LANES] + bg_ref[...]
    fcum = _cumsum_chunk(tril_bf, _log_sigmoid(gates))
    gates_t = gates.T
    fcum_t = fcum.T
    for h in range(HEADS):
        hs = slice(h * HEAD_DIM, (h + 1) * HEAD_DIM)
        q = p_ref[rs, COL_Q + h * HEAD_DIM:COL_Q + (h + 1) * HEAD_DIM]
        k = p_ref[rs, COL_K + h * HEAD_DIM:COL_K + (h + 1) * HEAD_DIM] * (HEAD_DIM ** -0.5)
        v = p_ref[rs, COL_V + h * HEAD_DIM:COL_V + (h + 1) * HEAD_DIM]
        o = p_ref[rs, COL_O + h * HEAD_DIM:COL_O + (h + 1) * HEAD_DIM]
        q_bf = q.astype(BF16)
        v_bf = v.astype(BF16)
        f_col = fcum[:, HEADS + h:HEADS + h + 1]
        i_col = gates[:, h:h + 1]
        f_row = fcum_t[HEADS + h:HEADS + h + 1, :]
        i_row = gates_t[h:h + 1, :]
        m_prev = m_ref[h:h + 1, 0:1]
        n_row = n_ref[h:h + 1, :]
        c_mat = C_ref[h]

        g = f_row[:, CHUNK - 1:CHUNK]
        dlog = jnp.where(tril, f_col - f_row + i_row, -jnp.inf)
        inter = f_col + m_prev
        m_t = jnp.maximum(inter, jnp.max(dlog, axis=-1, keepdims=True))
        dw = jnp.exp(dlog - m_t)
        a_inter = jnp.exp(inter - m_t)
        s = _bdot_nt(q_bf, k.astype(BF16)) * dw
        num = _bdot(s.astype(BF16), v_bf) + _bdot(q_bf, c_mat.astype(BF16)) * a_inter
        qn = jnp.sum(q * n_row, axis=-1, keepdims=True)
        den = jnp.sum(s, axis=-1, keepdims=True) + a_inter * qn
        bound = jnp.maximum(jnp.abs(den), jnp.exp(-m_t))
        hh = num * (1.0 / bound)

        wmax = jnp.max(g - f_row + i_row, axis=-1, keepdims=True)
        m_new = jnp.maximum(g + m_prev, wmax)
        wj = jnp.exp(g - f_col + i_col - m_new)
        decay = jnp.exp(g + m_prev - m_new)
        kw = k * wj
        C_ref[h] = decay * c_mat + _bdot(kw.T.astype(BF16), v_bf)
        n_ref[h:h + 1, :] = decay * n_row + jnp.sum(kw, axis=0, keepdims=True)
        m_ref[h:h + 1, :] = jnp.broadcast_to(m_new, (1, LANES))

        hn = hh * lax.rsqrt(jnp.mean(hh * hh, axis=-1, keepdims=True) + EPS) * ghead_ref[:, hs]
        mix_ref[rs, hs] = (hn * _sigmoid(o)).astype(BF16)


def _prompt_layer_kernel(x_ref, mod_ref, gmix_ref, win_ref, bg_ref, ghead_ref, wdw_ref, bdw_ref,
                         lng_ref, lnb_ref, wout_ref, gffn_ref, w13_ref, w2_ref, gfin_ref,
                         y_ref, c_out, n_out, m_out, conv_out,
                         h_ref, p_ref, mix_ref, x1_ref, act_ref, ubuf_ref, zc_ref, C_ref, n_ref,
                         m_ref, *, tm, final):
    j = pl.program_id(1)
    last = pl.num_programs(1) - 1
    row_chunk = 32

    @pl.when(j == 0)
    def _():
        C_ref[...] = jnp.zeros_like(C_ref)
        n_ref[...] = jnp.zeros_like(n_ref)
        m_ref[...] = jnp.zeros_like(m_ref)
        ubuf_ref[:, 0:HIST_PAD, :] = jnp.zeros((CONV_SLABS, HIST_PAD, LANES), F32)

    @pl.when(j > 0)
    def _():
        ubuf_ref[:, 0:HIST_PAD, :] = ubuf_ref[:, tm:tm + HIST_PAD, :]

    def mod_rows(name, cs, r=None):
        i = _MOD_ROW[name]
        return mod_ref[i:i + 1, cs]

    for r in _row_slices(tm, row_chunk):
        h = _rms_mod(x_ref[r, :], gmix_ref[...], mod_rows("sc1", slice(None)),
                     mod_rows("sh1", slice(None)))
        h_ref[r, :] = h.astype(BF16)
    _in_proj(h_ref, win_ref, p_ref)

    row_id = lax.broadcasted_iota(jnp.int32, (CHUNK, CHUNK), 0)
    col_id = lax.broadcasted_iota(jnp.int32, (CHUNK, CHUNK), 1)
    tril = row_id >= col_id
    tril_bf = jnp.where(tril, 1.0, 0.0).astype(BF16)
    for c in range(tm // CHUNK):
        _mlstm_chunk(p_ref, c * CHUNK, bg_ref, ghead_ref, mix_ref, C_ref, n_ref, m_ref,
                     tril, tril_bf)

    for r in _row_slices(tm, row_chunk):
        ro = slice(r.start + HIST_PAD, r.stop + HIST_PAD)
        u = p_ref[r, COL_A:COL_A + CONV_CH] * _sigmoid(p_ref[r, COL_B:COL_B + CONV_CH])
        for g in range(CONV_SLABS):
            ubuf_ref[g, ro, :] = u[:, g * LANES:(g + 1) * LANES]

    n_sets = row_chunk // SUBLANES
    firsts = [(t0 // CONV_STRIDE) * CONV_STRIDE * SUBLANES + t0 % CONV_STRIDE
              for t0 in range(n_sets)]
    for g in range(CONV_SLABS):
        gs = slice(g * LANES, (g + 1) * LANES)
        taps = [jnp.broadcast_to(wdw_ref[w:w + 1, gs], (SUBLANES, LANES)) for w in range(CONV_WIDTH)]
        bias = jnp.broadcast_to(bdw_ref[:, gs], (SUBLANES, LANES))

        def conv(i, carry, g=g, taps=taps, bias=bias):
            base = pl.multiple_of(i * row_chunk, row_chunk)
            acc = [bias for _ in range(n_sets)]
            for w in range(CONV_WIDTH):
                for t0 in range(n_sets):
                    src = pl.ds(base + (firsts[t0] + HIST_PAD - HIST + w), SUBLANES,
                                stride=CONV_STRIDE)
                    acc[t0] = acc[t0] + ubuf_ref[g, src, :] * taps[w]
            for t0 in range(n_sets):
                zc_ref[g, pl.ds(base + firsts[t0], SUBLANES, stride=CONV_STRIDE), :] = acc[t0]
            return carry

        lax.fori_loop(0, tm // row_chunk, conv, 0)

    for r in _row_slices(tm, row_chunk):
        z = jnp.concatenate([zc_ref[g, r, :] for g in range(CONV_SLABS)], axis=-1)
        mix_ref[r, MLSTM_WIDTH:] = _layernorm_silu(z, lng_ref[...], lnb_ref[...]).astype(BF16)

    @pl.when(j == last)
    def _():
        c_out[...] = C_ref[...]
        n_out[...] = n_ref[0:HEADS, :]
        m_out[...] = m_ref[0:HEADS, :]
        for g in range(CONV_SLABS):
            conv_out[:, g * LANES:(g + 1) * LANES] = ubuf_ref[g, tm + HIST_PAD - HIST:tm + HIST_PAD, :]

    _out_proj_ffn(x_ref, mix_ref, mod_rows, wout_ref, gffn_ref, w13_ref, w2_ref, gfin_ref,
                  y_ref, x1_ref, h_ref, act_ref, rows=tm, row_chunk=row_chunk, final=final)


def _resident(shape):
    nd = len(shape)
    return pl.BlockSpec(shape, lambda b, j: (0,) * nd, pipeline_mode=pl.Buffered(1))


def _prompt_layer(x, mod, wl, g_final, *, tm, final):
    bsz, seq, d = x.shape
    assert seq % tm == 0 and tm % CHUNK == 0 and tm >= HIST_PAD
    kern = functools.partial(_prompt_layer_kernel, tm=tm, final=final)
    in_specs = [
        pl.BlockSpec((None, tm, d), lambda b, j: (b, j, 0)),
        pl.BlockSpec((None, 6, d), lambda b, j: (b, 0, 0)),
        _resident((1, d)),
        _resident((d, PROJ_PAD)),
        _resident((1, LANES)),
        _resident((1, MLSTM_WIDTH)),
        _resident((CONV_WIDTH, CONV_CH)),
        _resident((1, CONV_CH)),
        _resident((1, CONV_CH)),
        _resident((1, CONV_CH)),
        _resident((d, d)),
        _resident((1, d)),
        _resident((d, 2 * D_FF)),
        _resident((D_FF, d)),
        _resident((1, d)),
    ]
    out_specs = [
        pl.BlockSpec((None, tm, d), lambda b, j: (b, j, 0)),
        pl.BlockSpec((None, HEADS, HEAD_DIM, HEAD_DIM), lambda b, j: (b, 0, 0, 0)),
        pl.BlockSpec((None, HEADS, HEAD_DIM), lambda b, j: (b, 0, 0)),
        pl.BlockSpec((None, HEADS, LANES), lambda b, j: (b, 0, 0)),
        pl.BlockSpec((None, HIST, CONV_CH), lambda b, j: (b, 0, 0)),
    ]
    out_shape = [
        jax.ShapeDtypeStruct((bsz, seq, d), F32),
        jax.ShapeDtypeStruct((bsz, HEADS, HEAD_DIM, HEAD_DIM), F32),
        jax.ShapeDtypeStruct((bsz, HEADS, HEAD_DIM), F32),
        jax.ShapeDtypeStruct((bsz, HEADS, LANES), F32),
        jax.ShapeDtypeStruct((bsz, HIST, CONV_CH), F32),
    ]
    scratch = [
        pltpu.VMEM((tm, d), BF16),
        pltpu.VMEM((tm, PROJ_PAD), F32),
        pltpu.VMEM((tm, d), BF16),
        pltpu.VMEM((tm, d), F32),
        pltpu.VMEM((tm, D_FF), BF16),
        pltpu.VMEM((CONV_SLABS, tm + HIST_PAD, LANES), F32),
        pltpu.VMEM((CONV_SLABS, tm, LANES), F32),
        pltpu.VMEM((HEADS, HEAD_DIM, HEAD_DIM), F32),
        pltpu.VMEM((SUBLANES, HEAD_DIM), F32),
        pltpu.VMEM((SUBLANES, LANES), F32),
    ]
    return pl.pallas_call(
        kern,
        grid=(bsz, seq // tm),
        in_specs=in_specs,
        out_specs=out_specs,
        out_shape=out_shape,
        scratch_shapes=scratch,
        compiler_params=pltpu.CompilerParams(
            dimension_semantics=("arbitrary", "arbitrary"), vmem_limit_bytes=VMEM_LIMIT),
        name="prompt_layer",
    )(x, mod, wl["g_mix"], wl["w_in"], wl["b_gates"], wl["g_head"], wl["w_dw"], wl["b_dw"],
      wl["ln_g"], wl["ln_b"], wl["w_out"], wl["g_ffn"], wl["w_ffn13"], wl["w_ffn2"], g_final)


def _sample_in_kernel(x_ref, mod_ref, gmix_ref, win_ref, p_ref, h_ref):
    h = _rms_mod(x_ref[...], gmix_ref[...], mod_ref[:, D_MODEL:2 * D_MODEL], mod_ref[:, 0:D_MODEL])
    h_ref[...] = h.astype(BF16)
    _in_proj(h_ref, win_ref, p_ref)


def _sample_in_proj(x, mod, wl):
    rows = x.shape[0]
    return pl.pallas_call(
        _sample_in_kernel,
        out_shape=jax.ShapeDtypeStruct((rows, PROJ_PAD), F32),
        scratch_shapes=[pltpu.VMEM((rows, D_MODEL), BF16)],
        compiler_params=pltpu.CompilerParams(vmem_limit_bytes=VMEM_LIMIT),
        name="sample_in_proj",
    )(x, mod, wl["g_mix"], wl["w_in"])


def _sample_step_kernel(p3_ref, p2_ref, c_ref, n_ref, m_ref, conv_ref, bg_ref, ghead_ref,
                        wdw_ref, bdw_ref, lng_ref, lnb_ref,
                        hm_ref, zc_ref, c_out, n_out, m_out, conv_out, *, bt):
    scale = HEAD_DIM ** -0.5
    for b in range(bt):
        qk_t = p3_ref[b, 0:2 * HEADS, :].T
        gates = p3_ref[b, COL_G // LANES:COL_G // LANES + 1, :] + bg_ref[...]
        logf = _log_sigmoid(gates)
        m_new_row = jnp.zeros((1, LANES), F32)
        for h in range(HEADS):
            q_col = qk_t[:, h:h + 1]
            k_col = qk_t[:, HEADS + h:HEADS + h + 1] * scale
            q_row = p3_ref[b, h:h + 1, :]
            k_row = p3_ref[b, HEADS + h:HEADS + h + 1, :] * scale
            v_row = p3_ref[b, 2 * HEADS + h:2 * HEADS + h + 1, :]
            o_row = p3_ref[b, 3 * HEADS + h:3 * HEADS + h + 1, :]
            i_log = gates[:, h:h + 1]
            f_log = logf[:, HEADS + h:HEADS + h + 1]
            m_prev = m_ref[b, :, h:h + 1]
            n_row = n_ref[b, h:h + 1, :]
            c_mat = c_ref[b, h]

            inter = f_log + m_prev
            m_t = jnp.maximum(inter, i_log)
            dw = jnp.exp(i_log - m_t)
            a_inter = jnp.exp(inter - m_t)
            s = jnp.sum(q_row * k_row, axis=-1, keepdims=True) * dw
            qc = jnp.sum(q_col * c_mat, axis=0, keepdims=True)
            num = s * v_row + qc * a_inter
            den = s + a_inter * jnp.sum(q_row * n_row, axis=-1, keepdims=True)
            bound = jnp.maximum(jnp.abs(den), jnp.exp(-m_t))
            hh = num * (1.0 / bound)
            c_out[b, h] = a_inter * c_mat + (k_col * dw) * v_row
            n_out[b, h:h + 1, :] = a_inter * n_row + k_row * dw
            lane = lax.broadcasted_iota(jnp.int32, (1, LANES), 1)
            m_new_row = jnp.where(lane == h, m_t, m_new_row)
            hn = hh * lax.rsqrt(jnp.mean(hh * hh, axis=-1, keepdims=True) + EPS)
            hn = hn * ghead_ref[:, h * HEAD_DIM:(h + 1) * HEAD_DIM]
            hm_ref[b, h:h + 1, :] = hn * _sigmoid(o_row)
        m_out[b] = m_new_row

        u = p2_ref[b:b + 1, COL_A:COL_A + CONV_CH] * _sigmoid(p2_ref[b:b + 1, COL_B:COL_B + CONV_CH])
        hist = conv_ref[b]
        z = (jnp.sum(hist * wdw_ref[0:HIST, :], axis=0, keepdims=True)
             + u * wdw_ref[HIST:HIST + 1, :] + bdw_ref[...])
        zc_ref[b:b + 1, :] = _layernorm_silu(z, lng_ref[...], lnb_ref[...])
        conv_out[b, 0:HIST - 1, :] = hist[1:HIST, :]
        conv_out[b, HIST - 1:HIST, :] = u


def _sample_step(p, state_c, state_n, state_m, state_conv, wl, *, bt=8):
    rows = p.shape[0]
    assert rows % bt == 0
    p3 = p.reshape(rows, PROJ_PAD // LANES, LANES)
    m3 = jnp.pad(state_m, ((0, 0), (0, LANES - HEADS))).reshape(rows, 1, LANES)
    blk = lambda *s: pl.BlockSpec((bt,) + s, lambda i: (i,) + (0,) * len(s))
    const = lambda *s: pl.BlockSpec(s, lambda i: (0,) * len(s))
    out_shape = [
        jax.ShapeDtypeStruct((rows, HEADS, HEAD_DIM), F32),
        jax.ShapeDtypeStruct((rows, CONV_CH), F32),
        jax.ShapeDtypeStruct(state_c.shape, F32),
        jax.ShapeDtypeStruct(state_n.shape, F32),
        jax.ShapeDtypeStruct((rows, 1, LANES), F32),
        jax.ShapeDtypeStruct(state_conv.shape, F32),
    ]
    hm, zc, c_new, n_new, m_new, conv_new = pl.pallas_call(
        functools.partial(_sample_step_kernel, bt=bt),
        grid=(rows // bt,),
        in_specs=[
            blk(PROJ_PAD // LANES, LANES), blk(PROJ_PAD),
            blk(HEADS, HEAD_DIM, HEAD_DIM), blk(HEADS, HEAD_DIM), blk(1, LANES),
            blk(HIST, CONV_CH),
            const(1, LANES), const(1, MLSTM_WIDTH), const(CONV_WIDTH, CONV_CH),
            const(1, CONV_CH), const(1, CONV_CH), const(1, CONV_CH),
        ],
        out_specs=[
            blk(HEADS, HEAD_DIM), blk(CONV_CH), blk(HEADS, HEAD_DIM, HEAD_DIM),
            blk(HEADS, HEAD_DIM), blk(1, LANES), blk(HIST, CONV_CH),
        ],
        out_shape=out_shape,
        compiler_params=pltpu.CompilerParams(
            dimension_semantics=("arbitrary",), vmem_limit_bytes=VMEM_LIMIT),
        name="sample_step",
    )(p3, p, state_c, state_n, m3, state_conv, wl["b_gates"], wl["g_head"], wl["w_dw"],
      wl["b_dw"], wl["ln_g"], wl["ln_b"])
    return hm.reshape(rows, MLSTM_WIDTH), zc, c_new, n_new, m_new[:, 0, :HEADS], conv_new


def _sample_out_kernel(x_ref, hm_ref, zc_ref, mod_ref, wout_ref, gffn_ref, w13_ref, w2_ref,
                       gfin_ref, y_ref, mix_ref, x1_ref, h2_ref, act_ref, *, rows, final):
    mix_ref[:, 0:MLSTM_WIDTH] = hm_ref[...].astype(BF16)
    mix_ref[:, MLSTM_WIDTH:] = zc_ref[...].astype(BF16)

    def mod_rows(name, cs, r=slice(None)):
        i = _MOD_ROW[name]
        if isinstance(cs, slice) and cs == slice(None):
            cs = slice(0, D_MODEL)
        return mod_ref[r, i * D_MODEL + cs.start:i * D_MODEL + cs.stop]

    _out_proj_ffn(x_ref, mix_ref, mod_rows, wout_ref, gffn_ref, w13_ref, w2_ref, gfin_ref,
                  y_ref, x1_ref, h2_ref, act_ref, rows=rows, row_chunk=32, final=final)


def _sample_out(x, hm, zc, mod, wl, g_final, *, final):
    rows = x.shape[0]
    return pl.pallas_call(
        functools.partial(_sample_out_kernel, rows=rows, final=final),
        out_shape=jax.ShapeDtypeStruct((rows, D_MODEL), F32),
        scratch_shapes=[
            pltpu.VMEM((rows, D_MODEL), BF16),
            pltpu.VMEM((rows, D_MODEL), F32),
            pltpu.VMEM((rows, D_MODEL), BF16),
            pltpu.VMEM((rows, D_FF), BF16),
        ],
        compiler_params=pltpu.CompilerParams(vmem_limit_bytes=VMEM_LIMIT),
        name="sample_out",
    )(x, hm, zc, mod, wl["w_out"], wl["g_ffn"], wl["w_ffn13"], wl["w_ffn2"], g_final)


def _prep_layer_weights(l, g_mix, w_in, b_gates, g_head, w_dw, b_dw, ln_g, ln_b, w_out, g_ffn,
                        w_ffn13, w_ffn2):
    d = w_in.shape[1]
    n_gate = 2 * HEADS
    gate0 = 4 * MLSTM_WIDTH
    w = w_in[l]
    w_r = jnp.concatenate(
        [w[:, :gate0], w[:, gate0 + n_gate:], w[:, gate0:gate0 + n_gate],
         jnp.zeros((d, LANES - n_gate), w.dtype)], axis=1).astype(BF16)
    return {
        "g_mix": g_mix[l].reshape(1, d),
        "w_in": w_r,
        "b_gates": jnp.pad(b_gates[l], (0, LANES - n_gate)).reshape(1, LANES),
        "g_head": g_head[l].reshape(1, MLSTM_WIDTH),
        "w_dw": w_dw[l],
        "b_dw": b_dw[l].reshape(1, CONV_CH),
        "ln_g": ln_g[l].reshape(1, CONV_CH),
        "ln_b": ln_b[l].reshape(1, CONV_CH),
        "w_out": w_out[l].astype(BF16),
        "g_ffn": g_ffn[l].reshape(1, d),
        "w_ffn13": w_ffn13[l].astype(BF16),
        "w_ffn2": w_ffn2[l].astype(BF16),
    }


def _forward(x_prompt, x_sample, c_prompt, c_sample, state_C, state_n, state_m, state_conv,
             w_ada, b_ada, g_mix, w_in, b_gates, g_head, w_dw, b_dw, ln_g, ln_b, w_out,
             g_ffn, w_ffn13, w_ffn2, g_final, *, tm):
    depth = w_ada.shape[0]
    bp = x_prompt.shape[0]
    d = x_prompt.shape[-1]
    mod = _modulation(jnp.concatenate([c_prompt, c_sample], axis=0), w_ada, b_ada)
    gfin = g_final.reshape(1, d)
    xp = x_prompt
    xs = x_sample.reshape(x_sample.shape[0], d)
    outs_p = [[], [], [], []]
    outs_s = [[], [], [], []]
    for l in range(depth):
        wl = _prep_layer_weights(l, g_mix, w_in, b_gates, g_head, w_dw, b_dw, ln_g, ln_b, w_out,
                                 g_ffn, w_ffn13, w_ffn2)
        final = l == depth - 1
        mod_p = mod[l, :bp].reshape(bp, 6, d)
        mod_s = mod[l, bp:]
        xp, c1, n1, m1, b1 = _prompt_layer(xp, mod_p, wl, gfin, tm=tm, final=final)
        for acc, val in zip(outs_p, (c1, n1, m1[:, :, 0], b1)):
            acc.append(val)
        p = _sample_in_proj(xs, mod_s, wl)
        hm, zc, c2, n2, m2, b2 = _sample_step(p, state_C[l], state_n[l], state_m[l],
                                              state_conv[l], wl)
        xs = _sample_out(xs, hm, zc, mod_s, wl, gfin, final=final)
        for acc, val in zip(outs_s, (c2, n2, m2, b2)):
            acc.append(val)
    y_sample = xs.reshape(x_sample.shape)
    return (xp, y_sample) + tuple(jnp.stack(a) for a in outs_p) + tuple(jnp.stack(a) for a in outs_s)


def kernel(x_prompt, x_sample, c_prompt, c_sample, state_C, state_n, state_m, state_conv, w_ada, b_ada, g_mix, w_in, b_gates, g_head, w_dw, b_dw, ln_g, ln_b, w_out, g_ffn, w_ffn13, w_ffn2, g_final):
    return _forward(x_prompt, x_sample, c_prompt, c_sample, state_C, state_n, state_m, state_conv,
                    w_ada, b_ada, g_mix, w_in, b_gates, g_head, w_dw, b_dw, ln_g, ln_b, w_out,
                    g_ffn, w_ffn13, w_ffn2, g_final, tm=256)
```

```python
import functools

import jax
import jax.numpy as jnp
from jax import lax
from jax.experimental import pallas as pl
from jax.experimental.pallas import tpu as pltpu

F32 = jnp.float32
BF16 = jnp.bfloat16

D_MODEL = 1024
HEADS = 4
HEAD_DIM = 128
MLSTM_WIDTH = HEADS * HEAD_DIM
CONV_CH = D_MODEL - MLSTM_WIDTH
CONV_WIDTH = 31
HIST = CONV_WIDTH - 1
D_FF = 2816
CHUNK = 128
EPS = 1e-6

LANES = 128
SUBLANES = 8
MXU_N = 256

COL_Q = 0
COL_K = MLSTM_WIDTH
COL_V = 2 * MLSTM_WIDTH
COL_O = 3 * MLSTM_WIDTH
COL_A = 4 * MLSTM_WIDTH
COL_B = COL_A + CONV_CH
COL_G = COL_B + CONV_CH
PROJ_PAD = COL_G + LANES
HIST_PAD = 32
CONV_SLABS = CONV_CH // LANES
CONV_STRIDE = 2
VMEM_LIMIT = 60 * 1024 * 1024


def _log_sigmoid(x):
    return jnp.minimum(x, 0.0) - jnp.log1p(jnp.exp(-jnp.abs(x)))


def _sigmoid(x):
    return 1.0 / (1.0 + jnp.exp(-x))


def _bdot(a, b):
    return jnp.dot(a, b, preferred_element_type=F32)


def _bdot_nt(a, b):
    return lax.dot_general(a, b, (((1,), (1,)), ((), ())), preferred_element_type=F32)


def _row_slices(rows, chunk):
    chunk = min(chunk, rows)
    assert rows % chunk == 0
    return [slice(r0, r0 + chunk) for r0 in range(0, rows, chunk)]


def _rms_mod(x, g, sc, sh):
    ms = jnp.mean(x * x, axis=-1, keepdims=True)
    return (x * lax.rsqrt(ms + EPS) * g) * (1.0 + sc) + sh


def _mod_kernel(c_ref, w_ref, b_ref, o_ref):
    o_ref[...] = _bdot(c_ref[...].astype(BF16), w_ref[...].astype(BF16)) + b_ref[...]


def _modulation(c_all, w_ada, b_ada):
    depth, d, n6 = w_ada.shape
    rows = c_all.shape[0]
    tn = 1024
    return pl.pallas_call(
        _mod_kernel,
        grid=(depth, n6 // tn),
        in_specs=[
            pl.BlockSpec((rows, d), lambda l, n: (0, 0)),
            pl.BlockSpec((None, d, tn), lambda l, n: (l, 0, n)),
            pl.BlockSpec((None, 1, tn), lambda l, n: (l, 0, n)),
        ],
        out_specs=pl.BlockSpec((None, rows, tn), lambda l, n: (l, 0, n)),
        out_shape=jax.ShapeDtypeStruct((depth, rows, n6), F32),
        compiler_params=pltpu.CompilerParams(
            dimension_semantics=("arbitrary", "arbitrary"), vmem_limit_bytes=VMEM_LIMIT),
        name="adaln_mod",
    )(c_all, w_ada, b_ada.reshape(depth, 1, n6))


def _in_proj(h_ref, win_ref, p_ref):
    for n0 in range(0, PROJ_PAD, MXU_N):
        n1 = min(n0 + MXU_N, PROJ_PAD)
        p_ref[:, n0:n1] = _bdot(h_ref[...], win_ref[:, n0:n1])


def _out_proj_norm(x_ref, mix_ref, mod_rows, wout_ref, gffn_ref, x1_ref, h2_ref, *, rows, row_chunk):
    for n0 in range(0, D_MODEL, MXU_N):
        cs = slice(n0, n0 + MXU_N)
        mx = _bdot(mix_ref[...], wout_ref[:, cs])
        x1_ref[:, cs] = x_ref[:, cs] + mod_rows("ga1", cs) * mx

    for r in _row_slices(rows, row_chunk):
        h2 = _rms_mod(x1_ref[r, :], gffn_ref[...], mod_rows("sc2", slice(None), r),
                      mod_rows("sh2", slice(None), r))
        h2_ref[r, :] = h2.astype(BF16)


def _ffn_steps(x1_ref, h2_ref, ga2, w13_ref, w2_ref, gfin_ref, y_ref, act_ref, *, rows, row_chunk,
               final):
    for n0 in range(0, D_FF, MXU_N):
        gg = _bdot(h2_ref[...], w13_ref[:, n0:n0 + MXU_N])
        uu = _bdot(h2_ref[...], w13_ref[:, D_FF + n0:D_FF + n0 + MXU_N])
        act_ref[:, n0:n0 + MXU_N] = (gg * _sigmoid(gg) * uu).astype(BF16)
        yield

    dst = x1_ref if final else y_ref
    for n0 in range(0, D_MODEL, MXU_N):
        cs = slice(n0, n0 + MXU_N)
        ff = _bdot(act_ref[...], w2_ref[:, cs])
        dst[:, cs] = x1_ref[:, cs] + ga2(cs) * ff
        yield

    if final:
        for r in _row_slices(rows, row_chunk):
            xb = x1_ref[r, :]
            ms = jnp.mean(xb * xb, axis=-1, keepdims=True)
            y_ref[r, :] = xb * lax.rsqrt(ms + EPS) * gfin_ref[...]


def _run(*gens):
    live = [g if isinstance(g, tuple) else (g, 1) for g in gens]
    while live:
        for entry in list(live):
            g, k = entry
            for _ in range(k):
                try:
                    next(g)
                except StopIteration:
                    live.remove(entry)
                    break


def _layernorm_silu(z, lng, lnb):
    mu = jnp.mean(z, axis=-1, keepdims=True)
    zc = z - mu
    var = jnp.mean(zc * zc, axis=-1, keepdims=True)
    zn = zc * lax.rsqrt(var + EPS) * lng + lnb
    return zn * _sigmoid(zn)


_MOD_ROW = {"sh1": 0, "sc1": 1, "ga1": 2, "sh2": 3, "sc2": 4, "ga2": 5}


def _cumsum_chunk(tril_bf, x):
    hi = x.astype(BF16)
    r1 = x - hi.astype(F32)
    mid = r1.astype(BF16)
    lo = (r1 - mid.astype(F32)).astype(BF16)
    return _bdot(tril_bf, hi) + _bdot(tril_bf, mid) + _bdot(tril_bf, lo)


def _mlstm_chunk(p_ref, r0, bg_ref, ghead_ref, mix_ref, C_ref, n_ref, m_ref, tril, tril_bf):
    rs = slice(r0, r0 + CHUNK)
    gates = p_ref[rs, COL_G:COL_G + LANES] + bg_ref[...]
    fcum = _cumsum_chunk(tril_bf, _log_sigmoid(gates))
    gates_t = gates.T
    fcum_t = fcum.T
    for h in range(HEADS):
        hs = slice(h * HEAD_DIM, (h + 1) * HEAD_DIM)
        q = p_ref[rs, COL_Q + h * HEAD_DIM:COL_Q + (h + 1) * HEAD_DIM]
        k = p_ref[rs, COL_K + h * HEAD_DIM:COL_K + (h + 1) * HEAD_DIM] * (HEAD_DIM ** -0.5)
        v = p_ref[rs, COL_V + h * HEAD_DIM:COL_V + (h + 1) * HEAD_DIM]
        o = p_ref[rs, COL_O + h * HEAD_DIM:COL_O + (h + 1) * HEAD_DIM]
        q_bf = q.astype(BF16)
        v_bf = v.astype(BF16)
        f_col = fcum[:, HEADS + h:HEADS + h + 1]
        i_col = gates[:, h:h + 1]
        f_row = fcum_t[HEADS + h:HEADS + h + 1, :]
        i_row = gates_t[h:h + 1, :]
        m_prev = m_ref[h:h + 1, 0:1]
        n_row = n_ref[h:h + 1, :]
        c_mat = C_ref[h]

        g = f_row[:, CHUNK - 1:CHUNK]
        dlog = jnp.where(tril, f_col - f_row + i_row, -jnp.inf)
        inter = f_col + m_prev
        m_t = jnp.maximum(inter, jnp.max(dlog, axis=-1, keepdims=True))
        dw = jnp.exp(dlog - m_t)
        a_inter = jnp.exp(inter - m_t)
        s = _bdot_nt(q_bf, k.astype(BF16)) * dw
        num = _bdot(s.astype(BF16), v_bf) + _bdot(q_bf, c_mat.astype(BF16)) * a_inter
        qn = jnp.sum(q * n_row, axis=-1, keepdims=True)
        den = jnp.sum(s, axis=-1, keepdims=True) + a_inter * qn
        bound = jnp.maximum(jnp.abs(den), jnp.exp(-m_t))
        hh = num * (1.0 / bound)

        wmax = jnp.max(g - f_row + i_row, axis=-1, keepdims=True)
        m_new = jnp.maximum(g + m_prev, wmax)
        wj = jnp.exp(g - f_col + i_col - m_new)
        decay = jnp.exp(g + m_prev - m_new)
        kw = k * wj
        C_ref[h] = decay * c_mat + _bdot(kw.T.astype(BF16), v_bf)
        n_ref[h:h + 1, :] = decay * n_row + jnp.sum(kw, axis=0, keepdims=True)
        m_ref[h:h + 1, :] = jnp.broadcast_to(m_new, (1, LANES))

        hn = hh * lax.rsqrt(jnp.mean(hh * hh, axis=-1, keepdims=True) + EPS) * ghead_ref[:, hs]
        mix_ref[rs, hs] = (hn * _sigmoid(o)).astype(BF16)
        yield


def _prompt_layer_kernel(x_ref, mod_ref, gmix_ref, win_ref, bg_ref, ghead_ref, wdw_ref, bdw_ref,
                         lng_ref, lnb_ref, wout_ref, gffn_ref, w13_ref, w2_ref, gfin_ref,
                         y_ref, c_out, n_out, m_out, conv_out,
                         h_ref, p_ref, mix_ref, x1_ref, h2_ref, ga2_ref, act_ref, ubuf_ref, zc_ref,
                         C_ref, n_ref, m_ref, *, tm, tiles_per_seq, final):
    s = pl.program_id(0)
    n_tiles = pl.num_programs(0) - 1
    j = jnp.minimum(s, n_tiles - 1) % tiles_per_seq
    last = tiles_per_seq - 1
    row_chunk = 32

    @pl.when(s == 0)
    def _():
        x1_ref[...] = jnp.zeros_like(x1_ref)
        h2_ref[...] = jnp.zeros_like(h2_ref)
        ga2_ref[...] = jnp.zeros_like(ga2_ref)

    @pl.when(j == 0)
    def _():
        C_ref[...] = jnp.zeros_like(C_ref)
        n_ref[...] = jnp.zeros_like(n_ref)
        m_ref[...] = jnp.zeros_like(m_ref)
        ubuf_ref[:, 0:HIST_PAD, :] = jnp.zeros((CONV_SLABS, HIST_PAD, LANES), F32)

    @pl.when(j > 0)
    def _():
        ubuf_ref[:, 0:HIST_PAD, :] = ubuf_ref[:, tm:tm + HIST_PAD, :]

    def mod_rows(name, cs, r=None):
        i = _MOD_ROW[name]
        return mod_ref[i:i + 1, cs]

    for r in _row_slices(tm, row_chunk):
        h = _rms_mod(x_ref[r, :], gmix_ref[...], mod_rows("sc1", slice(None)),
                     mod_rows("sh1", slice(None)))
        h_ref[r, :] = h.astype(BF16)
    _in_proj(h_ref, win_ref, p_ref)

    row_id = lax.broadcasted_iota(jnp.int32, (CHUNK, CHUNK), 0)
    col_id = lax.broadcasted_iota(jnp.int32, (CHUNK, CHUNK), 1)
    tril = row_id >= col_id
    tril_bf = jnp.where(tril, 1.0, 0.0).astype(BF16)

    def mlstm_steps():
        for c in range(tm // CHUNK):
            yield from _mlstm_chunk(p_ref, c * CHUNK, bg_ref, ghead_ref, mix_ref, C_ref, n_ref,
                                    m_ref, tril, tril_bf)

    def conv_steps():
        for r in _row_slices(tm, row_chunk):
            ro = slice(r.start + HIST_PAD, r.stop + HIST_PAD)
            u = p_ref[r, COL_A:COL_A + CONV_CH] * _sigmoid(p_ref[r, COL_B:COL_B + CONV_CH])
            for g in range(CONV_SLABS):
                ubuf_ref[g, ro, :] = u[:, g * LANES:(g + 1) * LANES]
            yield
        n_sets = row_chunk // SUBLANES
        firsts = [(t0 // CONV_STRIDE) * CONV_STRIDE * SUBLANES + t0 % CONV_STRIDE
                  for t0 in range(n_sets)]
        for g in range(CONV_SLABS):
            gs = slice(g * LANES, (g + 1) * LANES)
            bias = jnp.broadcast_to(bdw_ref[:, gs], (SUBLANES, LANES))
            for base in range(0, tm, row_chunk):
                acc = [bias for _ in range(n_sets)]
                for w in range(CONV_WIDTH):
                    tap = wdw_ref[w:w + 1, gs]
                    for t0 in range(n_sets):
                        src = pl.ds(base + firsts[t0] + HIST_PAD - HIST + w, SUBLANES,
                                    stride=CONV_STRIDE)
                        acc[t0] = acc[t0] + ubuf_ref[g, src, :] * tap
                for t0 in range(n_sets):
                    zc_ref[g, pl.ds(base + firsts[t0], SUBLANES, stride=CONV_STRIDE), :] = acc[t0]
                yield
        for r in _row_slices(tm, row_chunk):
            z = jnp.concatenate([zc_ref[g, r, :] for g in range(CONV_SLABS)], axis=-1)
            mix_ref[r, MLSTM_WIDTH:] = _layernorm_silu(z, lng_ref[...], lnb_ref[...]).astype(BF16)
            yield

    _run(_ffn_steps(x1_ref, h2_ref, lambda cs: ga2_ref[:, cs], w13_ref, w2_ref, gfin_ref, y_ref,
                    act_ref, rows=tm, row_chunk=row_chunk, final=final),
         mlstm_steps(), (conv_steps(), 3))

    @pl.when(jnp.logical_and(j == last, s < n_tiles))
    def _():
        c_out[...] = C_ref[...]
        n_out[...] = n_ref[0:HEADS, :]
        m_out[...] = m_ref[0:HEADS, :]
        for g in range(CONV_SLABS):
            conv_out[:, g * LANES:(g + 1) * LANES] = ubuf_ref[g, tm + HIST_PAD - HIST:tm + HIST_PAD, :]

    @pl.when(s >= 0)
    def _():
        _out_proj_norm(x_ref, mix_ref, mod_rows, wout_ref, gffn_ref, x1_ref, h2_ref,
                       rows=tm, row_chunk=row_chunk)
        ga2_ref[...] = mod_rows("ga2", slice(None))


def _resident(shape):
    nd = len(shape)
    return pl.BlockSpec(shape, lambda s: (0,) * nd, pipeline_mode=pl.Buffered(1))


def _prompt_layer(x, mod, wl, g_final, *, tm, final):
    bsz, seq, d = x.shape
    assert seq % tm == 0 and tm % CHUNK == 0 and tm >= HIST_PAD
    nt = seq // tm
    n_tiles = bsz * nt
    kern = functools.partial(_prompt_layer_kernel, tm=tm, tiles_per_seq=nt, final=final)

    def cur(s):
        return jnp.minimum(s, n_tiles - 1)

    def prev(s):
        return jnp.maximum(s - 1, 0)

    in_specs = [
        pl.BlockSpec((None, tm, d), lambda s: (cur(s) // nt, cur(s) % nt, 0)),
        pl.BlockSpec((None, 6, d), lambda s: (cur(s) // nt, 0, 0)),
        _resident((1, d)),
        _resident((d, PROJ_PAD)),
        _resident((1, LANES)),
        _resident((1, MLSTM_WIDTH)),
        _resident((CONV_WIDTH, CONV_CH)),
        _resident((1, CONV_CH)),
        _resident((1, CONV_CH)),
        _resident((1, CONV_CH)),
        _resident((d, d)),
        _resident((1, d)),
        _resident((d, 2 * D_FF)),
        _resident((D_FF, d)),
        _resident((1, d)),
    ]
    out_specs = [
        pl.BlockSpec((None, tm, d), lambda s: (prev(s) // nt, prev(s) % nt, 0)),
        pl.BlockSpec((None, HEADS, HEAD_DIM, HEAD_DIM), lambda s: (cur(s) // nt, 0, 0, 0)),
        pl.BlockSpec((None, HEADS, HEAD_DIM), lambda s: (cur(s) // nt, 0, 0)),
        pl.BlockSpec((None, HEADS, LANES), lambda s: (cur(s) // nt, 0, 0)),
        pl.BlockSpec((None, HIST, CONV_CH), lambda s: (cur(s) // nt, 0, 0)),
    ]
    out_shape = [
        jax.ShapeDtypeStruct((bsz, seq, d), F32),
        jax.ShapeDtypeStruct((bsz, HEADS, HEAD_DIM, HEAD_DIM), F32),
        jax.ShapeDtypeStruct((bsz, HEADS, HEAD_DIM), F32),
        jax.ShapeDtypeStruct((bsz, HEADS, LANES), F32),
        jax.ShapeDtypeStruct((bsz, HIST, CONV_CH), F32),
    ]
    scratch = [
        pltpu.VMEM((tm, d), BF16),
        pltpu.VMEM((tm, PROJ_PAD), F32),
        pltpu.VMEM((tm, d), BF16),
        pltpu.VMEM((tm, d), F32),
        pltpu.VMEM((tm, d), BF16),
        pltpu.VMEM((1, d), F32),
        pltpu.VMEM((tm, D_FF), BF16),
        pltpu.VMEM((CONV_SLABS, tm + HIST_PAD, LANES), F32),
        pltpu.VMEM((CONV_SLABS, tm, LANES), F32),
        pltpu.VMEM((HEADS, HEAD_DIM, HEAD_DIM), F32),
        pltpu.VMEM((SUBLANES, HEAD_DIM), F32),
        pltpu.VMEM((SUBLANES, LANES), F32),
    ]
    return pl.pallas_call(
        kern,
        grid=(n_tiles + 1,),
        in_specs=in_specs,
        out_specs=out_specs,
        out_shape=out_shape,
        scratch_shapes=scratch,
        compiler_params=pltpu.CompilerParams(
            dimension_semantics=("arbitrary",), vmem_limit_bytes=VMEM_LIMIT),
        name="prompt_layer",
    )(x, mod, wl["g_mix"], wl["w_in"], wl["b_gates"], wl["g_head"], wl["w_dw"], wl["b_dw"],
      wl["ln_g"], wl["ln_b"], wl["w_out"], wl["g_ffn"], wl["w_ffn13"], wl["w_ffn2"], g_final)


def _sample_in_kernel(x_ref, mod_ref, gmix_ref, win_ref, p_ref, h_ref):
    h = _rms_mod(x_ref[...], gmix_ref[...], mod_ref[:, D_MODEL:2 * D_MODEL], mod_ref[:, 0:D_MODEL])
    h_ref[...] = h.astype(BF16)
    _in_proj(h_ref, win_ref, p_ref)


def _sample_in_proj(x, mod, wl):
    rows = x.shape[0]
    return pl.pallas_call(
        _sample_in_kernel,
        out_shape=jax.ShapeDtypeStruct((rows, PROJ_PAD), F32),
        scratch_shapes=[pltpu.VMEM((rows, D_MODEL), BF16)],
        compiler_params=pltpu.CompilerParams(vmem_limit_bytes=VMEM_LIMIT),
        name="sample_in_proj",
    )(x, mod, wl["g_mix"], wl["w_in"])


def _sample_step_kernel(p3_ref, p2_ref, c_ref, n_ref, m_ref, conv_ref, bg_ref, ghead_ref,
                        wdw_ref, bdw_ref, lng_ref, lnb_ref,
                        hm_ref, zc_ref, c_out, n_out, m_out, conv_out, *, bt):
    scale = HEAD_DIM ** -0.5
    for b in range(bt):
        qk_t = p3_ref[b, 0:2 * HEADS, :].T
        gates = p3_ref[b, COL_G // LANES:COL_G // LANES + 1, :] + bg_ref[...]
        logf = _log_sigmoid(gates)
        m_new_row = jnp.zeros((1, LANES), F32)
        for h in range(HEADS):
            q_col = qk_t[:, h:h + 1]
            k_col = qk_t[:, HEADS + h:HEADS + h + 1] * scale
            q_row = p3_ref[b, h:h + 1, :]
            k_row = p3_ref[b, HEADS + h:HEADS + h + 1, :] * scale
            v_row = p3_ref[b, 2 * HEADS + h:2 * HEADS + h + 1, :]
            o_row = p3_ref[b, 3 * HEADS + h:3 * HEADS + h + 1, :]
            i_log = gates[:, h:h + 1]
            f_log = logf[:, HEADS + h:HEADS + h + 1]
            m_prev = m_ref[b, :, h:h + 1]
            n_row = n_ref[b, h:h + 1, :]
            c_mat = c_ref[b, h]

            inter = f_log + m_prev
            m_t = jnp.maximum(inter, i_log)
            dw = jnp.exp(i_log - m_t)
            a_inter = jnp.exp(inter - m_t)
            s = jnp.sum(q_row * k_row, axis=-1, keepdims=True) * dw
            qc = jnp.sum(q_col * c_mat, axis=0, keepdims=True)
            num = s * v_row + qc * a_inter
            den = s + a_inter * jnp.sum(q_row * n_row, axis=-1, keepdims=True)
            bound = jnp.maximum(jnp.abs(den), jnp.exp(-m_t))
            hh = num * (1.0 / bound)
            c_out[b, h] = a_inter * c_mat + (k_col * dw) * v_row
            n_out[b, h:h + 1, :] = a_inter * n_row + k_row * dw
            lane = lax.broadcasted_iota(jnp.int32, (1, LANES), 1)
            m_new_row = jnp.where(lane == h, m_t, m_new_row)
            hn = hh * lax.rsqrt(jnp.mean(hh * hh, axis=-1, keepdims=True) + EPS)
            hn = hn * ghead_ref[:, h * HEAD_DIM:(h + 1) * HEAD_DIM]
            hm_ref[b, h:h + 1, :] = hn * _sigmoid(o_row)
        m_out[b] = m_new_row

        u = p2_ref[b:b + 1, COL_A:COL_A + CONV_CH] * _sigmoid(p2_ref[b:b + 1, COL_B:COL_B + CONV_CH])
        hist = conv_ref[b]
        z = (jnp.sum(hist * wdw_ref[0:HIST, :], axis=0, keepdims=True)
             + u * wdw_ref[HIST:HIST + 1, :] + bdw_ref[...])
        zc_ref[b:b + 1, :] = _layernorm_silu(z, lng_ref[...], lnb_ref[...])
        conv_out[b, 0:HIST - 1, :] = hist[1:HIST, :]
        conv_out[b, HIST - 1:HIST, :] = u


def _sample_step(p, state_c, state_n, state_m, state_conv, wl, *, bt=8):
    rows = p.shape[0]
    assert rows % bt == 0
    p3 = p.reshape(rows, PROJ_PAD // LANES, LANES)
    m3 = jnp.pad(state_m, ((0, 0), (0, LANES - HEADS))).reshape(rows, 1, LANES)
    blk = lambda *s: pl.BlockSpec((bt,) + s, lambda i: (i,) + (0,) * len(s))
    const = lambda *s: pl.BlockSpec(s, lambda i: (0,) * len(s))
    out_shape = [
        jax.ShapeDtypeStruct((rows, HEADS, HEAD_DIM), F32),
        jax.ShapeDtypeStruct((rows, CONV_CH), F32),
        jax.ShapeDtypeStruct(state_c.shape, F32),
        jax.ShapeDtypeStruct(state_n.shape, F32),
        jax.ShapeDtypeStruct((rows, 1, LANES), F32),
        jax.ShapeDtypeStruct(state_conv.shape, F32),
    ]
    hm, zc, c_new, n_new, m_new, conv_new = pl.pallas_call(
        functools.partial(_sample_step_kernel, bt=bt),
        grid=(rows // bt,),
        in_specs=[
            blk(PROJ_PAD // LANES, LANES), blk(PROJ_PAD),
            blk(HEADS, HEAD_DIM, HEAD_DIM), blk(HEADS, HEAD_DIM), blk(1, LANES),
            blk(HIST, CONV_CH),
            const(1, LANES), const(1, MLSTM_WIDTH), const(CONV_WIDTH, CONV_CH),
            const(1, CONV_CH), const(1, CONV_CH), const(1, CONV_CH),
        ],
        out_specs=[
            blk(HEADS, HEAD_DIM), blk(CONV_CH), blk(HEADS, HEAD_DIM, HEAD_DIM),
            blk(HEADS, HEAD_DIM), blk(1, LANES), blk(HIST, CONV_CH),
        ],
        out_shape=out_shape,
        compiler_params=pltpu.CompilerParams(
            dimension_semantics=("arbitrary",), vmem_limit_bytes=VMEM_LIMIT),
        name="sample_step",
    )(p3, p, state_c, state_n, m3, state_conv, wl["b_gates"], wl["g_head"], wl["w_dw"],
      wl["b_dw"], wl["ln_g"], wl["ln_b"])
    return hm.reshape(rows, MLSTM_WIDTH), zc, c_new, n_new, m_new[:, 0, :HEADS], conv_new


def _sample_out_kernel(x_ref, hm_ref, zc_ref, mod_ref, wout_ref, gffn_ref, w13_ref, w2_ref,
                       gfin_ref, y_ref, mix_ref, x1_ref, h2_ref, act_ref, *, rows, final):
    mix_ref[:, 0:MLSTM_WIDTH] = hm_ref[...].astype(BF16)
    mix_ref[:, MLSTM_WIDTH:] = zc_ref[...].astype(BF16)

    def mod_rows(name, cs, r=slice(None)):
        i = _MOD_ROW[name]
        if isinstance(cs, slice) and cs == slice(None):
            cs = slice(0, D_MODEL)
        return mod_ref[r, i * D_MODEL + cs.start:i * D_MODEL + cs.stop]

    _out_proj_norm(x_ref, mix_ref, mod_rows, wout_ref, gffn_ref, x1_ref, h2_ref,
                   rows=rows, row_chunk=32)
    _run(_ffn_steps(x1_ref, h2_ref, lambda cs: mod_rows("ga2", cs), w13_ref, w2_ref, gfin_ref, y_ref,
                    act_ref, rows=rows, row_chunk=32, final=final))


def _sample_out(x, hm, zc, mod, wl, g_final, *, final):
    rows = x.shape[0]
    return pl.pallas_call(
        functools.partial(_sample_out_kernel, rows=rows, final=final),
        out_shape=jax.ShapeDtypeStruct((rows, D_MODEL), F32),
        scratch_shapes=[
            pltpu.VMEM((rows, D_MODEL), BF16),
            pltpu.VMEM((rows, D_MODEL), F32),
            pltpu.VMEM((rows, D_MODEL), BF16),
            pltpu.VMEM((rows, D_FF), BF16),
        ],
        compiler_params=pltpu.CompilerParams(vmem_limit_bytes=VMEM_LIMIT),
        name="sample_out",
    )(x, hm, zc, mod, wl["w_out"], wl["g_ffn"], wl["w_ffn13"], wl["w_ffn2"], g_final)


def _prep_layer_weights(l, g_mix, w_in, b_gates, g_head, w_dw, b_dw, ln_g, ln_b, w_out, g_ffn,
                        w_ffn13, w_ffn2):
    d = w_in.shape[1]
    n_gate = 2 * HEADS
    gate0 = 4 * MLSTM_WIDTH
    w = w_in[l]
    w_r = jnp.concatenate(
        [w[:, :gate0], w[:, gate0 + n_gate:], w[:, gate0:gate0 + n_gate],
         jnp.zeros((d, LANES - n_gate), w.dtype)], axis=1).astype(BF16)
    return {
        "g_mix": g_mix[l].reshape(1, d),
        "w_in": w_r,
        "b_gates": jnp.pad(b_gates[l], (0, LANES - n_gate)).reshape(1, LANES),
        "g_head": g_head[l].reshape(1, MLSTM_WIDTH),
        "w_dw": w_dw[l],
        "b_dw": b_dw[l].reshape(1, CONV_CH),
        "ln_g": ln_g[l].reshape(1, CONV_CH),
        "ln_b": ln_b[l].reshape(1, CONV_CH),
        "w_out": w_out[l].astype(BF16),
        "g_ffn": g_ffn[l].reshape(1, d),
        "w_ffn13": w_ffn13[l].astype(BF16),
        "w_ffn2": w_ffn2[l].astype(BF16),
    }


def _forward(x_prompt, x_sample, c_prompt, c_sample, state_C, state_n, state_m, state_conv,
             w_ada, b_ada, g_mix, w_in, b_gates, g_head, w_dw, b_dw, ln_g, ln_b, w_out,
             g_ffn, w_ffn13, w_ffn2, g_final, *, tm):
    depth = w_ada.shape[0]
    bp = x_prompt.shape[0]
    d = x_prompt.shape[-1]
    mod = _modulation(jnp.concatenate([c_prompt, c_sample], axis=0), w_ada, b_ada)
    gfin = g_final.reshape(1, d)
    xp = x_prompt
    xs = x_sample.reshape(x_sample.shape[0], d)
    outs_p = [[], [], [], []]
    outs_s = [[], [], [], []]
    for l in range(depth):
        wl = _prep_layer_weights(l, g_mix, w_in, b_gates, g_head, w_dw, b_dw, ln_g, ln_b, w_out,
                                 g_ffn, w_ffn13, w_ffn2)
        final = l == depth - 1
        mod_p = mod[l, :bp].reshape(bp, 6, d)
        mod_s = mod[l, bp:]
        xp, c1, n1, m1, b1 = _prompt_layer(xp, mod_p, wl, gfin, tm=tm, final=final)
        for acc, val in zip(outs_p, (c1, n1, m1[:, :, 0], b1)):
            acc.append(val)
        p = _sample_in_proj(xs, mod_s, wl)
        hm, zc, c2, n2, m2, b2 = _sample_step(p, state_C[l], state_n[l], state_m[l],
                                              state_conv[l], wl)
        xs = _sample_out(xs, hm, zc, mod_s, wl, gfin, final=final)
        for acc, val in zip(outs_s, (c2, n2, m2, b2)):
            acc.append(val)
    y_sample = xs.reshape(x_sample.shape)
    return (xp, y_sample) + tuple(jnp.stack(a) for a in outs_p) + tuple(jnp.stack(a) for a in outs_s)


def kernel(x_prompt, x_sample, c_prompt, c_sample, state_C, state_n, state_m, state_conv, w_ada, b_ada, g_mix, w_in, b_gates, g_head, w_dw, b_dw, ln_g, ln_b, w_out, g_ffn, w_ffn13, w_ffn2, g_final):
    return _forward(x_prompt, x_sample, c_prompt, c_sample, state_C, state_n, state_m, state_conv,
                    w_ada, b_ada, g_mix, w_in, b_gates, g_head, w_dw, b_dw, ln_g, ln_b, w_out,
                    g_ffn, w_ffn13, w_ffn2, g_final, tm=256)
```

```python
import functools

import jax
import jax.numpy as jnp
from jax import lax
from jax.experimental import pallas as pl
from jax.experimental.pallas import tpu as pltpu

F32 = jnp.float32
BF16 = jnp.bfloat16

D_MODEL = 1024
HEADS = 4
HEAD_DIM = 128
MLSTM_WIDTH = HEADS * HEAD_DIM
CONV_CH = D_MODEL - MLSTM_WIDTH
CONV_WIDTH = 31
HIST = CONV_WIDTH - 1
D_FF = 2816
CHUNK = 128
EPS = 1e-6

LANES = 128
SUBLANES = 8
MXU_N = 256

COL_Q = 0
COL_K = MLSTM_WIDTH
COL_V = 2 * MLSTM_WIDTH
COL_O = 3 * MLSTM_WIDTH
COL_A = 4 * MLSTM_WIDTH
COL_B = COL_A + CONV_CH
COL_G = COL_B + CONV_CH
PROJ_PAD = COL_G + LANES
HIST_PAD = 32
CONV_SLABS = CONV_CH // LANES
CONV_STRIDE = 2
VMEM_LIMIT = 60 * 1024 * 1024


def _log_sigmoid(x):
    return jnp.minimum(x, 0.0) - jnp.log1p(jnp.exp(-jnp.abs(x)))


def _sigmoid(x):
    return 1.0 / (1.0 + jnp.exp(-x))


def _bdot(a, b):
    return jnp.dot(a, b, preferred_element_type=F32)


def _bdot_nt(a, b):
    return lax.dot_general(a, b, (((1,), (1,)), ((), ())), preferred_element_type=F32)


def _row_slices(rows, chunk):
    chunk = min(chunk, rows)
    assert rows % chunk == 0
    return [slice(r0, r0 + chunk) for r0 in range(0, rows, chunk)]


def _rms_mod(x, g, sc, sh):
    ms = jnp.mean(x * x, axis=-1, keepdims=True)
    return (x * lax.rsqrt(ms + EPS) * g) * (1.0 + sc) + sh


def _mod_kernel(c_ref, w_ref, b_ref, o_ref):
    o_ref[...] = _bdot(c_ref[...].astype(BF16), w_ref[...].astype(BF16)) + b_ref[...]


def _modulation(c_all, w_ada, b_ada):
    depth, d, n6 = w_ada.shape
    rows = c_all.shape[0]
    tn = 1024
    return pl.pallas_call(
        _mod_kernel,
        grid=(depth, n6 // tn),
        in_specs=[
            pl.BlockSpec((rows, d), lambda l, n: (0, 0)),
            pl.BlockSpec((None, d, tn), lambda l, n: (l, 0, n)),
            pl.BlockSpec((None, 1, tn), lambda l, n: (l, 0, n)),
        ],
        out_specs=pl.BlockSpec((None, rows, tn), lambda l, n: (l, 0, n)),
        out_shape=jax.ShapeDtypeStruct((depth, rows, n6), F32),
        compiler_params=pltpu.CompilerParams(
            dimension_semantics=("arbitrary", "arbitrary"), vmem_limit_bytes=VMEM_LIMIT),
        name="adaln_mod",
    )(c_all, w_ada, b_ada.reshape(depth, 1, n6))


def _in_proj(h_ref, win_ref, p_ref):
    for n0 in range(0, PROJ_PAD, MXU_N):
        n1 = min(n0 + MXU_N, PROJ_PAD)
        p_ref[:, n0:n1] = _bdot(h_ref[...], win_ref[:, n0:n1])


def _out_proj_norm(x_ref, mix_ref, mod_rows, wout_ref, gffn_ref, x1_ref, h2_ref, *, rows, row_chunk):
    for n0 in range(0, D_MODEL, MXU_N):
        cs = slice(n0, n0 + MXU_N)
        mx = _bdot(mix_ref[...], wout_ref[:, cs])
        x1_ref[:, cs] = x_ref[:, cs] + mod_rows("ga1", cs) * mx

    for r in _row_slices(rows, row_chunk):
        h2 = _rms_mod(x1_ref[r, :], gffn_ref[...], mod_rows("sc2", slice(None), r),
                      mod_rows("sh2", slice(None), r))
        h2_ref[r, :] = h2.astype(BF16)


def _ffn_steps(x1_ref, h2_ref, ga2, w13_ref, w2_ref, gfin_ref, y_ref, act_ref, *, rows, row_chunk,
               final):
    for n0 in range(0, D_FF, MXU_N):
        gg = _bdot(h2_ref[...], w13_ref[:, n0:n0 + MXU_N])
        uu = _bdot(h2_ref[...], w13_ref[:, D_FF + n0:D_FF + n0 + MXU_N])
        act_ref[:, n0:n0 + MXU_N] = (gg * _sigmoid(gg) * uu).astype(BF16)
        yield

    dst = x1_ref if final else y_ref
    for n0 in range(0, D_MODEL, MXU_N):
        cs = slice(n0, n0 + MXU_N)
        ff = _bdot(act_ref[...], w2_ref[:, cs])
        dst[:, cs] = x1_ref[:, cs] + ga2(cs) * ff
        yield

    if final:
        for r in _row_slices(rows, row_chunk):
            xb = x1_ref[r, :]
            ms = jnp.mean(xb * xb, axis=-1, keepdims=True)
            y_ref[r, :] = xb * lax.rsqrt(ms + EPS) * gfin_ref[...]


def _run(*gens):
    live = [g if isinstance(g, tuple) else (g, 1) for g in gens]
    while live:
        for entry in list(live):
            g, k = entry
            for _ in range(k):
                try:
                    next(g)
                except StopIteration:
                    live.remove(entry)
                    break


def _layernorm_silu(z, lng, lnb):
    mu = jnp.mean(z, axis=-1, keepdims=True)
    zc = z - mu
    var = jnp.mean(zc * zc, axis=-1, keepdims=True)
    zn = zc * lax.rsqrt(var + EPS) * lng + lnb
    return zn * _sigmoid(zn)


_MOD_ROW = {"sh1": 0, "sc1": 1, "ga1": 2, "sh2": 3, "sc2": 4, "ga2": 5}


def _cumsum_chunk(tril_bf, x):
    hi = x.astype(BF16)
    r1 = x - hi.astype(F32)
    mid = r1.astype(BF16)
    lo = (r1 - mid.astype(F32)).astype(BF16)
    return _bdot(tril_bf, hi) + _bdot(tril_bf, mid) + _bdot(tril_bf, lo)


def _mlstm_chunk(p_ref, r0, bg_ref, ghead_ref, mix_ref, C_ref, n_ref, m_ref, tril, tril_bf):
    rs = slice(r0, r0 + CHUNK)
    gates = p_ref[rs, COL_G:COL_G + LANES] + bg_ref[...]
    fcum = _cumsum_chunk(tril_bf, _log_sigmoid(gates))
    gates_t = gates.T
    fcum_t = fcum.T
    for h in range(HEADS):
        hs = slice(h * HEAD_DIM, (h + 1) * HEAD_DIM)
        q = p_ref[rs, COL_Q + h * HEAD_DIM:COL_Q + (h + 1) * HEAD_DIM]
        k = p_ref[rs, COL_K + h * HEAD_DIM:COL_K + (h + 1) * HEAD_DIM] * (HEAD_DIM ** -0.5)
        v = p_ref[rs, COL_V + h * HEAD_DIM:COL_V + (h + 1) * HEAD_DIM]
        o = p_ref[rs, COL_O + h * HEAD_DIM:COL_O + (h + 1) * HEAD_DIM]
        q_bf = q.astype(BF16)
        v_bf = v.astype(BF16)
        f_col = fcum[:, HEADS + h:HEADS + h + 1]
        i_col = gates[:, h:h + 1]
        f_row = fcum_t[HEADS + h:HEADS + h + 1, :]
        i_row = gates_t[h:h + 1, :]
        m_prev = m_ref[h:h + 1, 0:1]
        n_row = n_ref[h:h + 1, :]
        c_mat = C_ref[h]

        g = f_row[:, CHUNK - 1:CHUNK]
        dlog = jnp.where(tril, f_col - f_row + i_row, -jnp.inf)
        inter = f_col + m_prev
        m_t = jnp.maximum(inter, jnp.max(dlog, axis=-1, keepdims=True))
        dw = jnp.exp(dlog - m_t)
        a_inter = jnp.exp(inter - m_t)
        s = _bdot_nt(q_bf, k.astype(BF16)) * dw
        num = _bdot(s.astype(BF16), v_bf) + _bdot(q_bf, c_mat.astype(BF16)) * a_inter
        qn = jnp.sum(q * n_row, axis=-1, keepdims=True)
        den = jnp.sum(s, axis=-1, keepdims=True) + a_inter * qn
        bound = jnp.maximum(jnp.abs(den), jnp.exp(-m_t))
        hh = num * (1.0 / bound)

        wmax = jnp.max(g - f_row + i_row, axis=-1, keepdims=True)
        m_new = jnp.maximum(g + m_prev, wmax)
        wj = jnp.exp(g - f_col + i_col - m_new)
        decay = jnp.exp(g + m_prev - m_new)
        kw = k * wj
        C_ref[h] = decay * c_mat + _bdot(kw.T.astype(BF16), v_bf)
        n_ref[h:h + 1, :] = decay * n_row + jnp.sum(kw, axis=0, keepdims=True)
        m_ref[h:h + 1, :] = jnp.broadcast_to(m_new, (1, LANES))

        hn = hh * lax.rsqrt(jnp.mean(hh * hh, axis=-1, keepdims=True) + EPS) * ghead_ref[:, hs]
        mix_ref[rs, hs] = (hn * _sigmoid(o)).astype(BF16)
        yield


def _prompt_layer_kernel(x_ref, mod_ref, gmix_ref, win_ref, bg_ref, ghead_ref, wdw_ref, bdw_ref,
                         lng_ref, lnb_ref, wout_ref, gffn_ref, w13_ref, w2_ref, gfin_ref,
                         y_ref, c_out, n_out, m_out, conv_out,
                         h_ref, p_ref, mix_ref, x1_ref, h2_ref, ga2_ref, act_ref, ubuf_ref, zc_ref,
                         C_ref, n_ref, m_ref, *, tm, tiles_per_seq, final):
    s = pl.program_id(0)
    n_tiles = pl.num_programs(0) - 1
    j = jnp.minimum(s, n_tiles - 1) % tiles_per_seq
    last = tiles_per_seq - 1
    row_chunk = 32

    @pl.when(s == 0)
    def _():
        x1_ref[...] = jnp.zeros_like(x1_ref)
        h2_ref[...] = jnp.zeros_like(h2_ref)
        ga2_ref[...] = jnp.zeros_like(ga2_ref)

    @pl.when(j == 0)
    def _():
        C_ref[...] = jnp.zeros_like(C_ref)
        n_ref[...] = jnp.zeros_like(n_ref)
        m_ref[...] = jnp.zeros_like(m_ref)
        ubuf_ref[:, 0:HIST_PAD, :] = jnp.zeros((CONV_SLABS, HIST_PAD, LANES), F32)

    @pl.when(j > 0)
    def _():
        ubuf_ref[:, 0:HIST_PAD, :] = ubuf_ref[:, tm:tm + HIST_PAD, :]

    def mod_rows(name, cs, r=None):
        i = _MOD_ROW[name]
        return mod_ref[i:i + 1, cs]

    for r in _row_slices(tm, row_chunk):
        h = _rms_mod(x_ref[r, :], gmix_ref[...], mod_rows("sc1", slice(None)),
                     mod_rows("sh1", slice(None)))
        h_ref[r, :] = h.astype(BF16)
    _in_proj(h_ref, win_ref, p_ref)

    row_id = lax.broadcasted_iota(jnp.int32, (CHUNK, CHUNK), 0)
    col_id = lax.broadcasted_iota(jnp.int32, (CHUNK, CHUNK), 1)
    tril = row_id >= col_id
    tril_bf = jnp.where(tril, 1.0, 0.0).astype(BF16)

    def mlstm_steps():
        for c in range(tm // CHUNK):
            yield from _mlstm_chunk(p_ref, c * CHUNK, bg_ref, ghead_ref, mix_ref, C_ref, n_ref,
                                    m_ref, tril, tril_bf)

    def conv_steps():
        for r in _row_slices(tm, row_chunk):
            ro = slice(r.start + HIST_PAD, r.stop + HIST_PAD)
            u = p_ref[r, COL_A:COL_A + CONV_CH] * _sigmoid(p_ref[r, COL_B:COL_B + CONV_CH])
            for g in range(CONV_SLABS):
                ubuf_ref[g, ro, :] = u[:, g * LANES:(g + 1) * LANES]
            yield
        n_sets = row_chunk // SUBLANES
        firsts = [(t0 // CONV_STRIDE) * CONV_STRIDE * SUBLANES + t0 % CONV_STRIDE
                  for t0 in range(n_sets)]
        for g in range(CONV_SLABS):
            gs = slice(g * LANES, (g + 1) * LANES)
            bias = jnp.broadcast_to(bdw_ref[:, gs], (SUBLANES, LANES))
            for base in range(0, tm, row_chunk):
                acc = [bias for _ in range(n_sets)]
                for w in range(CONV_WIDTH):
                    tap = wdw_ref[w:w + 1, gs]
                    for t0 in range(n_sets):
                        src = pl.ds(base + firsts[t0] + HIST_PAD - HIST + w, SUBLANES,
                                    stride=CONV_STRIDE)
                        acc[t0] = acc[t0] + ubuf_ref[g, src, :] * tap
                for t0 in range(n_sets):
                    zc_ref[g, pl.ds(base + firsts[t0], SUBLANES, stride=CONV_STRIDE), :] = acc[t0]
                yield
        for r in _row_slices(tm, row_chunk):
            z = jnp.concatenate([zc_ref[g, r, :] for g in range(CONV_SLABS)], axis=-1)
            mix_ref[r, MLSTM_WIDTH:] = _layernorm_silu(z, lng_ref[...], lnb_ref[...]).astype(BF16)
            yield

    _run(_ffn_steps(x1_ref, h2_ref, lambda cs: ga2_ref[:, cs], w13_ref, w2_ref, gfin_ref, y_ref,
                    act_ref, rows=tm, row_chunk=row_chunk, final=final),
         mlstm_steps(), (conv_steps(), 3))

    @pl.when(jnp.logical_and(j == last, s < n_tiles))
    def _():
        c_out[...] = C_ref[...]
        n_out[...] = n_ref[0:HEADS, :]
        m_out[...] = m_ref[0:HEADS, :]
        for g in range(CONV_SLABS):
            conv_out[:, g * LANES:(g + 1) * LANES] = ubuf_ref[g, tm + HIST_PAD - HIST:tm + HIST_PAD, :]

    @pl.when(s >= 0)
    def _():
        _out_proj_norm(x_ref, mix_ref, mod_rows, wout_ref, gffn_ref, x1_ref, h2_ref,
                       rows=tm, row_chunk=row_chunk)
        ga2_ref[...] = mod_rows("ga2", slice(None))


def _resident(shape):
    nd = len(shape)
    return pl.BlockSpec(shape, lambda s: (0,) * nd, pipeline_mode=pl.Buffered(1))


def _prompt_layer(x, mod, wl, g_final, *, tm, final):
    bsz, seq, d = x.shape
    assert seq % tm == 0 and tm % CHUNK == 0 and tm >= HIST_PAD
    nt = seq // tm
    n_tiles = bsz * nt
    kern = functools.partial(_prompt_layer_kernel, tm=tm, tiles_per_seq=nt, final=final)

    def cur(s):
        return jnp.minimum(s, n_tiles - 1)

    def prev(s):
        return jnp.maximum(s - 1, 0)

    in_specs = [
        pl.BlockSpec((None, tm, d), lambda s: (cur(s) // nt, cur(s) % nt, 0)),
        pl.BlockSpec((None, 6, d), lambda s: (cur(s) // nt, 0, 0)),
        _resident((1, d)),
        _resident((d, PROJ_PAD)),
        _resident((1, LANES)),
        _resident((1, MLSTM_WIDTH)),
        _resident((CONV_WIDTH, CONV_CH)),
        _resident((1, CONV_CH)),
        _resident((1, CONV_CH)),
        _resident((1, CONV_CH)),
        _resident((d, d)),
        _resident((1, d)),
        _resident((d, 2 * D_FF)),
        _resident((D_FF, d)),
        _resident((1, d)),
    ]
    out_specs = [
        pl.BlockSpec((None, tm, d), lambda s: (prev(s) // nt, prev(s) % nt, 0)),
        pl.BlockSpec((None, HEADS, HEAD_DIM, HEAD_DIM), lambda s: (cur(s) // nt, 0, 0, 0)),
        pl.BlockSpec((None, HEADS, HEAD_DIM), lambda s: (cur(s) // nt, 0, 0)),
        pl.BlockSpec((None, HEADS, LANES), lambda s: (cur(s) // nt, 0, 0)),
        pl.BlockSpec((None, HIST, CONV_CH), lambda s: (cur(s) // nt, 0, 0)),
    ]
    out_shape = [
        jax.ShapeDtypeStruct((bsz, seq, d), F32),
        jax.ShapeDtypeStruct((bsz, HEADS, HEAD_DIM, HEAD_DIM), F32),
        jax.ShapeDtypeStruct((bsz, HEADS, HEAD_DIM), F32),
        jax.ShapeDtypeStruct((bsz, HEADS, LANES), F32),
        jax.ShapeDtypeStruct((bsz, HIST, CONV_CH), F32),
    ]
    scratch = [
        pltpu.VMEM((tm, d), BF16),
        pltpu.VMEM((tm, PROJ_PAD), F32),
        pltpu.VMEM((tm, d), BF16),
        pltpu.VMEM((tm, d), F32),
        pltpu.VMEM((tm, d), BF16),
        pltpu.VMEM((1, d), F32),
        pltpu.VMEM((tm, D_FF), BF16),
        pltpu.VMEM((CONV_SLABS, tm + HIST_PAD, LANES), F32),
        pltpu.VMEM((CONV_SLABS, tm, LANES), F32),
        pltpu.VMEM((HEADS, HEAD_DIM, HEAD_DIM), F32),
        pltpu.VMEM((SUBLANES, HEAD_DIM), F32),
        pltpu.VMEM((SUBLANES, LANES), F32),
    ]
    return pl.pallas_call(
        kern,
        grid=(n_tiles + 1,),
        in_specs=in_specs,
        out_specs=out_specs,
        out_shape=out_shape,
        scratch_shapes=scratch,
        compiler_params=pltpu.CompilerParams(
            dimension_semantics=("arbitrary",), vmem_limit_bytes=VMEM_LIMIT),
        name="prompt_layer",
    )(x, mod, wl["g_mix"], wl["w_in"], wl["b_gates"], wl["g_head"], wl["w_dw"], wl["b_dw"],
      wl["ln_g"], wl["ln_b"], wl["w_out"], wl["g_ffn"], wl["w_ffn13"], wl["w_ffn2"], g_final)


def _sample_in_kernel(x_ref, mod_ref, gmix_ref, win_ref, p_ref, h_ref):
    h = _rms_mod(x_ref[...], gmix_ref[...], mod_ref[:, D_MODEL:2 * D_MODEL], mod_ref[:, 0:D_MODEL])
    h_ref[...] = h.astype(BF16)
    _in_proj(h_ref, win_ref, p_ref)


def _sample_in_proj(x, mod, wl):
    rows = x.shape[0]
    return pl.pallas_call(
        _sample_in_kernel,
        out_shape=jax.ShapeDtypeStruct((rows, PROJ_PAD), F32),
        scratch_shapes=[pltpu.VMEM((rows, D_MODEL), BF16)],
        compiler_params=pltpu.CompilerParams(vmem_limit_bytes=VMEM_LIMIT),
        name="sample_in_proj",
    )(x, mod, wl["g_mix"], wl["w_in"])


def _sample_step_kernel(p_ref, c_ref, n_ref, m_ref, conv_ref, bg_ref, ghead_ref, wdw_ref, bdw_ref,
                        lng_ref, lnb_ref, *rest, bt, n_alias, fill_other):
    hm_ref, zc_ref, c_out, n_out, m_out, conv_out = rest[n_alias:]
    slot = pl.program_id(0)

    @pl.when(slot == 0)
    def _():
        scale = HEAD_DIM ** -0.5
        gates = p_ref[:, COL_G:COL_G + LANES] + bg_ref[...]
        logf = pltpu.roll(_log_sigmoid(gates), LANES - HEADS, axis=1)
        inter = logf + m_ref[...]
        m_t = jnp.maximum(inter, gates)
        dw_all = jnp.exp(gates - m_t)
        a_all = jnp.exp(inter - m_t)
        eb_all = jnp.exp(-m_t)
        m_out[...] = m_t
        for h in range(HEADS):
            hs = slice(h * HEAD_DIM, (h + 1) * HEAD_DIM)
            q = p_ref[:, COL_Q + h * HEAD_DIM:COL_Q + (h + 1) * HEAD_DIM]
            k = p_ref[:, COL_K + h * HEAD_DIM:COL_K + (h + 1) * HEAD_DIM] * scale
            v = p_ref[:, COL_V + h * HEAD_DIM:COL_V + (h + 1) * HEAD_DIM]
            o = p_ref[:, COL_O + h * HEAD_DIM:COL_O + (h + 1) * HEAD_DIM]
            dw = dw_all[:, h:h + 1]
            a_in = a_all[:, h:h + 1]
            n_h = n_ref[:, hs]
            kw = k * dw
            s = jnp.sum(q * k, axis=-1, keepdims=True) * dw
            qn = jnp.sum(q * n_h, axis=-1, keepdims=True)
            n_out[:, hs] = a_in * n_h + kw
            q_t = q.T
            kw_t = kw.T
            a_rows = jnp.broadcast_to(a_in, (bt, HEAD_DIM))
            qc_rows = []
            for b in range(bt):
                c_mat = c_ref[b, h]
                qc_rows.append(jnp.sum(q_t[:, b:b + 1] * c_mat, axis=0, keepdims=True))
                c_out[b, h] = a_rows[b:b + 1, :] * c_mat + kw_t[:, b:b + 1] * v[b:b + 1, :]
            qc = jnp.concatenate(qc_rows, axis=0)
            num = s * v + qc * a_in
            den = s + a_in * qn
            bound = jnp.maximum(jnp.abs(den), eb_all[:, h:h + 1])
            hh = num * (1.0 / bound)
            hn = hh * lax.rsqrt(jnp.mean(hh * hh, axis=-1, keepdims=True) + EPS) * ghead_ref[:, hs]
            hm_ref[:, hs] = hn * _sigmoid(o)

        u = p_ref[:, COL_A:COL_A + CONV_CH] * _sigmoid(p_ref[:, COL_B:COL_B + CONV_CH])
        z_rows = []
        for b in range(bt):
            hist = conv_ref[b]
            z_rows.append(jnp.sum(hist * wdw_ref[0:HIST, :], axis=0, keepdims=True))
            conv_out[b, 0:HIST - 1, :] = hist[1:HIST, :]
            conv_out[b, HIST - 1:HIST, :] = u[b:b + 1, :]
        z = jnp.concatenate(z_rows, axis=0) + u * wdw_ref[HIST:HIST + 1, :] + bdw_ref[...]
        zc_ref[...] = _layernorm_silu(z, lng_ref[...], lnb_ref[...])

    if fill_other:
        @pl.when(slot != 0)
        def _():
            c_out[...] = jnp.zeros_like(c_out)
            conv_out[...] = jnp.zeros_like(conv_out)


def _sample_step(p, layer, state_c, state_n2, state_m_pad, state_conv, stacked, wl, *, bt=8):
    rows = p.shape[0]
    depth = state_c.shape[0]
    assert rows % bt == 0
    nblk = rows // bt
    first = stacked is None
    n_slots = depth if first else 1

    def blk_i(slot, i):
        return jnp.where(slot == 0, i, nblk - 1)

    def lay(*s):
        return pl.BlockSpec((None, bt) + s, lambda slot, i: (layer, blk_i(slot, i)) + (0,) * len(s))

    def row(*s):
        return pl.BlockSpec((bt,) + s, lambda slot, i: (blk_i(slot, i),) + (0,) * len(s))

    def stk(*s):
        return pl.BlockSpec((None, bt) + s,
                            lambda slot, i: ((slot if first else layer), i) + (0,) * len(s))

    const = lambda *s: pl.BlockSpec(s, lambda slot, i: (0,) * len(s))
    in_specs = [
        row(PROJ_PAD), lay(HEADS, HEAD_DIM, HEAD_DIM), lay(MLSTM_WIDTH), lay(LANES),
        lay(HIST, CONV_CH),
        const(1, LANES), const(1, MLSTM_WIDTH), const(CONV_WIDTH, CONV_CH),
        const(1, CONV_CH), const(1, CONV_CH), const(1, CONV_CH),
    ]
    args = [p, state_c, state_n2, state_m_pad, state_conv, wl["b_gates"], wl["g_head"], wl["w_dw"],
            wl["b_dw"], wl["ln_g"], wl["ln_b"]]
    aliases = {}
    if not first:
        aliases = {len(args): 2, len(args) + 1: 5}
        in_specs += [pl.BlockSpec(memory_space=pl.ANY), pl.BlockSpec(memory_space=pl.ANY)]
        args += list(stacked)
    out_shape = [
        jax.ShapeDtypeStruct((rows, MLSTM_WIDTH), F32),
        jax.ShapeDtypeStruct((rows, CONV_CH), F32),
        jax.ShapeDtypeStruct(state_c.shape, F32),
        jax.ShapeDtypeStruct((rows, MLSTM_WIDTH), F32),
        jax.ShapeDtypeStruct((rows, LANES), F32),
        jax.ShapeDtypeStruct(state_conv.shape, F32),
    ]
    hm, zc, c_new, n_new, m_new, conv_new = pl.pallas_call(
        functools.partial(_sample_step_kernel, bt=bt, n_alias=len(aliases),
                          fill_other=first and depth > 1),
        grid=(n_slots, nblk),
        in_specs=in_specs,
        out_specs=[row(MLSTM_WIDTH), row(CONV_CH), stk(HEADS, HEAD_DIM, HEAD_DIM),
                   row(MLSTM_WIDTH), row(LANES), stk(HIST, CONV_CH)],
        out_shape=out_shape,
        input_output_aliases=aliases,
        compiler_params=pltpu.CompilerParams(
            dimension_semantics=("arbitrary", "arbitrary"), vmem_limit_bytes=VMEM_LIMIT),
        name="sample_step",
    )(*args)
    return hm, zc, (c_new, conv_new), n_new.reshape(rows, HEADS, HEAD_DIM), m_new[:, :HEADS]


def _sample_out_kernel(x_ref, hm_ref, zc_ref, mod_ref, wout_ref, gffn_ref, w13_ref, w2_ref,
                       gfin_ref, y_ref, mix_ref, x1_ref, h2_ref, act_ref, *, rows, final):
    mix_ref[:, 0:MLSTM_WIDTH] = hm_ref[...].astype(BF16)
    mix_ref[:, MLSTM_WIDTH:] = zc_ref[...].astype(BF16)

    def mod_rows(name, cs, r=slice(None)):
        i = _MOD_ROW[name]
        if isinstance(cs, slice) and cs == slice(None):
            cs = slice(0, D_MODEL)
        return mod_ref[r, i * D_MODEL + cs.start:i * D_MODEL + cs.stop]

    _out_proj_norm(x_ref, mix_ref, mod_rows, wout_ref, gffn_ref, x1_ref, h2_ref,
                   rows=rows, row_chunk=32)
    _run(_ffn_steps(x1_ref, h2_ref, lambda cs: mod_rows("ga2", cs), w13_ref, w2_ref, gfin_ref, y_ref,
                    act_ref, rows=rows, row_chunk=32, final=final))


def _sample_out(x, hm, zc, mod, wl, g_final, *, final):
    rows = x.shape[0]
    return pl.pallas_call(
        functools.partial(_sample_out_kernel, rows=rows, final=final),
        out_shape=jax.ShapeDtypeStruct((rows, D_MODEL), F32),
        scratch_shapes=[
            pltpu.VMEM((rows, D_MODEL), BF16),
            pltpu.VMEM((rows, D_MODEL), F32),
            pltpu.VMEM((rows, D_MODEL), BF16),
            pltpu.VMEM((rows, D_FF), BF16),
        ],
        compiler_params=pltpu.CompilerParams(vmem_limit_bytes=VMEM_LIMIT),
        name="sample_out",
    )(x, hm, zc, mod, wl["w_out"], wl["g_ffn"], wl["w_ffn13"], wl["w_ffn2"], g_final)


def _prep_layer_weights(l, g_mix, w_in, b_gates, g_head, w_dw, b_dw, ln_g, ln_b, w_out, g_ffn,
                        w_ffn13, w_ffn2):
    d = w_in.shape[1]
    n_gate = 2 * HEADS
    gate0 = 4 * MLSTM_WIDTH
    w = w_in[l]
    w_r = jnp.concatenate(
        [w[:, :gate0], w[:, gate0 + n_gate:], w[:, gate0:gate0 + n_gate],
         jnp.zeros((d, LANES - n_gate), w.dtype)], axis=1).astype(BF16)
    return {
        "g_mix": g_mix[l].reshape(1, d),
        "w_in": w_r,
        "b_gates": jnp.pad(b_gates[l], (0, LANES - n_gate)).reshape(1, LANES),
        "g_head": g_head[l].reshape(1, MLSTM_WIDTH),
        "w_dw": w_dw[l],
        "b_dw": b_dw[l].reshape(1, CONV_CH),
        "ln_g": ln_g[l].reshape(1, CONV_CH),
        "ln_b": ln_b[l].reshape(1, CONV_CH),
        "w_out": w_out[l].astype(BF16),
        "g_ffn": g_ffn[l].reshape(1, d),
        "w_ffn13": w_ffn13[l].astype(BF16),
        "w_ffn2": w_ffn2[l].astype(BF16),
    }


def _forward(x_prompt, x_sample, c_prompt, c_sample, state_C, state_n, state_m, state_conv,
             w_ada, b_ada, g_mix, w_in, b_gates, g_head, w_dw, b_dw, ln_g, ln_b, w_out,
             g_ffn, w_ffn13, w_ffn2, g_final, *, tm):
    depth = w_ada.shape[0]
    bp = x_prompt.shape[0]
    d = x_prompt.shape[-1]
    mod = _modulation(jnp.concatenate([c_prompt, c_sample], axis=0), w_ada, b_ada)
    gfin = g_final.reshape(1, d)
    xp = x_prompt
    xs = x_sample.reshape(x_sample.shape[0], d)
    outs_p = [[], [], [], []]
    ns_l, ms_l = [], []
    stacked = None
    n_seq = xs.shape[0]
    state_n2 = state_n.reshape(depth, n_seq, MLSTM_WIDTH)
    state_m_pad = jnp.pad(state_m, ((0, 0), (0, 0), (0, LANES - HEADS)))
    for l in range(depth):
        wl = _prep_layer_weights(l, g_mix, w_in, b_gates, g_head, w_dw, b_dw, ln_g, ln_b, w_out,
                                 g_ffn, w_ffn13, w_ffn2)
        final = l == depth - 1
        mod_p = mod[l, :bp].reshape(bp, 6, d)
        mod_s = mod[l, bp:]
        xp, c1, n1, m1, b1 = _prompt_layer(xp, mod_p, wl, gfin, tm=tm, final=final)
        for acc, val in zip(outs_p, (c1, n1, m1[:, :, 0], b1)):
            acc.append(val)
        p = _sample_in_proj(xs, mod_s, wl)
        hm, zc, stacked, n2, m2 = _sample_step(p, l, state_C, state_n2, state_m_pad, state_conv,
                                               stacked, wl)
        xs = _sample_out(xs, hm, zc, mod_s, wl, gfin, final=final)
        ns_l.append(n2)
        ms_l.append(m2)
    y_sample = xs.reshape(x_sample.shape)
    c_sample_new, conv_sample_new = stacked
    return ((xp, y_sample) + tuple(jnp.stack(a) for a in outs_p)
            + (c_sample_new, jnp.stack(ns_l), jnp.stack(ms_l), conv_sample_new))


def kernel(x_prompt, x_sample, c_prompt, c_sample, state_C, state_n, state_m, state_conv, w_ada, b_ada, g_mix, w_in, b_gates, g_head, w_dw, b_dw, ln_g, ln_b, w_out, g_ffn, w_ffn13, w_ffn2, g_final):
    return _forward(x_prompt, x_sample, c_prompt, c_sample, state_C, state_n, state_m, state_conv,
                    w_ada, b_ada, g_mix, w_in, b_gates, g_head, w_dw, b_dw, ln_g, ln_b, w_out,
                    g_ffn, w_ffn13, w_ffn2, g_final, tm=256)
```

```python
import functools
import itertools

import jax
import jax.numpy as jnp
from jax import lax
from jax.experimental import pallas as pl
from jax.experimental.pallas import tpu as pltpu

F32 = jnp.float32
BF16 = jnp.bfloat16

D_MODEL = 1024
HEADS = 4
HEAD_DIM = 128
MLSTM_WIDTH = HEADS * HEAD_DIM
CONV_CH = D_MODEL - MLSTM_WIDTH
CONV_WIDTH = 31
HIST = CONV_WIDTH - 1
D_FF = 2816
CHUNK = 128
EPS = 1e-6

LANES = 128
SUBLANES = 8
MXU_N = 256

COL_Q = 0
COL_K = MLSTM_WIDTH
COL_V = 2 * MLSTM_WIDTH
COL_O = 3 * MLSTM_WIDTH
COL_A = 4 * MLSTM_WIDTH
COL_B = COL_A + CONV_CH
COL_G = COL_B + CONV_CH
PROJ_PAD = COL_G + LANES
HIST_PAD = 32
CONV_SLABS = CONV_CH // LANES
CONV_STRIDE = 2
VMEM_LIMIT = 60 * 1024 * 1024


def _log_sigmoid(x):
    return jnp.minimum(x, 0.0) - jnp.log1p(jnp.exp(-jnp.abs(x)))


def _sigmoid(x):
    return 1.0 / (1.0 + jnp.exp(-x))


def _bdot(a, b):
    return jnp.dot(a, b, preferred_element_type=F32)


def _bdot_nt(a, b):
    return lax.dot_general(a, b, (((1,), (1,)), ((), ())), preferred_element_type=F32)


def _row_slices(rows, chunk):
    chunk = min(chunk, rows)
    assert rows % chunk == 0
    return [slice(r0, r0 + chunk) for r0 in range(0, rows, chunk)]


def _rms_mod(x, g, sc, sh):
    ms = jnp.mean(x * x, axis=-1, keepdims=True)
    return (x * lax.rsqrt(ms + EPS) * g) * (1.0 + sc) + sh


def _mod_kernel(c_ref, w_ref, b_ref, o_ref):
    o_ref[...] = _bdot(c_ref[...].astype(BF16), w_ref[...].astype(BF16)) + b_ref[...]


def _modulation(c_all, w_ada, b_ada):
    depth, d, n6 = w_ada.shape
    rows = c_all.shape[0]
    tn = 1024
    return pl.pallas_call(
        _mod_kernel,
        grid=(depth, n6 // tn),
        in_specs=[
            pl.BlockSpec((rows, d), lambda l, n: (0, 0)),
            pl.BlockSpec((None, d, tn), lambda l, n: (l, 0, n)),
            pl.BlockSpec((None, 1, tn), lambda l, n: (l, 0, n)),
        ],
        out_specs=pl.BlockSpec((None, rows, tn), lambda l, n: (l, 0, n)),
        out_shape=jax.ShapeDtypeStruct((depth, rows, n6), F32),
        compiler_params=pltpu.CompilerParams(
            dimension_semantics=("arbitrary", "arbitrary"), vmem_limit_bytes=VMEM_LIMIT),
        name="adaln_mod",
    )(c_all, w_ada, b_ada.reshape(depth, 1, n6))


def _norm_steps(src_ref, dst_ref, g_ref, mod, sc, sh, *, rows, row_chunk):
    for r in _row_slices(rows, row_chunk):
        h = _rms_mod(src_ref[r, :], g_ref[...], mod(sc, slice(None), r), mod(sh, slice(None), r))
        dst_ref[r, :] = h.astype(BF16)
        yield


def _in_proj_steps(h_ref, win_ref, p_ref):
    for n0 in range(0, PROJ_PAD, MXU_N):
        n1 = min(n0 + MXU_N, PROJ_PAD)
        p_ref[:, n0:n1] = _bdot(h_ref[...], win_ref[:, n0:n1])
        yield


def _out_proj_steps(x_ref, mix_ref, mod, wout_ref, x1_ref):
    for n0 in range(0, D_MODEL, MXU_N):
        cs = slice(n0, n0 + MXU_N)
        mx = _bdot(mix_ref[...], wout_ref[:, cs])
        x1_ref[:, cs] = x_ref[:, cs] + mod("ga1", cs, slice(None)) * mx
        yield


def _ffn_gate_steps(h2_ref, w13_ref, act_ref):
    for n0 in range(0, D_FF, MXU_N):
        gg = _bdot(h2_ref[...], w13_ref[:, n0:n0 + MXU_N])
        uu = _bdot(h2_ref[...], w13_ref[:, D_FF + n0:D_FF + n0 + MXU_N])
        act_ref[:, n0:n0 + MXU_N] = (gg * _sigmoid(gg) * uu).astype(BF16)
        yield


def _ffn_out_steps(x1_ref, act_ref, ga2, w2_ref, gfin_ref, y_ref, *, rows, row_chunk, final):
    dst = x1_ref if final else y_ref
    for n0 in range(0, D_MODEL, MXU_N):
        cs = slice(n0, n0 + MXU_N)
        ff = _bdot(act_ref[...], w2_ref[:, cs])
        dst[:, cs] = x1_ref[:, cs] + ga2(cs) * ff
        yield
    if final:
        for r in _row_slices(rows, row_chunk):
            xb = x1_ref[r, :]
            ms = jnp.mean(xb * xb, axis=-1, keepdims=True)
            y_ref[r, :] = xb * lax.rsqrt(ms + EPS) * gfin_ref[...]
            yield


def _run(*gens):
    live = [g if isinstance(g, tuple) else (g, 1) for g in gens]
    while live:
        for entry in list(live):
            g, k = entry
            for _ in range(k):
                try:
                    next(g)
                except StopIteration:
                    live.remove(entry)
                    break


def _layernorm_silu(z, lng, lnb):
    mu = jnp.mean(z, axis=-1, keepdims=True)
    zc = z - mu
    var = jnp.mean(zc * zc, axis=-1, keepdims=True)
    zn = zc * lax.rsqrt(var + EPS) * lng + lnb
    return zn * _sigmoid(zn)


_MOD_ROW = {"sh1": 0, "sc1": 1, "ga1": 2, "sh2": 3, "sc2": 4, "ga2": 5}


def _cumsum_chunk(tril_bf, x):
    hi = x.astype(BF16)
    r1 = x - hi.astype(F32)
    mid = r1.astype(BF16)
    lo = (r1 - mid.astype(F32)).astype(BF16)
    return _bdot(tril_bf, hi) + _bdot(tril_bf, mid) + _bdot(tril_bf, lo)


def _mlstm_chunk(p_ref, r0, bg_ref, ghead_ref, mix_ref, C_ref, n_ref, m_ref, tril, tril_bf):
    rs = slice(r0, r0 + CHUNK)
    gates = p_ref[rs, COL_G:COL_G + LANES] + bg_ref[...]
    fcum = _cumsum_chunk(tril_bf, _log_sigmoid(gates))
    gates_t = gates.T
    fcum_t = fcum.T
    yield
    heads = range(HEADS)
    col = lambda base, h: slice(base + h * HEAD_DIM, base + (h + 1) * HEAD_DIM)
    st = [dict() for _ in heads]
    for h in heads:
        d = st[h]
        d["q"] = p_ref[rs, col(COL_Q, h)]
        d["k"] = p_ref[rs, col(COL_K, h)] * (HEAD_DIM ** -0.5)
        d["v_bf"] = p_ref[rs, col(COL_V, h)].astype(BF16)
        d["q_bf"] = d["q"].astype(BF16)
        d["f_col"] = fcum[:, HEADS + h:HEADS + h + 1]
        i_col = gates[:, h:h + 1]
        f_row = fcum_t[HEADS + h:HEADS + h + 1, :]
        i_row = gates_t[h:h + 1, :]
        g = f_row[:, CHUNK - 1:CHUNK]
        d["g"] = g
        d["dlog"] = jnp.where(tril, d["f_col"] - f_row + i_row, -jnp.inf)
        d["dmax"] = jnp.max(d["dlog"], axis=-1, keepdims=True)
        d["wlog"] = g - d["f_col"] + i_col
        d["wmax"] = jnp.max(g - f_row + i_row, axis=-1, keepdims=True)
        d["qk"] = _bdot_nt(d["q_bf"], d["k"].astype(BF16))
    yield
    for h in heads:
        d = st[h]
        m_prev = m_ref[h:h + 1, 0:1]
        inter = d["f_col"] + m_prev
        m_t = jnp.maximum(inter, d["dmax"])
        d["a_inter"] = jnp.exp(inter - m_t)
        d["ebound"] = jnp.exp(-m_t)
        s = d["qk"] * jnp.exp(d["dlog"] - m_t)
        d["ssum"] = jnp.sum(s, axis=-1, keepdims=True)
        d["sv"] = _bdot(s.astype(BF16), d["v_bf"])
        m_new = jnp.maximum(d["g"] + m_prev, d["wmax"])
        d["decay"] = jnp.exp(d["g"] + m_prev - m_new)
        kw = d["k"] * jnp.exp(d["wlog"] - m_new)
        d["kwsum"] = jnp.sum(kw, axis=0, keepdims=True)
        d["kv"] = _bdot(kw.T.astype(BF16), d["v_bf"])
        m_ref[h:h + 1, :] = jnp.broadcast_to(m_new, (1, LANES))
    yield
    for h in heads:
        d = st[h]
        n_row = n_ref[h:h + 1, :]
        c_mat = C_ref[h]
        num = d["sv"] + _bdot(d["q_bf"], c_mat.astype(BF16)) * d["a_inter"]
        qn = jnp.sum(d["q"] * n_row, axis=-1, keepdims=True)
        den = d["ssum"] + d["a_inter"] * qn
        bound = jnp.maximum(jnp.abs(den), d["ebound"])
        hh = num * (1.0 / bound)
        C_ref[h] = d["decay"] * c_mat + d["kv"]
        n_ref[h:h + 1, :] = d["decay"] * n_row + d["kwsum"]
        hn = hh * lax.rsqrt(jnp.mean(hh * hh, axis=-1, keepdims=True) + EPS) * ghead_ref[:, col(0, h)]
        mix_ref[rs, col(0, h)] = (hn * _sigmoid(p_ref[rs, col(COL_O, h)])).astype(BF16)
    yield


def _prompt_layer_kernel(x_ref, mod1_ref, mod2_ref, gmix_ref, win_ref, bg_ref, ghead_ref, wdw_ref,
                         bdw_ref, lng_ref, lnb_ref, wout_ref, gffn_ref, w13_ref, w2_ref, gfin_ref,
                         y_ref, c_out, n_out, m_out, conv_out,
                         h_ref, p_ref, mix_ref, xres_ref, x1_ref, h2_ref, ga2_ref, act_ref, ubuf_ref,
                         zc_ref, C_ref, n_ref, m_ref, *, tm, tiles_per_seq, final):
    s = pl.program_id(0)
    n_tiles = pl.num_programs(0) - 2
    j = jnp.clip(s - 1, 0, n_tiles - 1) % tiles_per_seq
    last = tiles_per_seq - 1
    row_chunk = 32

    @pl.when(s == 0)
    def _():
        for ref in (p_ref, xres_ref, x1_ref, h2_ref, ga2_ref):
            ref[...] = jnp.zeros_like(ref)

    @pl.when(j == 0)
    def _():
        C_ref[...] = jnp.zeros_like(C_ref)
        n_ref[...] = jnp.zeros_like(n_ref)
        m_ref[...] = jnp.zeros_like(m_ref)
        ubuf_ref[:, 0:HIST_PAD, :] = jnp.zeros((CONV_SLABS, HIST_PAD, LANES), F32)

    @pl.when(j > 0)
    def _():
        ubuf_ref[:, 0:HIST_PAD, :] = ubuf_ref[:, tm:tm + HIST_PAD, :]

    def mod1(name, cs, r=None):
        i = _MOD_ROW[name]
        return mod1_ref[i:i + 1, cs]

    def mod2(name, cs, r=None):
        i = _MOD_ROW[name]
        return mod2_ref[i:i + 1, cs]

    row_id = lax.broadcasted_iota(jnp.int32, (CHUNK, CHUNK), 0)
    col_id = lax.broadcasted_iota(jnp.int32, (CHUNK, CHUNK), 1)
    tril = row_id >= col_id
    tril_bf = jnp.where(tril, 1.0, 0.0).astype(BF16)

    def mlstm_steps():
        for c in range(tm // CHUNK):
            yield from _mlstm_chunk(p_ref, c * CHUNK, bg_ref, ghead_ref, mix_ref, C_ref, n_ref,
                                    m_ref, tril, tril_bf)

    def glu_steps():
        for r in _row_slices(tm, row_chunk):
            ro = slice(r.start + HIST_PAD, r.stop + HIST_PAD)
            u = p_ref[r, COL_A:COL_A + CONV_CH] * _sigmoid(p_ref[r, COL_B:COL_B + CONV_CH])
            for g in range(CONV_SLABS):
                ubuf_ref[g, ro, :] = u[:, g * LANES:(g + 1) * LANES]
            yield

    def conv_steps():
        n_sets = row_chunk // SUBLANES
        firsts = [(t0 // CONV_STRIDE) * CONV_STRIDE * SUBLANES + t0 % CONV_STRIDE
                  for t0 in range(n_sets)]
        for g in range(CONV_SLABS):
            gs = slice(g * LANES, (g + 1) * LANES)
            bias = jnp.broadcast_to(bdw_ref[:, gs], (SUBLANES, LANES))
            for base in range(0, tm, row_chunk):
                acc = [bias for _ in range(n_sets)]
                for w in range(CONV_WIDTH):
                    tap = wdw_ref[w:w + 1, gs]
                    for t0 in range(n_sets):
                        src = pl.ds(base + firsts[t0] + HIST_PAD - HIST + w, SUBLANES,
                                    stride=CONV_STRIDE)
                        acc[t0] = acc[t0] + ubuf_ref[g, src, :] * tap
                for t0 in range(n_sets):
                    zc_ref[g, pl.ds(base + firsts[t0], SUBLANES, stride=CONV_STRIDE), :] = acc[t0]
                yield

    def conv_norm_steps():
        for r in _row_slices(tm, row_chunk):
            z = jnp.concatenate([zc_ref[g, r, :] for g in range(CONV_SLABS)], axis=-1)
            mix_ref[r, MLSTM_WIDTH:] = _layernorm_silu(z, lng_ref[...], lnb_ref[...]).astype(BF16)
            yield

    def keep_steps():
        xres_ref[...] = x_ref[...]
        ga2_ref[...] = mod2("ga2", slice(None))
        yield

    _run(_ffn_gate_steps(h2_ref, w13_ref, act_ref), mlstm_steps(), glu_steps())

    @pl.when(s >= 0)
    def _():
        _run((conv_steps(), 6),
             _ffn_out_steps(x1_ref, act_ref, lambda cs: ga2_ref[:, cs], w2_ref, gfin_ref, y_ref,
                            rows=tm, row_chunk=row_chunk, final=final))

    @pl.when(jnp.logical_and(j == last, jnp.logical_and(s >= 1, s <= n_tiles)))
    def _():
        c_out[...] = C_ref[...]
        n_out[...] = n_ref[0:HEADS, :]
        m_out[...] = m_ref[0:HEADS, :]
        for g in range(CONV_SLABS):
            conv_out[:, g * LANES:(g + 1) * LANES] = ubuf_ref[g, tm + HIST_PAD - HIST:tm + HIST_PAD, :]

    @pl.when(s >= 0)
    def _():
        _run(itertools.chain(conv_norm_steps(),
                             _out_proj_steps(xres_ref, mix_ref, mod2, wout_ref, x1_ref),
                             _norm_steps(x1_ref, h2_ref, gffn_ref, mod2, "sc2", "sh2",
                                         rows=tm, row_chunk=row_chunk)),
             itertools.chain(_norm_steps(x_ref, h_ref, gmix_ref, mod1, "sc1", "sh1",
                                         rows=tm, row_chunk=row_chunk),
                             _in_proj_steps(h_ref, win_ref, p_ref)))
        _run(keep_steps())


def _resident(shape):
    nd = len(shape)
    return pl.BlockSpec(shape, lambda s: (0,) * nd, pipeline_mode=pl.Buffered(1))


def _prompt_layer(x, mod, wl, g_final, *, tm, final):
    bsz, seq, d = x.shape
    assert seq % tm == 0 and tm % CHUNK == 0 and tm >= HIST_PAD
    nt = seq // tm
    n_tiles = bsz * nt
    kern = functools.partial(_prompt_layer_kernel, tm=tm, tiles_per_seq=nt, final=final)

    def tile(s, lag):
        return jnp.clip(s - lag, 0, n_tiles - 1)

    def cur(s):
        return tile(s, 1)

    def prev(s):
        return tile(s, 2)

    in_specs = [
        pl.BlockSpec((None, tm, d), lambda s: (tile(s, 0) // nt, tile(s, 0) % nt, 0)),
        pl.BlockSpec((None, 6, d), lambda s: (tile(s, 0) // nt, 0, 0)),
        pl.BlockSpec((None, 6, d), lambda s: (cur(s) // nt, 0, 0)),
        _resident((1, d)),
        _resident((d, PROJ_PAD)),
        _resident((1, LANES)),
        _resident((1, MLSTM_WIDTH)),
        _resident((CONV_WIDTH, CONV_CH)),
        _resident((1, CONV_CH)),
        _resident((1, CONV_CH)),
        _resident((1, CONV_CH)),
        _resident((d, d)),
        _resident((1, d)),
        _resident((d, 2 * D_FF)),
        _resident((D_FF, d)),
        _resident((1, d)),
    ]
    out_specs = [
        pl.BlockSpec((None, tm, d), lambda s: (prev(s) // nt, prev(s) % nt, 0)),
        pl.BlockSpec((None, HEADS, HEAD_DIM, HEAD_DIM), lambda s: (cur(s) // nt, 0, 0, 0)),
        pl.BlockSpec((None, HEADS, HEAD_DIM), lambda s: (cur(s) // nt, 0, 0)),
        pl.BlockSpec((None, HEADS, LANES), lambda s: (cur(s) // nt, 0, 0)),
        pl.BlockSpec((None, HIST, CONV_CH), lambda s: (cur(s) // nt, 0, 0)),
    ]
    out_shape = [
        jax.ShapeDtypeStruct((bsz, seq, d), F32),
        jax.ShapeDtypeStruct((bsz, HEADS, HEAD_DIM, HEAD_DIM), F32),
        jax.ShapeDtypeStruct((bsz, HEADS, HEAD_DIM), F32),
        jax.ShapeDtypeStruct((bsz, HEADS, LANES), F32),
        jax.ShapeDtypeStruct((bsz, HIST, CONV_CH), F32),
    ]
    scratch = [
        pltpu.VMEM((tm, d), BF16),
        pltpu.VMEM((tm, PROJ_PAD), F32),
        pltpu.VMEM((tm, d), BF16),
        pltpu.VMEM((tm, d), F32),
        pltpu.VMEM((tm, d), F32),
        pltpu.VMEM((tm, d), BF16),
        pltpu.VMEM((1, d), F32),
        pltpu.VMEM((tm, D_FF), BF16),
        pltpu.VMEM((CONV_SLABS, tm + HIST_PAD, LANES), F32),
        pltpu.VMEM((CONV_SLABS, tm, LANES), F32),
        pltpu.VMEM((HEADS, HEAD_DIM, HEAD_DIM), F32),
        pltpu.VMEM((SUBLANES, HEAD_DIM), F32),
        pltpu.VMEM((SUBLANES, LANES), F32),
    ]
    return pl.pallas_call(
        kern,
        grid=(n_tiles + 2,),
        in_specs=in_specs,
        out_specs=out_specs,
        out_shape=out_shape,
        scratch_shapes=scratch,
        compiler_params=pltpu.CompilerParams(
            dimension_semantics=("arbitrary",), vmem_limit_bytes=VMEM_LIMIT),
        name="prompt_layer",
    )(x, mod, mod, wl["g_mix"], wl["w_in"], wl["b_gates"], wl["g_head"], wl["w_dw"], wl["b_dw"],
      wl["ln_g"], wl["ln_b"], wl["w_out"], wl["g_ffn"], wl["w_ffn13"], wl["w_ffn2"], g_final)


def _sample_in_kernel(x_ref, mod_ref, gmix_ref, win_ref, p_ref, h_ref):
    h = _rms_mod(x_ref[...], gmix_ref[...], mod_ref[:, D_MODEL:2 * D_MODEL], mod_ref[:, 0:D_MODEL])
    h_ref[...] = h.astype(BF16)
    _run(_in_proj_steps(h_ref, win_ref, p_ref))


def _sample_in_proj(x, mod, wl):
    rows = x.shape[0]
    return pl.pallas_call(
        _sample_in_kernel,
        out_shape=jax.ShapeDtypeStruct((rows, PROJ_PAD), F32),
        scratch_shapes=[pltpu.VMEM((rows, D_MODEL), BF16)],
        compiler_params=pltpu.CompilerParams(vmem_limit_bytes=VMEM_LIMIT),
        name="sample_in_proj",
    )(x, mod, wl["g_mix"], wl["w_in"])


def _sample_step_kernel(p_ref, c_ref, n_ref, m_ref, conv_ref, bg_ref, ghead_ref, wdw_ref, bdw_ref,
                        lng_ref, lnb_ref, *rest, bt, n_alias, fill_other):
    hm_ref, zc_ref, c_out, n_out, m_out, conv_out = rest[n_alias:]
    slot = pl.program_id(0)

    @pl.when(slot == 0)
    def _():
        scale = HEAD_DIM ** -0.5
        gates = p_ref[:, COL_G:COL_G + LANES] + bg_ref[...]
        logf = pltpu.roll(_log_sigmoid(gates), LANES - HEADS, axis=1)
        inter = logf + m_ref[...]
        m_t = jnp.maximum(inter, gates)
        dw_all = jnp.exp(gates - m_t)
        a_all = jnp.exp(inter - m_t)
        eb_all = jnp.exp(-m_t)
        m_out[...] = m_t
        for h in range(HEADS):
            hs = slice(h * HEAD_DIM, (h + 1) * HEAD_DIM)
            q = p_ref[:, COL_Q + h * HEAD_DIM:COL_Q + (h + 1) * HEAD_DIM]
            k = p_ref[:, COL_K + h * HEAD_DIM:COL_K + (h + 1) * HEAD_DIM] * scale
            v = p_ref[:, COL_V + h * HEAD_DIM:COL_V + (h + 1) * HEAD_DIM]
            o = p_ref[:, COL_O + h * HEAD_DIM:COL_O + (h + 1) * HEAD_DIM]
            dw = dw_all[:, h:h + 1]
            a_in = a_all[:, h:h + 1]
            n_h = n_ref[:, hs]
            kw = k * dw
            s = jnp.sum(q * k, axis=-1, keepdims=True) * dw
            qn = jnp.sum(q * n_h, axis=-1, keepdims=True)
            n_out[:, hs] = a_in * n_h + kw
            q_t = q.T
            kw_t = kw.T
            a_rows = jnp.broadcast_to(a_in, (bt, HEAD_DIM))
            qc_rows = []
            for b in range(bt):
                c_mat = c_ref[b, h]
                qc_rows.append(jnp.sum(q_t[:, b:b + 1] * c_mat, axis=0, keepdims=True))
                c_out[b, h] = a_rows[b:b + 1, :] * c_mat + kw_t[:, b:b + 1] * v[b:b + 1, :]
            qc = jnp.concatenate(qc_rows, axis=0)
            num = s * v + qc * a_in
            den = s + a_in * qn
            bound = jnp.maximum(jnp.abs(den), eb_all[:, h:h + 1])
            hh = num * (1.0 / bound)
            hn = hh * lax.rsqrt(jnp.mean(hh * hh, axis=-1, keepdims=True) + EPS) * ghead_ref[:, hs]
            hm_ref[:, hs] = hn * _sigmoid(o)

        u = p_ref[:, COL_A:COL_A + CONV_CH] * _sigmoid(p_ref[:, COL_B:COL_B + CONV_CH])
        z_rows = []
        for b in range(bt):
            hist = conv_ref[b]
            z_rows.append(jnp.sum(hist * wdw_ref[0:HIST, :], axis=0, keepdims=True))
            conv_out[b, 0:HIST - 1, :] = hist[1:HIST, :]
            conv_out[b, HIST - 1:HIST, :] = u[b:b + 1, :]
        z = jnp.concatenate(z_rows, axis=0) + u * wdw_ref[HIST:HIST + 1, :] + bdw_ref[...]
        zc_ref[...] = _layernorm_silu(z, lng_ref[...], lnb_ref[...])

    if fill_other:
        @pl.when(slot != 0)
        def _():
            c_out[...] = jnp.zeros_like(c_out)
            conv_out[...] = jnp.zeros_like(conv_out)


def _sample_step(p, layer, state_c, state_n2, state_m_pad, state_conv, stacked, wl, *, bt=8):
    rows = p.shape[0]
    depth = state_c.shape[0]
    assert rows % bt == 0
    nblk = rows // bt
    first = stacked is None
    n_slots = depth if first else 1

    def blk_i(slot, i):
        return jnp.where(slot == 0, i, nblk - 1)

    def lay(*s):
        return pl.BlockSpec((None, bt) + s, lambda slot, i: (layer, blk_i(slot, i)) + (0,) * len(s))

    def row(*s):
        return pl.BlockSpec((bt,) + s, lambda slot, i: (blk_i(slot, i),) + (0,) * len(s))

    def stk(*s):
        return pl.BlockSpec((None, bt) + s,
                            lambda slot, i: ((slot if first else layer), i) + (0,) * len(s))

    const = lambda *s: pl.BlockSpec(s, lambda slot, i: (0,) * len(s))
    in_specs = [
        row(PROJ_PAD), lay(HEADS, HEAD_DIM, HEAD_DIM), lay(MLSTM_WIDTH), lay(LANES),
        lay(HIST, CONV_CH),
        const(1, LANES), const(1, MLSTM_WIDTH), const(CONV_WIDTH, CONV_CH),
        const(1, CONV_CH), const(1, CONV_CH), const(1, CONV_CH),
    ]
    args = [p, state_c, state_n2, state_m_pad, state_conv, wl["b_gates"], wl["g_head"], wl["w_dw"],
            wl["b_dw"], wl["ln_g"], wl["ln_b"]]
    aliases = {}
    if not first:
        aliases = {len(args): 2, len(args) + 1: 5}
        in_specs += [pl.BlockSpec(memory_space=pl.ANY), pl.BlockSpec(memory_space=pl.ANY)]
        args += list(stacked)
    out_shape = [
        jax.ShapeDtypeStruct((rows, MLSTM_WIDTH), F32),
        jax.ShapeDtypeStruct((rows, CONV_CH), F32),
        jax.ShapeDtypeStruct(state_c.shape, F32),
        jax.ShapeDtypeStruct((rows, MLSTM_WIDTH), F32),
        jax.ShapeDtypeStruct((rows, LANES), F32),
        jax.ShapeDtypeStruct(state_conv.shape, F32),
    ]
    hm, zc, c_new, n_new, m_new, conv_new = pl.pallas_call(
        functools.partial(_sample_step_kernel, bt=bt, n_alias=len(aliases),
                          fill_other=first and depth > 1),
        grid=(n_slots, nblk),
        in_specs=in_specs,
        out_specs=[row(MLSTM_WIDTH), row(CONV_CH), stk(HEADS, HEAD_DIM, HEAD_DIM),
                   row(MLSTM_WIDTH), row(LANES), stk(HIST, CONV_CH)],
        out_shape=out_shape,
        input_output_aliases=aliases,
        compiler_params=pltpu.CompilerParams(
            dimension_semantics=("arbitrary", "arbitrary"), vmem_limit_bytes=VMEM_LIMIT),
        name="sample_step",
    )(*args)
    return hm, zc, (c_new, conv_new), n_new.reshape(rows, HEADS, HEAD_DIM), m_new[:, :HEADS]


def _sample_out_kernel(x_ref, hm_ref, zc_ref, mod_ref, wout_ref, gffn_ref, w13_ref, w2_ref,
                       gfin_ref, y_ref, mix_ref, x1_ref, h2_ref, act_ref, *, rows, final):
    mix_ref[:, 0:MLSTM_WIDTH] = hm_ref[...].astype(BF16)
    mix_ref[:, MLSTM_WIDTH:] = zc_ref[...].astype(BF16)

    def mod_rows(name, cs, r=slice(None)):
        i = _MOD_ROW[name]
        if isinstance(cs, slice) and cs == slice(None):
            cs = slice(0, D_MODEL)
        return mod_ref[r, i * D_MODEL + cs.start:i * D_MODEL + cs.stop]

    _run(itertools.chain(
        _out_proj_steps(x_ref, mix_ref, mod_rows, wout_ref, x1_ref),
        _norm_steps(x1_ref, h2_ref, gffn_ref, mod_rows, "sc2", "sh2", rows=rows, row_chunk=32),
        _ffn_gate_steps(h2_ref, w13_ref, act_ref),
        _ffn_out_steps(x1_ref, act_ref, lambda cs: mod_rows("ga2", cs), w2_ref, gfin_ref, y_ref,
                       rows=rows, row_chunk=32, final=final)))


def _sample_out(x, hm, zc, mod, wl, g_final, *, final):
    rows = x.shape[0]
    return pl.pallas_call(
        functools.partial(_sample_out_kernel, rows=rows, final=final),
        out_shape=jax.ShapeDtypeStruct((rows, D_MODEL), F32),
        scratch_shapes=[
            pltpu.VMEM((rows, D_MODEL), BF16),
            pltpu.VMEM((rows, D_MODEL), F32),
            pltpu.VMEM((rows, D_MODEL), BF16),
            pltpu.VMEM((rows, D_FF), BF16),
        ],
        compiler_params=pltpu.CompilerParams(vmem_limit_bytes=VMEM_LIMIT),
        name="sample_out",
    )(x, hm, zc, mod, wl["w_out"], wl["g_ffn"], wl["w_ffn13"], wl["w_ffn2"], g_final)


def _prep_layer_weights(l, g_mix, w_in, b_gates, g_head, w_dw, b_dw, ln_g, ln_b, w_out, g_ffn,
                        w_ffn13, w_ffn2):
    d = w_in.shape[1]
    n_gate = 2 * HEADS
    gate0 = 4 * MLSTM_WIDTH
    w = w_in[l]
    w_r = jnp.concatenate(
        [w[:, :gate0], w[:, gate0 + n_gate:], w[:, gate0:gate0 + n_gate],
         jnp.zeros((d, LANES - n_gate), w.dtype)], axis=1).astype(BF16)
    return {
        "g_mix": g_mix[l].reshape(1, d),
        "w_in": w_r,
        "b_gates": jnp.pad(b_gates[l], (0, LANES - n_gate)).reshape(1, LANES),
        "g_head": g_head[l].reshape(1, MLSTM_WIDTH),
        "w_dw": w_dw[l],
        "b_dw": b_dw[l].reshape(1, CONV_CH),
        "ln_g": ln_g[l].reshape(1, CONV_CH),
        "ln_b": ln_b[l].reshape(1, CONV_CH),
        "w_out": w_out[l].astype(BF16),
        "g_ffn": g_ffn[l].reshape(1, d),
        "w_ffn13": w_ffn13[l].astype(BF16),
        "w_ffn2": w_ffn2[l].astype(BF16),
    }


def _forward(x_prompt, x_sample, c_prompt, c_sample, state_C, state_n, state_m, state_conv,
             w_ada, b_ada, g_mix, w_in, b_gates, g_head, w_dw, b_dw, ln_g, ln_b, w_out,
             g_ffn, w_ffn13, w_ffn2, g_final, *, tm):
    depth = w_ada.shape[0]
    bp = x_prompt.shape[0]
    d = x_prompt.shape[-1]
    mod = _modulation(jnp.concatenate([c_prompt, c_sample], axis=0), w_ada, b_ada)
    gfin = g_final.reshape(1, d)
    xp = x_prompt
    xs = x_sample.reshape(x_sample.shape[0], d)
    outs_p = [[], [], [], []]
    ns_l, ms_l = [], []
    stacked = None
    n_seq = xs.shape[0]
    state_n2 = state_n.reshape(depth, n_seq, MLSTM_WIDTH)
    state_m_pad = jnp.pad(state_m, ((0, 0), (0, 0), (0, LANES - HEADS)))
    for l in range(depth):
        wl = _prep_layer_weights(l, g_mix, w_in, b_gates, g_head, w_dw, b_dw, ln_g, ln_b, w_out,
                                 g_ffn, w_ffn13, w_ffn2)
        final = l == depth - 1
        mod_p = mod[l, :bp].reshape(bp, 6, d)
        mod_s = mod[l, bp:]
        xp, c1, n1, m1, b1 = _prompt_layer(xp, mod_p, wl, gfin, tm=tm, final=final)
        for acc, val in zip(outs_p, (c1, n1, m1[:, :, 0], b1)):
            acc.append(val)
        p = _sample_in_proj(xs, mod_s, wl)
        hm, zc, stacked, n2, m2 = _sample_step(p, l, state_C, state_n2, state_m_pad, state_conv,
                                               stacked, wl)
        xs = _sample_out(xs, hm, zc, mod_s, wl, gfin, final=final)
        ns_l.append(n2)
        ms_l.append(m2)
    y_sample = xs.reshape(x_sample.shape)
    c_sample_new, conv_sample_new = stacked
    return ((xp, y_sample) + tuple(jnp.stack(a) for a in outs_p)
            + (c_sample_new, jnp.stack(ns_l), jnp.stack(ms_l), conv_sample_new))


def kernel(x_prompt, x_sample, c_prompt, c_sample, state_C, state_n, state_m, state_conv, w_ada, b_ada, g_mix, w_in, b_gates, g_head, w_dw, b_dw, ln_g, ln_b, w_out, g_ffn, w_ffn13, w_ffn2, g_final):
    return _forward(x_prompt, x_sample, c_prompt, c_sample, state_C, state_n, state_m, state_conv,
                    w_ada, b_ada, g_mix, w_in, b_gates, g_head, w_dw, b_dw, ln_g, ln_b, w_out,
                    g_ffn, w_ffn13, w_ffn2, g_final, tm=512)
```

```python
import functools
import itertools

import jax
import jax.numpy as jnp
from jax import lax
from jax.experimental import pallas as pl
from jax.experimental.pallas import tpu as pltpu

F32 = jnp.float32
BF16 = jnp.bfloat16

D_MODEL = 1024
HEADS = 4
HEAD_DIM = 128
MLSTM_WIDTH = HEADS * HEAD_DIM
CONV_CH = D_MODEL - MLSTM_WIDTH
CONV_WIDTH = 31
HIST = CONV_WIDTH - 1
D_FF = 2816
CHUNK = 128
EPS = 1e-6

LANES = 128
SUBLANES = 8
MXU_N = 256

COL_Q = 0
COL_K = MLSTM_WIDTH
COL_V = 2 * MLSTM_WIDTH
COL_O = 3 * MLSTM_WIDTH
COL_A = 4 * MLSTM_WIDTH
COL_B = COL_A + CONV_CH
COL_G = COL_B + CONV_CH
PROJ_PAD = COL_G + LANES
HIST_PAD = 32
CONV_SLABS = CONV_CH // LANES
CONV_STRIDE = 2
VMEM_LIMIT = 60 * 1024 * 1024


def _log_sigmoid(x):
    return jnp.minimum(x, 0.0) - jnp.log1p(jnp.exp(-jnp.abs(x)))


def _sigmoid(x):
    return 1.0 / (1.0 + jnp.exp(-x))


def _bdot(a, b):
    return jnp.dot(a, b, preferred_element_type=F32)


def _bdot_nt(a, b):
    return lax.dot_general(a, b, (((1,), (1,)), ((), ())), preferred_element_type=F32)


def _row_slices(rows, chunk):
    chunk = min(chunk, rows)
    assert rows % chunk == 0
    return [slice(r0, r0 + chunk) for r0 in range(0, rows, chunk)]


def _rms_mod(x, g, sc, sh):
    ms = jnp.mean(x * x, axis=-1, keepdims=True)
    return (x * lax.rsqrt(ms + EPS) * g) * (1.0 + sc) + sh


def _mod_kernel(c_ref, w_ref, b_ref, o_ref):
    o_ref[...] = _bdot(c_ref[...].astype(BF16), w_ref[...].astype(BF16)) + b_ref[...]


def _modulation(c_all, w_ada, b_ada):
    depth, d, n6 = w_ada.shape
    rows = c_all.shape[0]
    tn = 1024
    return pl.pallas_call(
        _mod_kernel,
        grid=(depth, n6 // tn),
        in_specs=[
            pl.BlockSpec((rows, d), lambda l, n: (0, 0)),
            pl.BlockSpec((None, d, tn), lambda l, n: (l, 0, n)),
            pl.BlockSpec((None, 1, tn), lambda l, n: (l, 0, n)),
        ],
        out_specs=pl.BlockSpec((None, rows, tn), lambda l, n: (l, 0, n)),
        out_shape=jax.ShapeDtypeStruct((depth, rows, n6), F32),
        compiler_params=pltpu.CompilerParams(
            dimension_semantics=("arbitrary", "arbitrary"), vmem_limit_bytes=VMEM_LIMIT),
        name="adaln_mod",
    )(c_all, w_ada, b_ada.reshape(depth, 1, n6))


def _norm_steps(src_ref, dst_ref, g_ref, mod, sc, sh, *, rows, row_chunk):
    for r in _row_slices(rows, row_chunk):
        h = _rms_mod(src_ref[r, :], g_ref[...], mod(sc, slice(None), r), mod(sh, slice(None), r))
        dst_ref[r, :] = h.astype(BF16)
        yield


def _in_proj_steps(h_ref, win_ref, p_ref):
    for n0 in range(0, PROJ_PAD, MXU_N):
        n1 = min(n0 + MXU_N, PROJ_PAD)
        p_ref[:, n0:n1] = _bdot(h_ref[...], win_ref[:, n0:n1])
        yield


def _out_proj_steps(x_ref, mix_ref, mod, wout_ref, x1_ref):
    for n0 in range(0, D_MODEL, MXU_N):
        cs = slice(n0, n0 + MXU_N)
        mx = _bdot(mix_ref[...], wout_ref[:, cs])
        x1_ref[:, cs] = x_ref[:, cs] + mod("ga1", cs, slice(None)) * mx
        yield


def _ffn_gate_steps(h2_ref, w13_ref, act_ref):
    for n0 in range(0, D_FF, MXU_N):
        gg = _bdot(h2_ref[...], w13_ref[:, n0:n0 + MXU_N])
        gg = gg * _sigmoid(gg)
        yield
        uu = _bdot(h2_ref[...], w13_ref[:, D_FF + n0:D_FF + n0 + MXU_N])
        act_ref[:, n0:n0 + MXU_N] = (gg * uu).astype(BF16)
        yield


def _ffn_out_steps(x1_ref, act_ref, ga2, w2_ref, gfin_ref, y_ref, *, rows, row_chunk, final):
    dst = x1_ref if final else y_ref
    for n0 in range(0, D_MODEL, MXU_N):
        cs = slice(n0, n0 + MXU_N)
        ff = _bdot(act_ref[...], w2_ref[:, cs])
        dst[:, cs] = x1_ref[:, cs] + ga2(cs) * ff
        yield
    if final:
        for r in _row_slices(rows, row_chunk):
            xb = x1_ref[r, :]
            ms = jnp.mean(xb * xb, axis=-1, keepdims=True)
            y_ref[r, :] = xb * lax.rsqrt(ms + EPS) * gfin_ref[...]
            yield


def _run(*gens):
    live = [g if isinstance(g, tuple) else (g, 1) for g in gens]
    while live:
        for entry in list(live):
            g, k = entry
            for _ in range(k):
                try:
                    next(g)
                except StopIteration:
                    live.remove(entry)
                    break


def _layernorm_silu(z, lng, lnb):
    mu = jnp.mean(z, axis=-1, keepdims=True)
    zc = z - mu
    var = jnp.mean(zc * zc, axis=-1, keepdims=True)
    zn = zc * lax.rsqrt(var + EPS) * lng + lnb
    return zn * _sigmoid(zn)


_MOD_ROW = {"sh1": 0, "sc1": 1, "ga1": 2, "sh2": 3, "sc2": 4, "ga2": 5}


def _cumsum_chunk(tril_bf, x):
    hi = x.astype(BF16)
    r1 = x - hi.astype(F32)
    mid = r1.astype(BF16)
    lo = (r1 - mid.astype(F32)).astype(BF16)
    return _bdot(tril_bf, hi) + _bdot(tril_bf, mid) + _bdot(tril_bf, lo)


def _mlstm_chunk(p_ref, r0, bg_ref, ghead_ref, mix_ref, C_ref, n_ref, m_ref, tril, tril_bf):
    rs = slice(r0, r0 + CHUNK)
    gates = p_ref[rs, COL_G:COL_G + LANES] + bg_ref[...]
    fcum = _cumsum_chunk(tril_bf, _log_sigmoid(gates))
    gates_t = gates.T
    fcum_t = fcum.T
    yield
    heads = range(HEADS)
    col = lambda base, h: slice(base + h * HEAD_DIM, base + (h + 1) * HEAD_DIM)
    st = [dict() for _ in heads]
    for h in heads:
        d = st[h]
        d["q"] = p_ref[rs, col(COL_Q, h)]
        d["k"] = p_ref[rs, col(COL_K, h)] * (HEAD_DIM ** -0.5)
        d["v_bf"] = p_ref[rs, col(COL_V, h)].astype(BF16)
        d["q_bf"] = d["q"].astype(BF16)
        d["f_col"] = fcum[:, HEADS + h:HEADS + h + 1]
        i_col = gates[:, h:h + 1]
        f_row = fcum_t[HEADS + h:HEADS + h + 1, :]
        i_row = gates_t[h:h + 1, :]
        g = f_row[:, CHUNK - 1:CHUNK]
        d["g"] = g
        d["dlog"] = jnp.where(tril, d["f_col"] - f_row + i_row, -jnp.inf)
        d["dmax"] = jnp.max(d["dlog"], axis=-1, keepdims=True)
        d["wlog"] = g - d["f_col"] + i_col
        d["wmax"] = jnp.max(g - f_row + i_row, axis=-1, keepdims=True)
        d["qk"] = _bdot_nt(d["q_bf"], d["k"].astype(BF16))
        yield
    for h in heads:
        d = st[h]
        m_prev = m_ref[h:h + 1, 0:1]
        inter = d["f_col"] + m_prev
        m_t = jnp.maximum(inter, d["dmax"])
        d["a_inter"] = jnp.exp(inter - m_t)
        d["ebound"] = jnp.exp(-m_t)
        s = d["qk"] * jnp.exp(d["dlog"] - m_t)
        d["ssum"] = jnp.sum(s, axis=-1, keepdims=True)
        d["sv"] = _bdot(s.astype(BF16), d["v_bf"])
        m_new = jnp.maximum(d["g"] + m_prev, d["wmax"])
        d["decay"] = jnp.exp(d["g"] + m_prev - m_new)
        kw = d["k"] * jnp.exp(d["wlog"] - m_new)
        d["kwsum"] = jnp.sum(kw, axis=0, keepdims=True)
        d["kv"] = _bdot(kw.T.astype(BF16), d["v_bf"])
        m_ref[h:h + 1, :] = jnp.broadcast_to(m_new, (1, LANES))
        yield
    for h in heads:
        d = st[h]
        n_row = n_ref[h:h + 1, :]
        c_mat = C_ref[h]
        num = d["sv"] + _bdot(d["q_bf"], c_mat.astype(BF16)) * d["a_inter"]
        qn = jnp.sum(d["q"] * n_row, axis=-1, keepdims=True)
        den = d["ssum"] + d["a_inter"] * qn
        bound = jnp.maximum(jnp.abs(den), d["ebound"])
        hh = num * (1.0 / bound)
        C_ref[h] = d["decay"] * c_mat + d["kv"]
        n_ref[h:h + 1, :] = d["decay"] * n_row + d["kwsum"]
        hn = hh * lax.rsqrt(jnp.mean(hh * hh, axis=-1, keepdims=True) + EPS) * ghead_ref[:, col(0, h)]
        mix_ref[rs, col(0, h)] = (hn * _sigmoid(p_ref[rs, col(COL_O, h)])).astype(BF16)
        yield


def _prompt_layer_kernel(x_ref, mod1_ref, mod2_ref, gmix_ref, win_ref, bg_ref, ghead_ref, wdw_ref,
                         bdw_ref, lng_ref, lnb_ref, wout_ref, gffn_ref, w13_ref, w2_ref, gfin_ref,
                         y_ref, c_out, n_out, m_out, conv_out,
                         h_ref, p_ref, mix_ref, xres_ref, x1_ref, h2_ref, ga2_ref, act_ref, ubuf_ref,
                         zc_ref, C_ref, n_ref, m_ref, *, tm, tiles_per_seq, final):
    s = pl.program_id(0)
    n_tiles = pl.num_programs(0) - 2
    j = jnp.clip(s - 1, 0, n_tiles - 1) % tiles_per_seq
    last = tiles_per_seq - 1
    row_chunk = 32

    @pl.when(s == 0)
    def _():
        for ref in (p_ref, xres_ref, x1_ref, h2_ref, ga2_ref):
            ref[...] = jnp.zeros_like(ref)

    @pl.when(j == 0)
    def _():
        C_ref[...] = jnp.zeros_like(C_ref)
        n_ref[...] = jnp.zeros_like(n_ref)
        m_ref[...] = jnp.zeros_like(m_ref)
        ubuf_ref[:, 0:HIST_PAD, :] = jnp.zeros((CONV_SLABS, HIST_PAD, LANES), F32)

    @pl.when(j > 0)
    def _():
        ubuf_ref[:, 0:HIST_PAD, :] = ubuf_ref[:, tm:tm + HIST_PAD, :]

    def mod1(name, cs, r=None):
        i = _MOD_ROW[name]
        return mod1_ref[i:i + 1, cs]

    def mod2(name, cs, r=None):
        i = _MOD_ROW[name]
        return mod2_ref[i:i + 1, cs]

    row_id = lax.broadcasted_iota(jnp.int32, (CHUNK, CHUNK), 0)
    col_id = lax.broadcasted_iota(jnp.int32, (CHUNK, CHUNK), 1)
    tril = row_id >= col_id
    tril_bf = jnp.where(tril, 1.0, 0.0).astype(BF16)

    def mlstm_steps():
        for c in range(tm // CHUNK):
            yield from _mlstm_chunk(p_ref, c * CHUNK, bg_ref, ghead_ref, mix_ref, C_ref, n_ref,
                                    m_ref, tril, tril_bf)

    def glu_steps():
        for r in _row_slices(tm, row_chunk):
            ro = slice(r.start + HIST_PAD, r.stop + HIST_PAD)
            u = p_ref[r, COL_A:COL_A + CONV_CH] * _sigmoid(p_ref[r, COL_B:COL_B + CONV_CH])
            for g in range(CONV_SLABS):
                ubuf_ref[g, ro, :] = u[:, g * LANES:(g + 1) * LANES]
            yield

    def conv_steps():
        n_sets = row_chunk // SUBLANES
        firsts = [(t0 // CONV_STRIDE) * CONV_STRIDE * SUBLANES + t0 % CONV_STRIDE
                  for t0 in range(n_sets)]
        for g in range(CONV_SLABS):
            gs = slice(g * LANES, (g + 1) * LANES)
            bias = jnp.broadcast_to(bdw_ref[:, gs], (SUBLANES, LANES))
            for base in range(0, tm, row_chunk):
                acc = [bias for _ in range(n_sets)]
                for w in range(CONV_WIDTH):
                    tap = wdw_ref[w:w + 1, gs]
                    for t0 in range(n_sets):
                        src = pl.ds(base + firsts[t0] + HIST_PAD - HIST + w, SUBLANES,
                                    stride=CONV_STRIDE)
                        acc[t0] = acc[t0] + ubuf_ref[g, src, :] * tap
                for t0 in range(n_sets):
                    zc_ref[g, pl.ds(base + firsts[t0], SUBLANES, stride=CONV_STRIDE), :] = acc[t0]
                yield

    def conv_norm_steps():
        for r in _row_slices(tm, row_chunk):
            z = jnp.concatenate([zc_ref[g, r, :] for g in range(CONV_SLABS)], axis=-1)
            mix_ref[r, MLSTM_WIDTH:] = _layernorm_silu(z, lng_ref[...], lnb_ref[...]).astype(BF16)
            yield

    def keep_steps():
        xres_ref[...] = x_ref[...]
        ga2_ref[...] = mod2("ga2", slice(None))
        yield

    _run(_ffn_gate_steps(h2_ref, w13_ref, act_ref), mlstm_steps(),
         _norm_steps(x_ref, h_ref, gmix_ref, mod1, "sc1", "sh1", rows=tm, row_chunk=row_chunk),
         (itertools.chain(glu_steps(), conv_steps()), 2))

    @pl.when(jnp.logical_and(j == last, jnp.logical_and(s >= 1, s <= n_tiles)))
    def _():
        c_out[...] = C_ref[...]
        n_out[...] = n_ref[0:HEADS, :]
        m_out[...] = m_ref[0:HEADS, :]
        for g in range(CONV_SLABS):
            conv_out[:, g * LANES:(g + 1) * LANES] = ubuf_ref[g, tm + HIST_PAD - HIST:tm + HIST_PAD, :]

    @pl.when(s >= 0)
    def _():
        _run(_ffn_out_steps(x1_ref, act_ref, lambda cs: ga2_ref[:, cs], w2_ref, gfin_ref, y_ref,
                            rows=tm, row_chunk=row_chunk, final=final),
             (conv_norm_steps(), 2))

    @pl.when(s >= 0)
    def _():
        _run(_out_proj_steps(xres_ref, mix_ref, mod2, wout_ref, x1_ref))
        _run(_in_proj_steps(h_ref, win_ref, p_ref),
             _norm_steps(x1_ref, h2_ref, gffn_ref, mod2, "sc2", "sh2", rows=tm, row_chunk=row_chunk))
        _run(keep_steps())


def _resident(shape):
    nd = len(shape)
    return pl.BlockSpec(shape, lambda s: (0,) * nd, pipeline_mode=pl.Buffered(1))


def _prompt_layer(x, mod, wl, g_final, *, tm, final):
    bsz, seq, d = x.shape
    assert seq % tm == 0 and tm % CHUNK == 0 and tm >= HIST_PAD
    nt = seq // tm
    n_tiles = bsz * nt
    kern = functools.partial(_prompt_layer_kernel, tm=tm, tiles_per_seq=nt, final=final)

    def tile(s, lag):
        return jnp.clip(s - lag, 0, n_tiles - 1)

    def cur(s):
        return tile(s, 1)

    def prev(s):
        return tile(s, 2)

    in_specs = [
        pl.BlockSpec((None, tm, d), lambda s: (tile(s, 0) // nt, tile(s, 0) % nt, 0)),
        pl.BlockSpec((None, 6, d), lambda s: (tile(s, 0) // nt, 0, 0)),
        pl.BlockSpec((None, 6, d), lambda s: (cur(s) // nt, 0, 0)),
        _resident((1, d)),
        _resident((d, PROJ_PAD)),
        _resident((1, LANES)),
        _resident((1, MLSTM_WIDTH)),
        _resident((CONV_WIDTH, CONV_CH)),
        _resident((1, CONV_CH)),
        _resident((1, CONV_CH)),
        _resident((1, CONV_CH)),
        _resident((d, d)),
        _resident((1, d)),
        _resident((d, 2 * D_FF)),
        _resident((D_FF, d)),
        _resident((1, d)),
    ]
    out_specs = [
        pl.BlockSpec((None, tm, d), lambda s: (prev(s) // nt, prev(s) % nt, 0)),
        pl.BlockSpec((None, HEADS, HEAD_DIM, HEAD_DIM), lambda s: (cur(s) // nt, 0, 0, 0)),
        pl.BlockSpec((None, HEADS, HEAD_DIM), lambda s: (cur(s) // nt, 0, 0)),
        pl.BlockSpec((None, HEADS, LANES), lambda s: (cur(s) // nt, 0, 0)),
        pl.BlockSpec((None, HIST, CONV_CH), lambda s: (cur(s) // nt, 0, 0)),
    ]
    out_shape = [
        jax.ShapeDtypeStruct((bsz, seq, d), F32),
        jax.ShapeDtypeStruct((bsz, HEADS, HEAD_DIM, HEAD_DIM), F32),
        jax.ShapeDtypeStruct((bsz, HEADS, HEAD_DIM), F32),
        jax.ShapeDtypeStruct((bsz, HEADS, LANES), F32),
        jax.ShapeDtypeStruct((bsz, HIST, CONV_CH), F32),
    ]
    scratch = [
        pltpu.VMEM((tm, d), BF16),
        pltpu.VMEM((tm, PROJ_PAD), F32),
        pltpu.VMEM((tm, d), BF16),
        pltpu.VMEM((tm, d), F32),
        pltpu.VMEM((tm, d), F32),
        pltpu.VMEM((tm, d), BF16),
        pltpu.VMEM((1, d), F32),
        pltpu.VMEM((tm, D_FF), BF16),
        pltpu.VMEM((CONV_SLABS, tm + HIST_PAD, LANES), F32),
        pltpu.VMEM((CONV_SLABS, tm, LANES), F32),
        pltpu.VMEM((HEADS, HEAD_DIM, HEAD_DIM), F32),
        pltpu.VMEM((SUBLANES, HEAD_DIM), F32),
        pltpu.VMEM((SUBLANES, LANES), F32),
    ]
    return pl.pallas_call(
        kern,
        grid=(n_tiles + 2,),
        in_specs=in_specs,
        out_specs=out_specs,
        out_shape=out_shape,
        scratch_shapes=scratch,
        compiler_params=pltpu.CompilerParams(
            dimension_semantics=("arbitrary",), vmem_limit_bytes=VMEM_LIMIT),
        name="prompt_layer",
    )(x, mod, mod, wl["g_mix"], wl["w_in"], wl["b_gates"], wl["g_head"], wl["w_dw"], wl["b_dw"],
      wl["ln_g"], wl["ln_b"], wl["w_out"], wl["g_ffn"], wl["w_ffn13"], wl["w_ffn2"], g_final)


def _sample_in_kernel(x_ref, mod_ref, gmix_ref, win_ref, p_ref, h_ref):
    h = _rms_mod(x_ref[...], gmix_ref[...], mod_ref[:, D_MODEL:2 * D_MODEL], mod_ref[:, 0:D_MODEL])
    h_ref[...] = h.astype(BF16)
    _run(_in_proj_steps(h_ref, win_ref, p_ref))


def _sample_in_proj(x, mod, wl):
    rows = x.shape[0]
    return pl.pallas_call(
        _sample_in_kernel,
        out_shape=jax.ShapeDtypeStruct((rows, PROJ_PAD), F32),
        scratch_shapes=[pltpu.VMEM((rows, D_MODEL), BF16)],
        compiler_params=pltpu.CompilerParams(vmem_limit_bytes=VMEM_LIMIT),
        name="sample_in_proj",
    )(x, mod, wl["g_mix"], wl["w_in"])


def _sample_step_kernel(p_ref, c_ref, n_ref, m_ref, conv_ref, bg_ref, ghead_ref, wdw_ref, bdw_ref,
                        lng_ref, lnb_ref, *rest, bt, n_alias, fill_other):
    hm_ref, zc_ref, c_out, n_out, m_out, conv_out = rest[n_alias:]
    slot = pl.program_id(0)

    @pl.when(slot == 0)
    def _():
        scale = HEAD_DIM ** -0.5
        gates = p_ref[:, COL_G:COL_G + LANES] + bg_ref[...]
        logf = pltpu.roll(_log_sigmoid(gates), LANES - HEADS, axis=1)
        inter = logf + m_ref[...]
        m_t = jnp.maximum(inter, gates)
        dw_all = jnp.exp(gates - m_t)
        a_all = jnp.exp(inter - m_t)
        eb_all = jnp.exp(-m_t)
        m_out[...] = m_t
        for h in range(HEADS):
            hs = slice(h * HEAD_DIM, (h + 1) * HEAD_DIM)
            q = p_ref[:, COL_Q + h * HEAD_DIM:COL_Q + (h + 1) * HEAD_DIM]
            k = p_ref[:, COL_K + h * HEAD_DIM:COL_K + (h + 1) * HEAD_DIM] * scale
            v = p_ref[:, COL_V + h * HEAD_DIM:COL_V + (h + 1) * HEAD_DIM]
            o = p_ref[:, COL_O + h * HEAD_DIM:COL_O + (h + 1) * HEAD_DIM]
            dw = dw_all[:, h:h + 1]
            a_in = a_all[:, h:h + 1]
            n_h = n_ref[:, hs]
            kw = k * dw
            s = jnp.sum(q * k, axis=-1, keepdims=True) * dw
            qn = jnp.sum(q * n_h, axis=-1, keepdims=True)
            n_out[:, hs] = a_in * n_h + kw
            q_t = q.T
            kw_t = kw.T
            a_rows = jnp.broadcast_to(a_in, (bt, HEAD_DIM))
            qc_rows = []
            for b in range(bt):
                c_mat = c_ref[b, h]
                qc_rows.append(jnp.sum(q_t[:, b:b + 1] * c_mat, axis=0, keepdims=True))
                c_out[b, h] = a_rows[b:b + 1, :] * c_mat + kw_t[:, b:b + 1] * v[b:b + 1, :]
            qc = jnp.concatenate(qc_rows, axis=0)
            num = s * v + qc * a_in
            den = s + a_in * qn
            bound = jnp.maximum(jnp.abs(den), eb_all[:, h:h + 1])
            hh = num * (1.0 / bound)
            hn = hh * lax.rsqrt(jnp.mean(hh * hh, axis=-1, keepdims=True) + EPS) * ghead_ref[:, hs]
            hm_ref[:, hs] = hn * _sigmoid(o)

        u = p_ref[:, COL_A:COL_A + CONV_CH] * _sigmoid(p_ref[:, COL_B:COL_B + CONV_CH])
        z_rows = []
        for b in range(bt):
            hist = conv_ref[b]
            z_rows.append(jnp.sum(hist * wdw_ref[0:HIST, :], axis=0, keepdims=True))
            conv_out[b, 0:HIST - 1, :] = hist[1:HIST, :]
            conv_out[b, HIST - 1:HIST, :] = u[b:b + 1, :]
        z = jnp.concatenate(z_rows, axis=0) + u * wdw_ref[HIST:HIST + 1, :] + bdw_ref[...]
        zc_ref[...] = _layernorm_silu(z, lng_ref[...], lnb_ref[...])

    if fill_other:
        @pl.when(slot != 0)
        def _():
            c_out[...] = jnp.zeros_like(c_out)
            conv_out[...] = jnp.zeros_like(conv_out)


def _sample_step(p, layer, state_c, state_n2, state_m_pad, state_conv, stacked, wl, *, bt=8):
    rows = p.shape[0]
    depth = state_c.shape[0]
    assert rows % bt == 0
    nblk = rows // bt
    first = stacked is None
    n_slots = depth if first else 1

    def blk_i(slot, i):
        return jnp.where(slot == 0, i, nblk - 1)

    def lay(*s):
        return pl.BlockSpec((None, bt) + s, lambda slot, i: (layer, blk_i(slot, i)) + (0,) * len(s))

    def row(*s):
        return pl.BlockSpec((bt,) + s, lambda slot, i: (blk_i(slot, i),) + (0,) * len(s))

    def stk(*s):
        return pl.BlockSpec((None, bt) + s,
                            lambda slot, i: ((slot if first else layer), i) + (0,) * len(s))

    const = lambda *s: pl.BlockSpec(s, lambda slot, i: (0,) * len(s))
    in_specs = [
        row(PROJ_PAD), lay(HEADS, HEAD_DIM, HEAD_DIM), lay(MLSTM_WIDTH), lay(LANES),
        lay(HIST, CONV_CH),
        const(1, LANES), const(1, MLSTM_WIDTH), const(CONV_WIDTH, CONV_CH),
        const(1, CONV_CH), const(1, CONV_CH), const(1, CONV_CH),
    ]
    args = [p, state_c, state_n2, state_m_pad, state_conv, wl["b_gates"], wl["g_head"], wl["w_dw"],
            wl["b_dw"], wl["ln_g"], wl["ln_b"]]
    aliases = {}
    if not first:
        aliases = {len(args): 2, len(args) + 1: 5}
        in_specs += [pl.BlockSpec(memory_space=pl.ANY), pl.BlockSpec(memory_space=pl.ANY)]
        args += list(stacked)
    out_shape = [
        jax.ShapeDtypeStruct((rows, MLSTM_WIDTH), F32),
        jax.ShapeDtypeStruct((rows, CONV_CH), F32),
        jax.ShapeDtypeStruct(state_c.shape, F32),
        jax.ShapeDtypeStruct((rows, MLSTM_WIDTH), F32),
        jax.ShapeDtypeStruct((rows, LANES), F32),
        jax.ShapeDtypeStruct(state_conv.shape, F32),
    ]
    hm, zc, c_new, n_new, m_new, conv_new = pl.pallas_call(
        functools.partial(_sample_step_kernel, bt=bt, n_alias=len(aliases),
                          fill_other=first and depth > 1),
        grid=(n_slots, nblk),
        in_specs=in_specs,
        out_specs=[row(MLSTM_WIDTH), row(CONV_CH), stk(HEADS, HEAD_DIM, HEAD_DIM),
                   row(MLSTM_WIDTH), row(LANES), stk(HIST, CONV_CH)],
        out_shape=out_shape,
        input_output_aliases=aliases,
        compiler_params=pltpu.CompilerParams(
            dimension_semantics=("arbitrary", "arbitrary"), vmem_limit_bytes=VMEM_LIMIT),
        name="sample_step",
    )(*args)
    return hm, zc, (c_new, conv_new), n_new.reshape(rows, HEADS, HEAD_DIM), m_new[:, :HEADS]


def _sample_out_kernel(x_ref, hm_ref, zc_ref, mod_ref, wout_ref, gffn_ref, w13_ref, w2_ref,
                       gfin_ref, y_ref, mix_ref, x1_ref, h2_ref, act_ref, *, rows, final):
    mix_ref[:, 0:MLSTM_WIDTH] = hm_ref[...].astype(BF16)
    mix_ref[:, MLSTM_WIDTH:] = zc_ref[...].astype(BF16)

    def mod_rows(name, cs, r=slice(None)):
        i = _MOD_ROW[name]
        if isinstance(cs, slice) and cs == slice(None):
            cs = slice(0, D_MODEL)
        return mod_ref[r, i * D_MODEL + cs.start:i * D_MODEL + cs.stop]

    _run(itertools.chain(
        _out_proj_steps(x_ref, mix_ref, mod_rows, wout_ref, x1_ref),
        _norm_steps(x1_ref, h2_ref, gffn_ref, mod_rows, "sc2", "sh2", rows=rows, row_chunk=32),
        _ffn_gate_steps(h2_ref, w13_ref, act_ref),
        _ffn_out_steps(x1_ref, act_ref, lambda cs: mod_rows("ga2", cs), w2_ref, gfin_ref, y_ref,
                       rows=rows, row_chunk=32, final=final)))


def _sample_out(x, hm, zc, mod, wl, g_final, *, final):
    rows = x.shape[0]
    return pl.pallas_call(
        functools.partial(_sample_out_kernel, rows=rows, final=final),
        out_shape=jax.ShapeDtypeStruct((rows, D_MODEL), F32),
        scratch_shapes=[
            pltpu.VMEM((rows, D_MODEL), BF16),
            pltpu.VMEM((rows, D_MODEL), F32),
            pltpu.VMEM((rows, D_MODEL), BF16),
            pltpu.VMEM((rows, D_FF), BF16),
        ],
        compiler_params=pltpu.CompilerParams(vmem_limit_bytes=VMEM_LIMIT),
        name="sample_out",
    )(x, hm, zc, mod, wl["w_out"], wl["g_ffn"], wl["w_ffn13"], wl["w_ffn2"], g_final)


def _prep_layer_weights(l, g_mix, w_in, b_gates, g_head, w_dw, b_dw, ln_g, ln_b, w_out, g_ffn,
                        w_ffn13, w_ffn2):
    d = w_in.shape[1]
    n_gate = 2 * HEADS
    gate0 = 4 * MLSTM_WIDTH
    w = w_in[l]
    w_r = jnp.concatenate(
        [w[:, :gate0], w[:, gate0 + n_gate:], w[:, gate0:gate0 + n_gate],
         jnp.zeros((d, LANES - n_gate), w.dtype)], axis=1).astype(BF16)
    return {
        "g_mix": g_mix[l].reshape(1, d),
        "w_in": w_r,
        "b_gates": jnp.pad(b_gates[l], (0, LANES - n_gate)).reshape(1, LANES),
        "g_head": g_head[l].reshape(1, MLSTM_WIDTH),
        "w_dw": w_dw[l],
        "b_dw": b_dw[l].reshape(1, CONV_CH),
        "ln_g": ln_g[l].reshape(1, CONV_CH),
        "ln_b": ln_b[l].reshape(1, CONV_CH),
        "w_out": w_out[l].astype(BF16),
        "g_ffn": g_ffn[l].reshape(1, d),
        "w_ffn13": w_ffn13[l].astype(BF16),
        "w_ffn2": w_ffn2[l].astype(BF16),
    }


def _forward(x_prompt, x_sample, c_prompt, c_sample, state_C, state_n, state_m, state_conv,
             w_ada, b_ada, g_mix, w_in, b_gates, g_head, w_dw, b_dw, ln_g, ln_b, w_out,
             g_ffn, w_ffn13, w_ffn2, g_final, *, tm):
    depth = w_ada.shape[0]
    bp = x_prompt.shape[0]
    d = x_prompt.shape[-1]
    mod = _modulation(jnp.concatenate([c_prompt, c_sample], axis=0), w_ada, b_ada)
    gfin = g_final.reshape(1, d)
    xp = x_prompt
    xs = x_sample.reshape(x_sample.shape[0], d)
    outs_p = [[], [], [], []]
    ns_l, ms_l = [], []
    stacked = None
    n_seq = xs.shape[0]
    state_n2 = state_n.reshape(depth, n_seq, MLSTM_WIDTH)
    state_m_pad = jnp.pad(state_m, ((0, 0), (0, 0), (0, LANES - HEADS)))
    for l in range(depth):
        wl = _prep_layer_weights(l, g_mix, w_in, b_gates, g_head, w_dw, b_dw, ln_g, ln_b, w_out,
                                 g_ffn, w_ffn13, w_ffn2)
        final = l == depth - 1
        mod_p = mod[l, :bp].reshape(bp, 6, d)
        mod_s = mod[l, bp:]
        xp, c1, n1, m1, b1 = _prompt_layer(xp, mod_p, wl, gfin, tm=tm, final=final)
        for acc, val in zip(outs_p, (c1, n1, m1[:, :, 0], b1)):
            acc.append(val)
        p = _sample_in_proj(xs, mod_s, wl)
        hm, zc, stacked, n2, m2 = _sample_step(p, l, state_C, state_n2, state_m_pad, state_conv,
                                               stacked, wl)
        xs = _sample_out(xs, hm, zc, mod_s, wl, gfin, final=final)
        ns_l.append(n2)
        ms_l.append(m2)
    y_sample = xs.reshape(x_sample.shape)
    c_sample_new, conv_sample_new = stacked
    return ((xp, y_sample) + tuple(jnp.stack(a) for a in outs_p)
            + (c_sample_new, jnp.stack(ns_l), jnp.stack(ms_l), conv_sample_new))


def kernel(x_prompt, x_sample, c_prompt, c_sample, state_C, state_n, state_m, state_conv, w_ada, b_ada, g_mix, w_in, b_gates, g_head, w_dw, b_dw, ln_g, ln_b, w_out, g_ffn, w_ffn13, w_ffn2, g_final):
    return _forward(x_prompt, x_sample, c_prompt, c_sample, state_C, state_n, state_m, state_conv,
                    w_ada, b_ada, g_mix, w_in, b_gates, g_head, w_dw, b_dw, ln_g, ln_b, w_out,
                    g_ffn, w_ffn13, w_ffn2, g_final, tm=256)
```

```python
import functools
import itertools

import jax
import jax.numpy as jnp
from jax import lax
from jax.experimental import pallas as pl
from jax.experimental.pallas import tpu as pltpu

F32 = jnp.float32
BF16 = jnp.bfloat16

D_MODEL = 1024
HEADS = 4
HEAD_DIM = 128
MLSTM_WIDTH = HEADS * HEAD_DIM
CONV_CH = D_MODEL - MLSTM_WIDTH
CONV_WIDTH = 31
HIST = CONV_WIDTH - 1
D_FF = 2816
CHUNK = 128
EPS = 1e-6

LANES = 128
SUBLANES = 8
MXU_N = 256

COL_Q = 0
COL_K = MLSTM_WIDTH
COL_V = 2 * MLSTM_WIDTH
COL_O = 3 * MLSTM_WIDTH
COL_A = 4 * MLSTM_WIDTH
COL_B = COL_A + CONV_CH
COL_G = COL_B + CONV_CH
PROJ_PAD = COL_G + LANES
HIST_PAD = 32
CONV_SLABS = CONV_CH // LANES
CONV_STRIDE = 2
VMEM_LIMIT = 60 * 1024 * 1024


def _log_sigmoid(x):
    return jnp.minimum(x, 0.0) - jnp.log1p(jnp.exp(-jnp.abs(x)))


def _sigmoid(x):
    return 1.0 / (1.0 + jnp.exp(-x))


def _bdot(a, b):
    return jnp.dot(a, b, preferred_element_type=F32)


def _bdot_nt(a, b):
    return lax.dot_general(a, b, (((1,), (1,)), ((), ())), preferred_element_type=F32)


def _row_slices(rows, chunk):
    chunk = min(chunk, rows)
    assert rows % chunk == 0
    return [slice(r0, r0 + chunk) for r0 in range(0, rows, chunk)]


def _rms_mod(x, g, sc, sh):
    ms = jnp.mean(x * x, axis=-1, keepdims=True)
    return (x * lax.rsqrt(ms + EPS) * g) * (1.0 + sc) + sh


def _mod_kernel(c_ref, w_ref, b_ref, o_ref):
    o_ref[...] = _bdot(c_ref[...].astype(BF16), w_ref[...].astype(BF16)) + b_ref[...]


def _modulation(c_all, w_ada, b_ada):
    depth, d, n6 = w_ada.shape
    rows = c_all.shape[0]
    tn = 1024
    return pl.pallas_call(
        _mod_kernel,
        grid=(depth, n6 // tn),
        in_specs=[
            pl.BlockSpec((rows, d), lambda l, n: (0, 0)),
            pl.BlockSpec((None, d, tn), lambda l, n: (l, 0, n)),
            pl.BlockSpec((None, 1, tn), lambda l, n: (l, 0, n)),
        ],
        out_specs=pl.BlockSpec((None, rows, tn), lambda l, n: (l, 0, n)),
        out_shape=jax.ShapeDtypeStruct((depth, rows, n6), F32),
        compiler_params=pltpu.CompilerParams(
            dimension_semantics=("arbitrary", "arbitrary"), vmem_limit_bytes=VMEM_LIMIT),
        name="adaln_mod",
    )(c_all, w_ada, b_ada.reshape(depth, 1, n6))


def _norm_steps(src_ref, dst_ref, g_ref, mod, sc, sh, *, rows, row_chunk):
    for r in _row_slices(rows, row_chunk):
        h = _rms_mod(src_ref[r, :], g_ref[...], mod(sc, slice(None), r), mod(sh, slice(None), r))
        dst_ref[r, :] = h.astype(BF16)
        yield


def _in_proj_steps(h_ref, win_ref, p_ref):
    for n0 in range(0, PROJ_PAD, MXU_N):
        n1 = min(n0 + MXU_N, PROJ_PAD)
        p_ref[:, n0:n1] = _bdot(h_ref[...], win_ref[:, n0:n1])
        yield


def _out_proj_steps(x_ref, mix_ref, mod, wout_ref, x1_ref):
    for n0 in range(0, D_MODEL, MXU_N):
        cs = slice(n0, n0 + MXU_N)
        mx = _bdot(mix_ref[...], wout_ref[:, cs])
        x1_ref[:, cs] = x_ref[:, cs] + mod("ga1", cs, slice(None)) * mx
        yield


def _ffn_gate_steps(h2_ref, w13_ref, act_ref):
    for n0 in range(0, D_FF, MXU_N):
        gg = _bdot(h2_ref[...], w13_ref[:, n0:n0 + MXU_N])
        gg = gg * _sigmoid(gg)
        yield
        uu = _bdot(h2_ref[...], w13_ref[:, D_FF + n0:D_FF + n0 + MXU_N])
        act_ref[:, n0:n0 + MXU_N] = (gg * uu).astype(BF16)
        yield


def _ffn_out_steps(x1_ref, act_ref, ga2, w2_ref, gfin_ref, y_ref, *, rows, row_chunk, final):
    dst = x1_ref if final else y_ref
    for n0 in range(0, D_MODEL, MXU_N):
        cs = slice(n0, n0 + MXU_N)
        ff = _bdot(act_ref[...], w2_ref[:, cs])
        dst[:, cs] = x1_ref[:, cs] + ga2(cs) * ff
        yield
    if final:
        for r in _row_slices(rows, row_chunk):
            xb = x1_ref[r, :]
            ms = jnp.mean(xb * xb, axis=-1, keepdims=True)
            y_ref[r, :] = xb * lax.rsqrt(ms + EPS) * gfin_ref[...]
            yield


def _run(*gens):
    live = [g if isinstance(g, tuple) else (g, 1) for g in gens]
    while live:
        for entry in list(live):
            g, k = entry
            for _ in range(k):
                try:
                    next(g)
                except StopIteration:
                    live.remove(entry)
                    break


def _layernorm_silu(z, lng, lnb):
    mu = jnp.mean(z, axis=-1, keepdims=True)
    zc = z - mu
    var = jnp.mean(zc * zc, axis=-1, keepdims=True)
    zn = zc * lax.rsqrt(var + EPS) * lng + lnb
    return zn * _sigmoid(zn)


_MOD_ROW = {"sh1": 0, "sc1": 1, "ga1": 2, "sh2": 3, "sc2": 4, "ga2": 5}


def _cumsum_chunk(tril_bf, x):
    hi = x.astype(BF16)
    r1 = x - hi.astype(F32)
    mid = r1.astype(BF16)
    lo = (r1 - mid.astype(F32)).astype(BF16)
    return _bdot(tril_bf, hi) + _bdot(tril_bf, mid) + _bdot(tril_bf, lo)


def _mlstm_chunk(p_ref, r0, bg_ref, ghead_ref, mix_ref, C_ref, n_ref, m_ref, tril, tril_bf):
    rs = slice(r0, r0 + CHUNK)
    gates = p_ref[rs, COL_G:COL_G + LANES] + bg_ref[...]
    fcum = _cumsum_chunk(tril_bf, _log_sigmoid(gates))
    gates_t = gates.T
    fcum_t = fcum.T
    yield
    heads = range(HEADS)
    col = lambda base, h: slice(base + h * HEAD_DIM, base + (h + 1) * HEAD_DIM)
    st = [dict() for _ in heads]
    for h in heads:
        d = st[h]
        d["q"] = p_ref[rs, col(COL_Q, h)]
        d["k"] = p_ref[rs, col(COL_K, h)] * (HEAD_DIM ** -0.5)
        d["v_bf"] = p_ref[rs, col(COL_V, h)].astype(BF16)
        d["q_bf"] = d["q"].astype(BF16)
        d["f_col"] = fcum[:, HEADS + h:HEADS + h + 1]
        i_col = gates[:, h:h + 1]
        f_row = fcum_t[HEADS + h:HEADS + h + 1, :]
        i_row = gates_t[h:h + 1, :]
        g = f_row[:, CHUNK - 1:CHUNK]
        d["g"] = g
        d["dlog"] = jnp.where(tril, d["f_col"] - f_row + i_row, -jnp.inf)
        d["dmax"] = jnp.max(d["dlog"], axis=-1, keepdims=True)
        d["wlog"] = g - d["f_col"] + i_col
        d["wmax"] = jnp.max(g - f_row + i_row, axis=-1, keepdims=True)
        d["qk"] = _bdot_nt(d["q_bf"], d["k"].astype(BF16))
        yield
    for h in heads:
        d = st[h]
        m_prev = m_ref[h:h + 1, 0:1]
        inter = d["f_col"] + m_prev
        m_t = jnp.maximum(inter, d["dmax"])
        d["a_inter"] = jnp.exp(inter - m_t)
        d["ebound"] = jnp.exp(-m_t)
        s = d["qk"] * jnp.exp(d["dlog"] - m_t)
        d["ssum"] = jnp.sum(s, axis=-1, keepdims=True)
        d["sv"] = _bdot(s.astype(BF16), d["v_bf"])
        m_new = jnp.maximum(d["g"] + m_prev, d["wmax"])
        d["decay"] = jnp.exp(d["g"] + m_prev - m_new)
        kw = d["k"] * jnp.exp(d["wlog"] - m_new)
        d["kwsum"] = jnp.sum(kw, axis=0, keepdims=True)
        d["kv"] = _bdot(kw.T.astype(BF16), d["v_bf"])
        m_ref[h:h + 1, :] = jnp.broadcast_to(m_new, (1, LANES))
        yield
    for h in heads:
        d = st[h]
        n_row = n_ref[h:h + 1, :]
        c_mat = C_ref[h]
        num = d["sv"] + _bdot(d["q_bf"], c_mat.astype(BF16)) * d["a_inter"]
        qn = jnp.sum(d["q"] * n_row, axis=-1, keepdims=True)
        den = d["ssum"] + d["a_inter"] * qn
        bound = jnp.maximum(jnp.abs(den), d["ebound"])
        hh = num * (1.0 / bound)
        C_ref[h] = d["decay"] * c_mat + d["kv"]
        n_ref[h:h + 1, :] = d["decay"] * n_row + d["kwsum"]
        hn = hh * lax.rsqrt(jnp.mean(hh * hh, axis=-1, keepdims=True) + EPS) * ghead_ref[:, col(0, h)]
        mix_ref[rs, col(0, h)] = (hn * _sigmoid(p_ref[rs, col(COL_O, h)])).astype(BF16)
        yield


def _prompt_layer_kernel(x_ref, mod1_ref, mod2_ref, gmix_ref, win_ref, bg_ref, ghead_ref, wdw_ref,
                         bdw_ref, lng_ref, lnb_ref, wout_ref, gffn_ref, w13_ref, w2_ref, gfin_ref,
                         y_ref, c_out, n_out, m_out, conv_out,
                         h_ref, p_ref, mix_ref, xres_ref, x1_ref, h2_ref, ga2_ref, act_ref, ubuf_ref,
                         zc_ref, C_ref, n_ref, m_ref, *, tm, tiles_per_seq, final):
    s = pl.program_id(0)
    n_tiles = pl.num_programs(0) - 2
    j = jnp.clip(s - 1, 0, n_tiles - 1) % tiles_per_seq
    last = tiles_per_seq - 1
    row_chunk = 32

    @pl.when(s == 0)
    def _():
        for ref in (p_ref, xres_ref, x1_ref, h2_ref, ga2_ref):
            ref[...] = jnp.zeros_like(ref)

    @pl.when(j == 0)
    def _():
        C_ref[...] = jnp.zeros_like(C_ref)
        n_ref[...] = jnp.zeros_like(n_ref)
        m_ref[...] = jnp.zeros_like(m_ref)
        ubuf_ref[:, 0:HIST_PAD, :] = jnp.zeros((CONV_SLABS, HIST_PAD, LANES), F32)

    @pl.when(j > 0)
    def _():
        ubuf_ref[:, 0:HIST_PAD, :] = ubuf_ref[:, tm:tm + HIST_PAD, :]

    def mod1(name, cs, r=None):
        i = _MOD_ROW[name]
        return mod1_ref[i:i + 1, cs]

    def mod2(name, cs, r=None):
        i = _MOD_ROW[name]
        return mod2_ref[i:i + 1, cs]

    row_id = lax.broadcasted_iota(jnp.int32, (CHUNK, CHUNK), 0)
    col_id = lax.broadcasted_iota(jnp.int32, (CHUNK, CHUNK), 1)
    tril = row_id >= col_id
    tril_bf = jnp.where(tril, 1.0, 0.0).astype(BF16)

    def mlstm_steps():
        for c in range(tm // CHUNK):
            yield from _mlstm_chunk(p_ref, c * CHUNK, bg_ref, ghead_ref, mix_ref, C_ref, n_ref,
                                    m_ref, tril, tril_bf)

    def glu_steps():
        for r in _row_slices(tm, row_chunk):
            ro = slice(r.start + HIST_PAD, r.stop + HIST_PAD)
            u = p_ref[r, COL_A:COL_A + CONV_CH] * _sigmoid(p_ref[r, COL_B:COL_B + CONV_CH])
            for g in range(CONV_SLABS):
                ubuf_ref[g, ro, :] = u[:, g * LANES:(g + 1) * LANES]
            yield

    def conv_steps():
        n_sets = row_chunk // SUBLANES
        firsts = [(t0 // CONV_STRIDE) * CONV_STRIDE * SUBLANES + t0 % CONV_STRIDE
                  for t0 in range(n_sets)]
        for g in range(CONV_SLABS):
            gs = slice(g * LANES, (g + 1) * LANES)
            bias = jnp.broadcast_to(bdw_ref[:, gs], (SUBLANES, LANES))
            for base in range(0, tm, row_chunk):
                acc = [bias for _ in range(n_sets)]
                for w in range(CONV_WIDTH):
                    tap = wdw_ref[w:w + 1, gs]
                    for t0 in range(n_sets):
                        src = pl.ds(base + firsts[t0] + HIST_PAD - HIST + w, SUBLANES,
                                    stride=CONV_STRIDE)
                        acc[t0] = acc[t0] + ubuf_ref[g, src, :] * tap
                for t0 in range(n_sets):
                    zc_ref[g, pl.ds(base + firsts[t0], SUBLANES, stride=CONV_STRIDE), :] = acc[t0]
                yield

    def conv_norm_steps():
        for r in _row_slices(tm, row_chunk):
            z = jnp.concatenate([zc_ref[g, r, :] for g in range(CONV_SLABS)], axis=-1)
            mix_ref[r, MLSTM_WIDTH:] = _layernorm_silu(z, lng_ref[...], lnb_ref[...]).astype(BF16)
            yield

    def keep_steps():
        xres_ref[...] = x_ref[...]
        ga2_ref[...] = mod2("ga2", slice(None))
        yield

    _run(_ffn_gate_steps(h2_ref, w13_ref, act_ref), mlstm_steps(),
         _norm_steps(x_ref, h_ref, gmix_ref, mod1, "sc1", "sh1", rows=tm, row_chunk=row_chunk),
         (itertools.chain(glu_steps(), conv_steps()), 2))

    @pl.when(jnp.logical_and(j == last, jnp.logical_and(s >= 1, s <= n_tiles)))
    def _():
        c_out[...] = C_ref[...]
        n_out[...] = n_ref[0:HEADS, :]
        m_out[...] = m_ref[0:HEADS, :]
        for g in range(CONV_SLABS):
            conv_out[:, g * LANES:(g + 1) * LANES] = ubuf_ref[g, tm + HIST_PAD - HIST:tm + HIST_PAD, :]

    @pl.when(s >= 0)
    def _():
        _run(_ffn_out_steps(x1_ref, act_ref, lambda cs: ga2_ref[:, cs], w2_ref, gfin_ref, y_ref,
                            rows=tm, row_chunk=row_chunk, final=final),
             (conv_norm_steps(), 2))

    @pl.when(s >= 0)
    def _():
        _run(_out_proj_steps(xres_ref, mix_ref, mod2, wout_ref, x1_ref))
        _run(_in_proj_steps(h_ref, win_ref, p_ref),
             _norm_steps(x1_ref, h2_ref, gffn_ref, mod2, "sc2", "sh2", rows=tm, row_chunk=row_chunk))
        _run(keep_steps())


def _resident(shape):
    nd = len(shape)
    return pl.BlockSpec(shape, lambda *g: (0,) * nd, pipeline_mode=pl.Buffered(1))


def _layer_resident(layer, shape):
    nd = len(shape)
    return pl.BlockSpec((None,) + shape, lambda *g: (layer,) + (0,) * nd,
                        pipeline_mode=pl.Buffered(1))


def _prompt_layer(x, mod, layer, wl, g_final, *, tm, final):
    bsz, seq, d = x.shape
    lw = functools.partial(_layer_resident, layer)
    assert seq % tm == 0 and tm % CHUNK == 0 and tm >= HIST_PAD
    nt = seq // tm
    n_tiles = bsz * nt
    kern = functools.partial(_prompt_layer_kernel, tm=tm, tiles_per_seq=nt, final=final)

    def tile(s, lag):
        return jnp.clip(s - lag, 0, n_tiles - 1)

    def cur(s):
        return tile(s, 1)

    def prev(s):
        return tile(s, 2)

    in_specs = [
        pl.BlockSpec((None, tm, d), lambda s: (tile(s, 0) // nt, tile(s, 0) % nt, 0)),
        pl.BlockSpec((None, 6, d), lambda s: (tile(s, 0) // nt, 0, 0)),
        pl.BlockSpec((None, 6, d), lambda s: (cur(s) // nt, 0, 0)),
        lw((1, d)),
        lw((d, PROJ_PAD)),
        lw((1, LANES)),
        lw((1, MLSTM_WIDTH)),
        lw((CONV_WIDTH, CONV_CH)),
        lw((1, CONV_CH)),
        lw((1, CONV_CH)),
        lw((1, CONV_CH)),
        lw((d, d)),
        lw((1, d)),
        lw((d, 2 * D_FF)),
        lw((D_FF, d)),
        _resident((1, d)),
    ]
    out_specs = [
        pl.BlockSpec((None, tm, d), lambda s: (prev(s) // nt, prev(s) % nt, 0)),
        pl.BlockSpec((None, HEADS, HEAD_DIM, HEAD_DIM), lambda s: (cur(s) // nt, 0, 0, 0)),
        pl.BlockSpec((None, HEADS, HEAD_DIM), lambda s: (cur(s) // nt, 0, 0)),
        pl.BlockSpec((None, HEADS, LANES), lambda s: (cur(s) // nt, 0, 0)),
        pl.BlockSpec((None, HIST, CONV_CH), lambda s: (cur(s) // nt, 0, 0)),
    ]
    out_shape = [
        jax.ShapeDtypeStruct((bsz, seq, d), F32),
        jax.ShapeDtypeStruct((bsz, HEADS, HEAD_DIM, HEAD_DIM), F32),
        jax.ShapeDtypeStruct((bsz, HEADS, HEAD_DIM), F32),
        jax.ShapeDtypeStruct((bsz, HEADS, LANES), F32),
        jax.ShapeDtypeStruct((bsz, HIST, CONV_CH), F32),
    ]
    scratch = [
        pltpu.VMEM((tm, d), BF16),
        pltpu.VMEM((tm, PROJ_PAD), F32),
        pltpu.VMEM((tm, d), BF16),
        pltpu.VMEM((tm, d), F32),
        pltpu.VMEM((tm, d), F32),
        pltpu.VMEM((tm, d), BF16),
        pltpu.VMEM((1, d), F32),
        pltpu.VMEM((tm, D_FF), BF16),
        pltpu.VMEM((CONV_SLABS, tm + HIST_PAD, LANES), F32),
        pltpu.VMEM((CONV_SLABS, tm, LANES), F32),
        pltpu.VMEM((HEADS, HEAD_DIM, HEAD_DIM), F32),
        pltpu.VMEM((SUBLANES, HEAD_DIM), F32),
        pltpu.VMEM((SUBLANES, LANES), F32),
    ]
    return pl.pallas_call(
        kern,
        grid=(n_tiles + 2,),
        in_specs=in_specs,
        out_specs=out_specs,
        out_shape=out_shape,
        scratch_shapes=scratch,
        compiler_params=pltpu.CompilerParams(
            dimension_semantics=("arbitrary",), vmem_limit_bytes=VMEM_LIMIT),
        name="prompt_layer",
    )(x, mod, mod, wl["g_mix"], wl["w_in"], wl["b_gates"], wl["g_head"], wl["w_dw"], wl["b_dw"],
      wl["ln_g"], wl["ln_b"], wl["w_out"], wl["g_ffn"], wl["w_ffn13"], wl["w_ffn2"], g_final)


def _sample_in_kernel(x_ref, mod_ref, gmix_ref, win_ref, p_ref, h_ref):
    h = _rms_mod(x_ref[...], gmix_ref[...], mod_ref[:, D_MODEL:2 * D_MODEL], mod_ref[:, 0:D_MODEL])
    h_ref[...] = h.astype(BF16)
    _run(_in_proj_steps(h_ref, win_ref, p_ref))


def _sample_in_proj(x, mod, layer, wl):
    rows, d = x.shape
    lw = functools.partial(_layer_resident, layer)
    return pl.pallas_call(
        _sample_in_kernel,
        grid=(1,),
        in_specs=[_resident((rows, d)), _resident(mod.shape), lw((1, d)), lw((d, PROJ_PAD))],
        out_specs=pl.BlockSpec((rows, PROJ_PAD), lambda i: (0, 0)),
        out_shape=jax.ShapeDtypeStruct((rows, PROJ_PAD), F32),
        scratch_shapes=[pltpu.VMEM((rows, D_MODEL), BF16)],
        compiler_params=pltpu.CompilerParams(
            dimension_semantics=("arbitrary",), vmem_limit_bytes=VMEM_LIMIT),
        name="sample_in_proj",
    )(x, mod, wl["g_mix"], wl["w_in"])


def _sample_step_kernel(p_ref, c_ref, n_ref, m_ref, conv_ref, bg_ref, ghead_ref, wdw_ref, bdw_ref,
                        lng_ref, lnb_ref, *rest, bt, n_alias, fill_other):
    hm_ref, zc_ref, c_out, n_out, m_out, conv_out = rest[n_alias:]
    slot = pl.program_id(0)

    @pl.when(slot == 0)
    def _():
        scale = HEAD_DIM ** -0.5
        gates = p_ref[:, COL_G:COL_G + LANES] + bg_ref[...]
        logf = pltpu.roll(_log_sigmoid(gates), LANES - HEADS, axis=1)
        inter = logf + m_ref[...]
        m_t = jnp.maximum(inter, gates)
        dw_all = jnp.exp(gates - m_t)
        a_all = jnp.exp(inter - m_t)
        eb_all = jnp.exp(-m_t)
        m_out[...] = m_t
        for h in range(HEADS):
            hs = slice(h * HEAD_DIM, (h + 1) * HEAD_DIM)
            q = p_ref[:, COL_Q + h * HEAD_DIM:COL_Q + (h + 1) * HEAD_DIM]
            k = p_ref[:, COL_K + h * HEAD_DIM:COL_K + (h + 1) * HEAD_DIM] * scale
            v = p_ref[:, COL_V + h * HEAD_DIM:COL_V + (h + 1) * HEAD_DIM]
            o = p_ref[:, COL_O + h * HEAD_DIM:COL_O + (h + 1) * HEAD_DIM]
            dw = dw_all[:, h:h + 1]
            a_in = a_all[:, h:h + 1]
            n_h = n_ref[:, hs]
            kw = k * dw
            s = jnp.sum(q * k, axis=-1, keepdims=True) * dw
            qn = jnp.sum(q * n_h, axis=-1, keepdims=True)
            n_out[:, hs] = a_in * n_h + kw
            q_t = q.T
            kw_t = kw.T
            a_rows = jnp.broadcast_to(a_in, (bt, HEAD_DIM))
            qc_rows = []
            for b in range(bt):
                c_mat = c_ref[b, h]
                qc_rows.append(jnp.sum(q_t[:, b:b + 1] * c_mat, axis=0, keepdims=True))
                c_out[b, h] = a_rows[b:b + 1, :] * c_mat + kw_t[:, b:b + 1] * v[b:b + 1, :]
            qc = jnp.concatenate(qc_rows, axis=0)
            num = s * v + qc * a_in
            den = s + a_in * qn
            bound = jnp.maximum(jnp.abs(den), eb_all[:, h:h + 1])
            hh = num * (1.0 / bound)
            hn = hh * lax.rsqrt(jnp.mean(hh * hh, axis=-1, keepdims=True) + EPS) * ghead_ref[:, hs]
            hm_ref[:, hs] = hn * _sigmoid(o)

        u = p_ref[:, COL_A:COL_A + CONV_CH] * _sigmoid(p_ref[:, COL_B:COL_B + CONV_CH])
        z = u * wdw_ref[HIST:HIST + 1, :] + bdw_ref[...]
        for w in range(HIST):
            z = z + conv_ref[w] * wdw_ref[w:w + 1, :]
        zc_ref[...] = _layernorm_silu(z, lng_ref[...], lnb_ref[...])
        conv_out[0:HIST - 1] = conv_ref[1:HIST]
        conv_out[HIST - 1] = u

    if fill_other:
        @pl.when(slot != 0)
        def _():
            c_out[...] = jnp.zeros_like(c_out)
            conv_out[...] = jnp.zeros_like(conv_out)


def _sample_step(p, layer, state_c, state_n2, state_m_pad, state_conv, stacked, wl, *, bt=8):
    rows = p.shape[0]
    depth = state_c.shape[0]
    assert rows % bt == 0
    nblk = rows // bt
    first = stacked is None
    n_slots = depth if first else 1

    def blk_i(slot, i):
        return jnp.where(slot == 0, i, nblk - 1)

    def lay(*s):
        return pl.BlockSpec((None, bt) + s, lambda slot, i: (layer, blk_i(slot, i)) + (0,) * len(s))

    def row(*s):
        return pl.BlockSpec((bt,) + s, lambda slot, i: (blk_i(slot, i),) + (0,) * len(s))

    def stk(*s):
        return pl.BlockSpec((None, bt) + s,
                            lambda slot, i: ((slot if first else layer), i) + (0,) * len(s))

    conv_in = pl.BlockSpec((None, HIST, bt, CONV_CH),
                           lambda slot, i: (layer, 0, blk_i(slot, i), 0))
    conv_stk = pl.BlockSpec((None, HIST, bt, CONV_CH),
                            lambda slot, i: ((slot if first else layer), 0, i, 0))
    lw = functools.partial(_layer_resident, layer)
    in_specs = [
        row(PROJ_PAD), lay(HEADS, HEAD_DIM, HEAD_DIM), lay(MLSTM_WIDTH), lay(LANES), conv_in,
        lw((1, LANES)), lw((1, MLSTM_WIDTH)), lw((CONV_WIDTH, CONV_CH)),
        lw((1, CONV_CH)), lw((1, CONV_CH)), lw((1, CONV_CH)),
    ]
    args = [p, state_c, state_n2, state_m_pad, state_conv, wl["b_gates"], wl["g_head"], wl["w_dw"],
            wl["b_dw"], wl["ln_g"], wl["ln_b"]]
    aliases = {}
    if not first:
        aliases = {len(args): 2, len(args) + 1: 5}
        in_specs += [pl.BlockSpec(memory_space=pl.ANY), pl.BlockSpec(memory_space=pl.ANY)]
        args += list(stacked)
    out_shape = [
        jax.ShapeDtypeStruct((rows, MLSTM_WIDTH), F32),
        jax.ShapeDtypeStruct((rows, CONV_CH), F32),
        jax.ShapeDtypeStruct(state_c.shape, F32),
        jax.ShapeDtypeStruct((rows, MLSTM_WIDTH), F32),
        jax.ShapeDtypeStruct((rows, LANES), F32),
        jax.ShapeDtypeStruct(state_conv.shape, F32),
    ]
    hm, zc, c_new, n_new, m_new, conv_new = pl.pallas_call(
        functools.partial(_sample_step_kernel, bt=bt, n_alias=len(aliases),
                          fill_other=first and depth > 1),
        grid=(n_slots, nblk),
        in_specs=in_specs,
        out_specs=[row(MLSTM_WIDTH), row(CONV_CH), stk(HEADS, HEAD_DIM, HEAD_DIM),
                   row(MLSTM_WIDTH), row(LANES), conv_stk],
        out_shape=out_shape,
        input_output_aliases=aliases,
        compiler_params=pltpu.CompilerParams(
            dimension_semantics=("arbitrary", "arbitrary"), vmem_limit_bytes=VMEM_LIMIT),
        name="sample_step",
    )(*args)
    return hm, zc, (c_new, conv_new), n_new.reshape(rows, HEADS, HEAD_DIM), m_new[:, :HEADS]


def _sample_out_kernel(x_ref, hm_ref, zc_ref, mod_ref, wout_ref, gffn_ref, w13_ref, w2_ref,
                       gfin_ref, y_ref, mix_ref, x1_ref, h2_ref, act_ref, *, rows, final):
    mix_ref[:, 0:MLSTM_WIDTH] = hm_ref[...].astype(BF16)
    mix_ref[:, MLSTM_WIDTH:] = zc_ref[...].astype(BF16)

    def mod_rows(name, cs, r=slice(None)):
        i = _MOD_ROW[name]
        if isinstance(cs, slice) and cs == slice(None):
            cs = slice(0, D_MODEL)
        return mod_ref[r, i * D_MODEL + cs.start:i * D_MODEL + cs.stop]

    _run(itertools.chain(
        _out_proj_steps(x_ref, mix_ref, mod_rows, wout_ref, x1_ref),
        _norm_steps(x1_ref, h2_ref, gffn_ref, mod_rows, "sc2", "sh2", rows=rows, row_chunk=32),
        _ffn_gate_steps(h2_ref, w13_ref, act_ref),
        _ffn_out_steps(x1_ref, act_ref, lambda cs: mod_rows("ga2", cs), w2_ref, gfin_ref, y_ref,
                       rows=rows, row_chunk=32, final=final)))


def _sample_out(x, hm, zc, mod, layer, wl, g_final, *, final):
    rows, d = x.shape
    lw = functools.partial(_layer_resident, layer)
    return pl.pallas_call(
        functools.partial(_sample_out_kernel, rows=rows, final=final),
        grid=(1,),
        in_specs=[_resident((rows, d)), _resident(hm.shape), _resident(zc.shape),
                  _resident(mod.shape), lw((d, d)), lw((1, d)), lw((d, 2 * D_FF)), lw((D_FF, d)),
                  _resident((1, d))],
        out_specs=pl.BlockSpec((rows, d), lambda i: (0, 0)),
        out_shape=jax.ShapeDtypeStruct((rows, D_MODEL), F32),
        scratch_shapes=[
            pltpu.VMEM((rows, D_MODEL), BF16),
            pltpu.VMEM((rows, D_MODEL), F32),
            pltpu.VMEM((rows, D_MODEL), BF16),
            pltpu.VMEM((rows, D_FF), BF16),
        ],
        compiler_params=pltpu.CompilerParams(
            dimension_semantics=("arbitrary",), vmem_limit_bytes=VMEM_LIMIT),
        name="sample_out",
    )(x, hm, zc, mod, wl["w_out"], wl["g_ffn"], wl["w_ffn13"], wl["w_ffn2"], g_final)


def _prep_weights(g_mix, w_in, b_gates, g_head, w_dw, b_dw, ln_g, ln_b, w_out, g_ffn, w_ffn13,
                  w_ffn2):
    depth, d, _ = w_in.shape
    n_gate = 2 * HEADS
    gate0 = 4 * MLSTM_WIDTH
    w_r = jnp.concatenate(
        [w_in[:, :, :gate0], w_in[:, :, gate0 + n_gate:], w_in[:, :, gate0:gate0 + n_gate],
         jnp.zeros((depth, d, LANES - n_gate), w_in.dtype)], axis=2).astype(BF16)
    return {
        "g_mix": g_mix.reshape(depth, 1, d),
        "w_in": w_r,
        "b_gates": jnp.pad(b_gates, ((0, 0), (0, LANES - n_gate))).reshape(depth, 1, LANES),
        "g_head": g_head.reshape(depth, 1, MLSTM_WIDTH),
        "w_dw": w_dw,
        "b_dw": b_dw.reshape(depth, 1, CONV_CH),
        "ln_g": ln_g.reshape(depth, 1, CONV_CH),
        "ln_b": ln_b.reshape(depth, 1, CONV_CH),
        "w_out": w_out.astype(BF16),
        "g_ffn": g_ffn.reshape(depth, 1, d),
        "w_ffn13": w_ffn13.astype(BF16),
        "w_ffn2": w_ffn2.astype(BF16),
    }


def _forward(x_prompt, x_sample, c_prompt, c_sample, state_C, state_n, state_m, state_conv,
             w_ada, b_ada, g_mix, w_in, b_gates, g_head, w_dw, b_dw, ln_g, ln_b, w_out,
             g_ffn, w_ffn13, w_ffn2, g_final, *, tm):
    depth = w_ada.shape[0]
    bp = x_prompt.shape[0]
    d = x_prompt.shape[-1]
    mod = _modulation(jnp.concatenate([c_prompt, c_sample], axis=0), w_ada, b_ada)
    gfin = g_final.reshape(1, d)
    xp = x_prompt
    xs = x_sample.reshape(x_sample.shape[0], d)
    outs_p = [[], [], [], []]
    ns_l, ms_l = [], []
    stacked = None
    n_seq = xs.shape[0]
    state_n2 = state_n.reshape(depth, n_seq, MLSTM_WIDTH)
    state_m_pad = jnp.pad(state_m, ((0, 0), (0, 0), (0, LANES - HEADS)))
    state_conv_t = jnp.swapaxes(state_conv, 1, 2)
    wl = _prep_weights(g_mix, w_in, b_gates, g_head, w_dw, b_dw, ln_g, ln_b, w_out, g_ffn,
                       w_ffn13, w_ffn2)
    for l in range(depth):
        final = l == depth - 1
        mod_p = mod[l, :bp].reshape(bp, 6, d)
        mod_s = mod[l, bp:]
        xp, c1, n1, m1, b1 = _prompt_layer(xp, mod_p, l, wl, gfin, tm=tm, final=final)
        for acc, val in zip(outs_p, (c1, n1, m1[:, :, 0], b1)):
            acc.append(val)
        p = _sample_in_proj(xs, mod_s, l, wl)
        hm, zc, stacked, n2, m2 = _sample_step(p, l, state_C, state_n2, state_m_pad, state_conv_t,
                                               stacked, wl)
        xs = _sample_out(xs, hm, zc, mod_s, l, wl, gfin, final=final)
        ns_l.append(n2)
        ms_l.append(m2)
    y_sample = xs.reshape(x_sample.shape)
    c_sample_new, conv_t_new = stacked
    return ((xp, y_sample) + tuple(jnp.stack(a) for a in outs_p)
            + (c_sample_new, jnp.stack(ns_l), jnp.stack(ms_l), jnp.swapaxes(conv_t_new, 1, 2)))


def kernel(x_prompt, x_sample, c_prompt, c_sample, state_C, state_n, state_m, state_conv, w_ada, b_ada, g_mix, w_in, b_gates, g_head, w_dw, b_dw, ln_g, ln_b, w_out, g_ffn, w_ffn13, w_ffn2, g_final):
    return _forward(x_prompt, x_sample, c_prompt, c_sample, state_C, state_n, state_m, state_conv,
                    w_ada, b_ada, g_mix, w_in, b_gates, g_head, w_dw, b_dw, ln_g, ln_b, w_out,
                    g_ffn, w_ffn13, w_ffn2, g_final, tm=256)
```

```python
import functools
import itertools

import jax
import jax.numpy as jnp
from jax import lax
from jax.experimental import pallas as pl
from jax.experimental.pallas import tpu as pltpu

F32 = jnp.float32
BF16 = jnp.bfloat16

D_MODEL = 1024
HEADS = 4
HEAD_DIM = 128
MLSTM_WIDTH = HEADS * HEAD_DIM
CONV_CH = D_MODEL - MLSTM_WIDTH
CONV_WIDTH = 31
HIST = CONV_WIDTH - 1
D_FF = 2816
CHUNK = 128
EPS = 1e-6

LANES = 128
SUBLANES = 8
MXU_N = 256

COL_Q = 0
COL_K = MLSTM_WIDTH
COL_V = 2 * MLSTM_WIDTH
COL_O = 3 * MLSTM_WIDTH
COL_A = 4 * MLSTM_WIDTH
COL_B = COL_A + CONV_CH
COL_G = COL_B + CONV_CH
PROJ_PAD = COL_G + LANES
HIST_PAD = 32
CONV_SLABS = CONV_CH // LANES
CONV_STRIDE = 2
D_OUT_PAD = D_MODEL + LANES
VMEM_LIMIT = 60 * 1024 * 1024


def _log_sigmoid(x):
    return jnp.minimum(x, 0.0) - jnp.log1p(jnp.exp(-jnp.abs(x)))


def _sigmoid(x):
    return 1.0 / (1.0 + jnp.exp(-x))


def _bdot(a, b):
    return jnp.dot(a, b, preferred_element_type=F32)


def _bdot_nt(a, b):
    return lax.dot_general(a, b, (((1,), (1,)), ((), ())), preferred_element_type=F32)


def _row_slices(rows, chunk):
    chunk = min(chunk, rows)
    assert rows % chunk == 0
    return [slice(r0, r0 + chunk) for r0 in range(0, rows, chunk)]


def _rms_mod(x, g, sc, sh):
    ms = jnp.mean(x * x, axis=-1, keepdims=True)
    return (x * lax.rsqrt(ms + EPS) * g) * (1.0 + sc) + sh


def _mod_kernel(c_ref, w_ref, b_ref, o_ref):
    o_ref[...] = _bdot(c_ref[...].astype(BF16), w_ref[...].astype(BF16)) + b_ref[...]


def _modulation(c_all, w_ada, b_ada):
    depth, d, n6 = w_ada.shape
    rows = c_all.shape[0]
    tn = 1024
    return pl.pallas_call(
        _mod_kernel,
        grid=(depth, n6 // tn),
        in_specs=[
            pl.BlockSpec((rows, d), lambda l, n: (0, 0)),
            pl.BlockSpec((None, d, tn), lambda l, n: (l, 0, n)),
            pl.BlockSpec((None, 1, tn), lambda l, n: (l, 0, n)),
        ],
        out_specs=pl.BlockSpec((None, rows, tn), lambda l, n: (l, 0, n)),
        out_shape=jax.ShapeDtypeStruct((depth, rows, n6), F32),
        compiler_params=pltpu.CompilerParams(
            dimension_semantics=("arbitrary", "arbitrary"), vmem_limit_bytes=VMEM_LIMIT),
        name="adaln_mod",
    )(c_all, w_ada, b_ada.reshape(depth, 1, n6))


def _norm_steps(src_ref, dst_ref, g_ref, mod, sc, sh, *, rows, row_chunk):
    for r in _row_slices(rows, row_chunk):
        h = _rms_mod(src_ref[r, :], g_ref[...], mod(sc, slice(None), r), mod(sh, slice(None), r))
        dst_ref[r, :] = h.astype(BF16)
        yield


def _in_proj_steps(h_ref, w_refs, p_ref):
    for w_ref, col0, width in zip(w_refs, (COL_Q, COL_A, COL_G), (COL_A, 2 * CONV_CH, LANES)):
        for n0 in range(0, width, MXU_N):
            n1 = min(n0 + MXU_N, width)
            p_ref[:, col0 + n0:col0 + n1] = _bdot(h_ref[...], w_ref[:, n0:n1])
            yield


def _out_proj_steps(x_ref, mix_ref, mod, wout_ref, x1_ref):
    for n0 in range(0, D_MODEL, MXU_N):
        cs = slice(n0, n0 + MXU_N)
        mx = _bdot(mix_ref[...], wout_ref[:, cs])
        x1_ref[:, cs] = x_ref[:, cs] + mod("ga1", cs, slice(None)) * mx
        yield


def _ffn_gate_steps(h2_ref, w13_ref, act_ref):
    for n0 in range(0, D_FF, MXU_N):
        gg = _bdot(h2_ref[...], w13_ref[:, n0:n0 + MXU_N])
        gg = gg * _sigmoid(gg)
        yield
        uu = _bdot(h2_ref[...], w13_ref[:, D_FF + n0:D_FF + n0 + MXU_N])
        act_ref[:, n0:n0 + MXU_N] = (gg * uu).astype(BF16)
        yield


def _ffn_out_steps(x1_ref, act_ref, ga2, w2_ref, gfin_ref, y_ref, *, rows, row_chunk, final):
    dst = x1_ref if final else y_ref
    for n0 in range(0, D_MODEL, MXU_N):
        cs = slice(n0, n0 + MXU_N)
        ff = _bdot(act_ref[...], w2_ref[:, cs])
        dst[:, cs] = x1_ref[:, cs] + ga2(cs) * ff
        yield
    if final:
        for r in _row_slices(rows, row_chunk):
            xb = x1_ref[r, :]
            ms = jnp.mean(xb * xb, axis=-1, keepdims=True)
            y_ref[r, :] = xb * lax.rsqrt(ms + EPS) * gfin_ref[...]
            yield


def _run(*gens):
    live = [g if isinstance(g, tuple) else (g, 1) for g in gens]
    while live:
        for entry in list(live):
            g, k = entry
            for _ in range(k):
                try:
                    next(g)
                except StopIteration:
                    live.remove(entry)
                    break


def _layernorm_silu(z, lng, lnb):
    mu = jnp.mean(z, axis=-1, keepdims=True)
    zc = z - mu
    var = jnp.mean(zc * zc, axis=-1, keepdims=True)
    zn = zc * lax.rsqrt(var + EPS) * lng + lnb
    return zn * _sigmoid(zn)


_MOD_ROW = {"sh1": 0, "sc1": 1, "ga1": 2, "sh2": 3, "sc2": 4, "ga2": 5}


def _cumsum_chunk(tril_bf, x):
    hi = x.astype(BF16)
    r1 = x - hi.astype(F32)
    mid = r1.astype(BF16)
    lo = (r1 - mid.astype(F32)).astype(BF16)
    return _bdot(tril_bf, hi) + _bdot(tril_bf, mid) + _bdot(tril_bf, lo)


def _mlstm_chunk(p_ref, r0, bg_ref, ghead_ref, mix_ref, C_ref, n_ref, m_ref, tril, tril_bf):
    rs = slice(r0, r0 + CHUNK)
    gates = p_ref[rs, COL_G:COL_G + LANES] + bg_ref[...]
    fcum = _cumsum_chunk(tril_bf, _log_sigmoid(gates))
    gates_t = gates.T
    fcum_t = fcum.T
    yield
    heads = range(HEADS)
    col = lambda base, h: slice(base + h * HEAD_DIM, base + (h + 1) * HEAD_DIM)
    st = [dict() for _ in heads]
    for h in heads:
        d = st[h]
        d["q"] = p_ref[rs, col(COL_Q, h)]
        d["k"] = p_ref[rs, col(COL_K, h)] * (HEAD_DIM ** -0.5)
        d["v_bf"] = p_ref[rs, col(COL_V, h)].astype(BF16)
        d["q_bf"] = d["q"].astype(BF16)
        d["f_col"] = fcum[:, HEADS + h:HEADS + h + 1]
        i_col = gates[:, h:h + 1]
        f_row = fcum_t[HEADS + h:HEADS + h + 1, :]
        i_row = gates_t[h:h + 1, :]
        g = f_row[:, CHUNK - 1:CHUNK]
        d["g"] = g
        d["dlog"] = jnp.where(tril, d["f_col"] - f_row + i_row, -jnp.inf)
        d["dmax"] = jnp.max(d["dlog"], axis=-1, keepdims=True)
        d["wlog"] = g - d["f_col"] + i_col
        d["wmax"] = jnp.max(g - f_row + i_row, axis=-1, keepdims=True)
        d["qk"] = _bdot_nt(d["q_bf"], d["k"].astype(BF16))
        yield
    for h in heads:
        d = st[h]
        m_prev = m_ref[h:h + 1, 0:1]
        inter = d["f_col"] + m_prev
        m_t = jnp.maximum(inter, d["dmax"])
        d["a_inter"] = jnp.exp(inter - m_t)
        d["ebound"] = jnp.exp(-m_t)
        s = d["qk"] * jnp.exp(d["dlog"] - m_t)
        d["ssum"] = jnp.sum(s, axis=-1, keepdims=True)
        d["sv"] = _bdot(s.astype(BF16), d["v_bf"])
        m_new = jnp.maximum(d["g"] + m_prev, d["wmax"])
        d["decay"] = jnp.exp(d["g"] + m_prev - m_new)
        kw = d["k"] * jnp.exp(d["wlog"] - m_new)
        d["kwsum"] = jnp.sum(kw, axis=0, keepdims=True)
        d["kv"] = _bdot(kw.T.astype(BF16), d["v_bf"])
        m_ref[h:h + 1, :] = jnp.broadcast_to(m_new, (1, LANES))
        yield
    for h in heads:
        d = st[h]
        n_row = n_ref[h:h + 1, :]
        c_mat = C_ref[h]
        num = d["sv"] + _bdot(d["q_bf"], c_mat.astype(BF16)) * d["a_inter"]
        qn = jnp.sum(d["q"] * n_row, axis=-1, keepdims=True)
        den = d["ssum"] + d["a_inter"] * qn
        bound = jnp.maximum(jnp.abs(den), d["ebound"])
        hh = num * (1.0 / bound)
        C_ref[h] = d["decay"] * c_mat + d["kv"]
        n_ref[h:h + 1, :] = d["decay"] * n_row + d["kwsum"]
        hn = hh * lax.rsqrt(jnp.mean(hh * hh, axis=-1, keepdims=True) + EPS) * ghead_ref[:, col(0, h)]
        mix_ref[rs, col(0, h)] = (hn * _sigmoid(p_ref[rs, col(COL_O, h)])).astype(BF16)
        yield


def _prompt_layer_kernel(x_ref, mod1_ref, mod2_ref, gmix_ref, wqkvo_ref, wglu_ref, wgate_ref, bg_ref,
                         ghead_ref, wdw_ref, bdw_ref, lng_ref, lnb_ref, wout_ref, gffn_ref, w13_ref,
                         w2_ref, gfin_ref,
                         y_ref, c_out, n_out, m_out, conv_out,
                         h_ref, p_ref, mix_ref, xres_ref, x1_ref, h2_ref, ga2_ref, act_ref, ubuf_ref,
                         zc_ref, C_ref, n_ref, m_ref, *, tm, tiles_per_seq, final):
    s = pl.program_id(0)
    n_tiles = pl.num_programs(0) - 2
    j = jnp.clip(s - 1, 0, n_tiles - 1) % tiles_per_seq
    last = tiles_per_seq - 1
    row_chunk = 32

    @pl.when(s == 0)
    def _():
        for ref in (p_ref, xres_ref, x1_ref, h2_ref, ga2_ref):
            ref[...] = jnp.zeros_like(ref)

    @pl.when(j == 0)
    def _():
        C_ref[...] = jnp.zeros_like(C_ref)
        n_ref[...] = jnp.zeros_like(n_ref)
        m_ref[...] = jnp.zeros_like(m_ref)
        ubuf_ref[:, 0:HIST_PAD, :] = jnp.zeros((CONV_SLABS, HIST_PAD, LANES), F32)

    @pl.when(j > 0)
    def _():
        ubuf_ref[:, 0:HIST_PAD, :] = ubuf_ref[:, tm:tm + HIST_PAD, :]

    def mod1(name, cs, r=None):
        i = _MOD_ROW[name]
        return mod1_ref[i:i + 1, cs]

    def mod2(name, cs, r=None):
        i = _MOD_ROW[name]
        return mod2_ref[i:i + 1, cs]

    row_id = lax.broadcasted_iota(jnp.int32, (CHUNK, CHUNK), 0)
    col_id = lax.broadcasted_iota(jnp.int32, (CHUNK, CHUNK), 1)
    tril = row_id >= col_id
    tril_bf = jnp.where(tril, 1.0, 0.0).astype(BF16)

    def mlstm_steps():
        for c in range(tm // CHUNK):
            yield from _mlstm_chunk(p_ref, c * CHUNK, bg_ref, ghead_ref, mix_ref, C_ref, n_ref,
                                    m_ref, tril, tril_bf)

    def glu_steps():
        for r in _row_slices(tm, row_chunk):
            ro = slice(r.start + HIST_PAD, r.stop + HIST_PAD)
            u = p_ref[r, COL_A:COL_A + CONV_CH] * _sigmoid(p_ref[r, COL_B:COL_B + CONV_CH])
            for g in range(CONV_SLABS):
                ubuf_ref[g, ro, :] = u[:, g * LANES:(g + 1) * LANES]
            yield

    def conv_steps(slabs):
        n_sets = row_chunk // SUBLANES
        firsts = [(t0 // CONV_STRIDE) * CONV_STRIDE * SUBLANES + t0 % CONV_STRIDE
                  for t0 in range(n_sets)]
        for g in slabs:
            gs = slice(g * LANES, (g + 1) * LANES)
            bias = jnp.broadcast_to(bdw_ref[:, gs], (SUBLANES, LANES))
            for base in range(0, tm, row_chunk):
                acc = [bias for _ in range(n_sets)]
                for w in range(CONV_WIDTH):
                    tap = wdw_ref[w:w + 1, gs]
                    for t0 in range(n_sets):
                        src = pl.ds(base + firsts[t0] + HIST_PAD - HIST + w, SUBLANES,
                                    stride=CONV_STRIDE)
                        acc[t0] = acc[t0] + ubuf_ref[g, src, :] * tap
                for t0 in range(n_sets):
                    zc_ref[g, pl.ds(base + firsts[t0], SUBLANES, stride=CONV_STRIDE), :] = acc[t0]
                yield

    def conv_norm_steps():
        for r in _row_slices(tm, row_chunk):
            z = jnp.concatenate([zc_ref[g, r, :] for g in range(CONV_SLABS)], axis=-1)
            mix_ref[r, MLSTM_WIDTH:] = _layernorm_silu(z, lng_ref[...], lnb_ref[...]).astype(BF16)
            yield

    def keep_steps():
        xres_ref[...] = x_ref[...]
        ga2_ref[...] = mod2("ga2", slice(None))
        yield

    _run(mlstm_steps(), _ffn_gate_steps(h2_ref, w13_ref, act_ref),
         (itertools.chain(glu_steps(), conv_steps(range(CONV_SLABS))), 2),
         _norm_steps(x_ref, h_ref, gmix_ref, mod1, "sc1", "sh1", rows=tm, row_chunk=row_chunk))

    @pl.when(jnp.logical_and(j == last, jnp.logical_and(s >= 1, s <= n_tiles)))
    def _():
        c_out[...] = C_ref[...]
        n_out[...] = n_ref[0:HEADS, :]
        m_out[...] = m_ref[0:HEADS, :]
        for g in range(CONV_SLABS):
            conv_out[:, g * LANES:(g + 1) * LANES] = ubuf_ref[g, tm + HIST_PAD - HIST:tm + HIST_PAD, :]

    @pl.when(s >= 0)
    def _():
        _run(_ffn_out_steps(x1_ref, act_ref, lambda cs: ga2_ref[:, cs], w2_ref, gfin_ref, y_ref,
                            rows=tm, row_chunk=row_chunk, final=final),
             (conv_norm_steps(), 2))

    @pl.when(s >= 0)
    def _():
        _run(_out_proj_steps(xres_ref, mix_ref, mod2, wout_ref, x1_ref))
        _run(_in_proj_steps(h_ref, (wqkvo_ref, wglu_ref, wgate_ref), p_ref),
             _norm_steps(x1_ref, h2_ref, gffn_ref, mod2, "sc2", "sh2", rows=tm, row_chunk=row_chunk))
        _run(keep_steps())


def _resident(shape):
    nd = len(shape)
    return pl.BlockSpec(shape, lambda *g: (0,) * nd, pipeline_mode=pl.Buffered(1))


def _layer_resident(layer, shape):
    nd = len(shape)
    return pl.BlockSpec((None,) + shape, lambda *g: (layer,) + (0,) * nd,
                        pipeline_mode=pl.Buffered(1))


def _prompt_layer(x, mod, layer, wl, g_final, *, tm, final):
    bsz, seq, d = x.shape
    lw = functools.partial(_layer_resident, layer)
    assert seq % tm == 0 and tm % CHUNK == 0 and tm >= HIST_PAD
    nt = seq // tm
    n_tiles = bsz * nt
    kern = functools.partial(_prompt_layer_kernel, tm=tm, tiles_per_seq=nt, final=final)

    def tile(s, lag):
        return jnp.clip(s - lag, 0, n_tiles - 1)

    def cur(s):
        return tile(s, 1)

    def prev(s):
        return tile(s, 2)

    in_specs = [
        pl.BlockSpec((None, tm, d), lambda s: (tile(s, 0) // nt, tile(s, 0) % nt, 0)),
        pl.BlockSpec((None, 6, d), lambda s: (tile(s, 0) // nt, 0, 0)),
        pl.BlockSpec((None, 6, d), lambda s: (cur(s) // nt, 0, 0)),
        lw((1, d)),
        lw((d, COL_A)),
        lw((d, D_OUT_PAD)),
        lw((d, LANES)),
        lw((1, LANES)),
        lw((1, MLSTM_WIDTH)),
        lw((CONV_WIDTH, CONV_CH)),
        lw((1, CONV_CH)),
        lw((1, CONV_CH)),
        lw((1, CONV_CH)),
        lw((d, D_OUT_PAD)),
        lw((1, d)),
        lw((d, 2 * D_FF)),
        lw((D_FF, D_OUT_PAD)),
        _resident((1, d)),
    ]
    out_specs = [
        pl.BlockSpec((None, tm, d), lambda s: (prev(s) // nt, prev(s) % nt, 0)),
        pl.BlockSpec((None, HEADS, HEAD_DIM, HEAD_DIM), lambda s: (cur(s) // nt, 0, 0, 0)),
        pl.BlockSpec((None, HEADS, HEAD_DIM), lambda s: (cur(s) // nt, 0, 0)),
        pl.BlockSpec((None, HEADS, LANES), lambda s: (cur(s) // nt, 0, 0)),
        pl.BlockSpec((None, HIST, CONV_CH), lambda s: (cur(s) // nt, 0, 0)),
    ]
    out_shape = [
        jax.ShapeDtypeStruct((bsz, seq, d), F32),
        jax.ShapeDtypeStruct((bsz, HEADS, HEAD_DIM, HEAD_DIM), F32),
        jax.ShapeDtypeStruct((bsz, HEADS, HEAD_DIM), F32),
        jax.ShapeDtypeStruct((bsz, HEADS, LANES), F32),
        jax.ShapeDtypeStruct((bsz, HIST, CONV_CH), F32),
    ]
    scratch = [
        pltpu.VMEM((tm, d), BF16),
        pltpu.VMEM((tm, PROJ_PAD), F32),
        pltpu.VMEM((tm, d), BF16),
        pltpu.VMEM((tm, d), F32),
        pltpu.VMEM((tm, d), F32),
        pltpu.VMEM((tm, d), BF16),
        pltpu.VMEM((1, d), F32),
        pltpu.VMEM((tm, D_FF), BF16),
        pltpu.VMEM((CONV_SLABS, tm + HIST_PAD, LANES), F32),
        pltpu.VMEM((CONV_SLABS, tm, LANES), F32),
        pltpu.VMEM((HEADS, HEAD_DIM, HEAD_DIM), F32),
        pltpu.VMEM((SUBLANES, HEAD_DIM), F32),
        pltpu.VMEM((SUBLANES, LANES), F32),
    ]
    return pl.pallas_call(
        kern,
        grid=(n_tiles + 2,),
        in_specs=in_specs,
        out_specs=out_specs,
        out_shape=out_shape,
        scratch_shapes=scratch,
        compiler_params=pltpu.CompilerParams(
            dimension_semantics=("arbitrary",), vmem_limit_bytes=VMEM_LIMIT),
        name="prompt_layer",
    )(x, mod, mod, wl["g_mix"], wl["w_qkvo"], wl["w_glu"], wl["w_gate"], wl["b_gates"], wl["g_head"], wl["w_dw"], wl["b_dw"],
      wl["ln_g"], wl["ln_b"], wl["w_out"], wl["g_ffn"], wl["w_ffn13"], wl["w_ffn2"], g_final)


def _sample_in_kernel(x_ref, mod_ref, gmix_ref, wqkvo_ref, wglu_ref, wgate_ref, p_ref, h_ref):
    h = _rms_mod(x_ref[...], gmix_ref[...], mod_ref[:, D_MODEL:2 * D_MODEL], mod_ref[:, 0:D_MODEL])
    h_ref[...] = h.astype(BF16)
    _run(_in_proj_steps(h_ref, (wqkvo_ref, wglu_ref, wgate_ref), p_ref))


def _sample_in_proj(x, mod, layer, wl):
    rows, d = x.shape
    lw = functools.partial(_layer_resident, layer)
    return pl.pallas_call(
        _sample_in_kernel,
        grid=(1,),
        in_specs=[_resident((rows, d)), _resident(mod.shape), lw((1, d)), lw((d, COL_A)),
                  lw((d, D_OUT_PAD)), lw((d, LANES))],
        out_specs=pl.BlockSpec((rows, PROJ_PAD), lambda i: (0, 0)),
        out_shape=jax.ShapeDtypeStruct((rows, PROJ_PAD), F32),
        scratch_shapes=[pltpu.VMEM((rows, D_MODEL), BF16)],
        compiler_params=pltpu.CompilerParams(
            dimension_semantics=("arbitrary",), vmem_limit_bytes=VMEM_LIMIT),
        name="sample_in_proj",
    )(x, mod, wl["g_mix"], wl["w_qkvo"], wl["w_glu"], wl["w_gate"])


def _sample_step_kernel(p_ref, c_ref, n_ref, m_ref, conv_ref, bg_ref, ghead_ref, wdw_ref, bdw_ref,
                        lng_ref, lnb_ref, *rest, bt, n_alias, fill_other):
    hm_ref, zc_ref, c_out, n_out, m_out, conv_out = rest[n_alias:]
    slot = pl.program_id(0)

    @pl.when(slot == 0)
    def _():
        scale = HEAD_DIM ** -0.5
        gates = p_ref[:, COL_G:COL_G + LANES] + bg_ref[...]
        logf = pltpu.roll(_log_sigmoid(gates), LANES - HEADS, axis=1)
        inter = logf + m_ref[...]
        m_t = jnp.maximum(inter, gates)
        dw_all = jnp.exp(gates - m_t)
        a_all = jnp.exp(inter - m_t)
        eb_all = jnp.exp(-m_t)
        m_out[...] = m_t
        for h in range(HEADS):
            hs = slice(h * HEAD_DIM, (h + 1) * HEAD_DIM)
            q = p_ref[:, COL_Q + h * HEAD_DIM:COL_Q + (h + 1) * HEAD_DIM]
            k = p_ref[:, COL_K + h * HEAD_DIM:COL_K + (h + 1) * HEAD_DIM] * scale
            v = p_ref[:, COL_V + h * HEAD_DIM:COL_V + (h + 1) * HEAD_DIM]
            o = p_ref[:, COL_O + h * HEAD_DIM:COL_O + (h + 1) * HEAD_DIM]
            dw = dw_all[:, h:h + 1]
            a_in = a_all[:, h:h + 1]
            n_h = n_ref[:, hs]
            kw = k * dw
            s = jnp.sum(q * k, axis=-1, keepdims=True) * dw
            qn = jnp.sum(q * n_h, axis=-1, keepdims=True)
            n_out[:, hs] = a_in * n_h + kw
            q_t = q.T
            kw_t = kw.T
            a_rows = jnp.broadcast_to(a_in, (bt, HEAD_DIM))
            qc_rows = []
            for b in range(bt):
                c_mat = c_ref[b, h]
                qc_rows.append(jnp.sum(q_t[:, b:b + 1] * c_mat, axis=0, keepdims=True))
                c_out[b, h] = a_rows[b:b + 1, :] * c_mat + kw_t[:, b:b + 1] * v[b:b + 1, :]
            qc = jnp.concatenate(qc_rows, axis=0)
            num = s * v + qc * a_in
            den = s + a_in * qn
            bound = jnp.maximum(jnp.abs(den), eb_all[:, h:h + 1])
            hh = num * (1.0 / bound)
            hn = hh * lax.rsqrt(jnp.mean(hh * hh, axis=-1, keepdims=True) + EPS) * ghead_ref[:, hs]
            hm_ref[:, hs] = hn * _sigmoid(o)

        u = p_ref[:, COL_A:COL_A + CONV_CH] * _sigmoid(p_ref[:, COL_B:COL_B + CONV_CH])
        z = u * wdw_ref[HIST:HIST + 1, :] + bdw_ref[...]
        for w in range(HIST):
            z = z + conv_ref[w] * wdw_ref[w:w + 1, :]
        zc_ref[...] = _layernorm_silu(z, lng_ref[...], lnb_ref[...])
        conv_out[0:HIST - 1] = conv_ref[1:HIST]
        conv_out[HIST - 1] = u

    if fill_other:
        @pl.when(slot != 0)
        def _():
            c_out[...] = jnp.zeros_like(c_out)
            conv_out[...] = jnp.zeros_like(conv_out)


def _sample_step(p, layer, state_c, state_n2, state_m_pad, state_conv, stacked, wl, *, bt=8):
    rows = p.shape[0]
    depth = state_c.shape[0]
    assert rows % bt == 0
    nblk = rows // bt
    first = stacked is None
    n_slots = depth if first else 1

    def blk_i(slot, i):
        return jnp.where(slot == 0, i, nblk - 1)

    def lay(*s):
        return pl.BlockSpec((None, bt) + s, lambda slot, i: (layer, blk_i(slot, i)) + (0,) * len(s))

    def row(*s):
        return pl.BlockSpec((bt,) + s, lambda slot, i: (blk_i(slot, i),) + (0,) * len(s))

    def stk(*s):
        return pl.BlockSpec((None, bt) + s,
                            lambda slot, i: ((slot if first else layer), i) + (0,) * len(s))

    conv_in = pl.BlockSpec((None, HIST, bt, CONV_CH),
                           lambda slot, i: (layer, 0, blk_i(slot, i), 0))
    conv_stk = pl.BlockSpec((None, HIST, bt, CONV_CH),
                            lambda slot, i: ((slot if first else layer), 0, i, 0))
    lw = functools.partial(_layer_resident, layer)
    in_specs = [
        row(PROJ_PAD), lay(HEADS, HEAD_DIM, HEAD_DIM), lay(MLSTM_WIDTH), lay(LANES), conv_in,
        lw((1, LANES)), lw((1, MLSTM_WIDTH)), lw((CONV_WIDTH, CONV_CH)),
        lw((1, CONV_CH)), lw((1, CONV_CH)), lw((1, CONV_CH)),
    ]
    args = [p, state_c, state_n2, state_m_pad, state_conv, wl["b_gates"], wl["g_head"], wl["w_dw"],
            wl["b_dw"], wl["ln_g"], wl["ln_b"]]
    aliases = {}
    if not first:
        aliases = {len(args): 2, len(args) + 1: 5}
        in_specs += [pl.BlockSpec(memory_space=pl.ANY), pl.BlockSpec(memory_space=pl.ANY)]
        args += list(stacked)
    out_shape = [
        jax.ShapeDtypeStruct((rows, MLSTM_WIDTH), F32),
        jax.ShapeDtypeStruct((rows, CONV_CH), F32),
        jax.ShapeDtypeStruct(state_c.shape, F32),
        jax.ShapeDtypeStruct((rows, MLSTM_WIDTH), F32),
        jax.ShapeDtypeStruct((rows, LANES), F32),
        jax.ShapeDtypeStruct(state_conv.shape, F32),
    ]
    hm, zc, c_new, n_new, m_new, conv_new = pl.pallas_call(
        functools.partial(_sample_step_kernel, bt=bt, n_alias=len(aliases),
                          fill_other=first and depth > 1),
        grid=(n_slots, nblk),
        in_specs=in_specs,
        out_specs=[row(MLSTM_WIDTH), row(CONV_CH), stk(HEADS, HEAD_DIM, HEAD_DIM),
                   row(MLSTM_WIDTH), row(LANES), conv_stk],
        out_shape=out_shape,
        input_output_aliases=aliases,
        compiler_params=pltpu.CompilerParams(
            dimension_semantics=("arbitrary", "arbitrary"), vmem_limit_bytes=VMEM_LIMIT),
        name="sample_step",
    )(*args)
    return hm, zc, (c_new, conv_new), n_new.reshape(rows, HEADS, HEAD_DIM), m_new[:, :HEADS]


def _sample_out_kernel(x_ref, hm_ref, zc_ref, mod_ref, wout_ref, gffn_ref, w13_ref, w2_ref,
                       gfin_ref, y_ref, mix_ref, x1_ref, h2_ref, act_ref, *, rows, final):
    mix_ref[:, 0:MLSTM_WIDTH] = hm_ref[...].astype(BF16)
    mix_ref[:, MLSTM_WIDTH:] = zc_ref[...].astype(BF16)

    def mod_rows(name, cs, r=slice(None)):
        i = _MOD_ROW[name]
        if isinstance(cs, slice) and cs == slice(None):
            cs = slice(0, D_MODEL)
        return mod_ref[r, i * D_MODEL + cs.start:i * D_MODEL + cs.stop]

    _run(itertools.chain(
        _out_proj_steps(x_ref, mix_ref, mod_rows, wout_ref, x1_ref),
        _norm_steps(x1_ref, h2_ref, gffn_ref, mod_rows, "sc2", "sh2", rows=rows, row_chunk=32),
        _ffn_gate_steps(h2_ref, w13_ref, act_ref),
        _ffn_out_steps(x1_ref, act_ref, lambda cs: mod_rows("ga2", cs), w2_ref, gfin_ref, y_ref,
                       rows=rows, row_chunk=32, final=final)))


def _sample_out(x, hm, zc, mod, layer, wl, g_final, *, final):
    rows, d = x.shape
    lw = functools.partial(_layer_resident, layer)
    return pl.pallas_call(
        functools.partial(_sample_out_kernel, rows=rows, final=final),
        grid=(1,),
        in_specs=[_resident((rows, d)), _resident(hm.shape), _resident(zc.shape),
                  _resident(mod.shape), lw((d, D_OUT_PAD)), lw((1, d)), lw((d, 2 * D_FF)),
                  lw((D_FF, D_OUT_PAD)),
                  _resident((1, d))],
        out_specs=pl.BlockSpec((rows, d), lambda i: (0, 0)),
        out_shape=jax.ShapeDtypeStruct((rows, D_MODEL), F32),
        scratch_shapes=[
            pltpu.VMEM((rows, D_MODEL), BF16),
            pltpu.VMEM((rows, D_MODEL), F32),
            pltpu.VMEM((rows, D_MODEL), BF16),
            pltpu.VMEM((rows, D_FF), BF16),
        ],
        compiler_params=pltpu.CompilerParams(
            dimension_semantics=("arbitrary",), vmem_limit_bytes=VMEM_LIMIT),
        name="sample_out",
    )(x, hm, zc, mod, wl["w_out"], wl["g_ffn"], wl["w_ffn13"], wl["w_ffn2"], g_final)


def _prep_weights(g_mix, w_in, b_gates, g_head, w_dw, b_dw, ln_g, ln_b, w_out, g_ffn, w_ffn13,
                  w_ffn2):
    depth, d, _ = w_in.shape
    n_gate = 2 * HEADS
    gate0 = 4 * MLSTM_WIDTH
    lane_pad = lambda w, n: jnp.pad(w.astype(BF16), ((0, 0), (0, 0), (0, n - w.shape[2])))
    return {
        "g_mix": g_mix.reshape(depth, 1, d),
        "w_qkvo": w_in[:, :, :gate0].astype(BF16),
        "w_glu": lane_pad(w_in[:, :, gate0 + n_gate:], D_OUT_PAD),
        "w_gate": lane_pad(w_in[:, :, gate0:gate0 + n_gate], LANES),
        "b_gates": jnp.pad(b_gates, ((0, 0), (0, LANES - n_gate))).reshape(depth, 1, LANES),
        "g_head": g_head.reshape(depth, 1, MLSTM_WIDTH),
        "w_dw": w_dw,
        "b_dw": b_dw.reshape(depth, 1, CONV_CH),
        "ln_g": ln_g.reshape(depth, 1, CONV_CH),
        "ln_b": ln_b.reshape(depth, 1, CONV_CH),
        "w_out": jnp.pad(w_out.astype(BF16), ((0, 0), (0, 0), (0, D_OUT_PAD - d))),
        "g_ffn": g_ffn.reshape(depth, 1, d),
        "w_ffn13": w_ffn13.astype(BF16),
        "w_ffn2": jnp.pad(w_ffn2.astype(BF16), ((0, 0), (0, 0), (0, D_OUT_PAD - d))),
    }


def _forward(x_prompt, x_sample, c_prompt, c_sample, state_C, state_n, state_m, state_conv,
             w_ada, b_ada, g_mix, w_in, b_gates, g_head, w_dw, b_dw, ln_g, ln_b, w_out,
             g_ffn, w_ffn13, w_ffn2, g_final, *, tm):
    depth = w_ada.shape[0]
    bp = x_prompt.shape[0]
    d = x_prompt.shape[-1]
    mod = _modulation(jnp.concatenate([c_prompt, c_sample], axis=0), w_ada, b_ada)
    gfin = g_final.reshape(1, d)
    xp = x_prompt
    xs = x_sample.reshape(x_sample.shape[0], d)
    outs_p = [[], [], [], []]
    ns_l, ms_l = [], []
    stacked = None
    n_seq = xs.shape[0]
    state_n2 = state_n.reshape(depth, n_seq, MLSTM_WIDTH)
    state_m_pad = jnp.pad(state_m, ((0, 0), (0, 0), (0, LANES - HEADS)))
    state_conv_t = jnp.swapaxes(state_conv, 1, 2)
    wl = _prep_weights(g_mix, w_in, b_gates, g_head, w_dw, b_dw, ln_g, ln_b, w_out, g_ffn,
                       w_ffn13, w_ffn2)
    for l in range(depth):
        final = l == depth - 1
        mod_p = mod[l, :bp].reshape(bp, 6, d)
        mod_s = mod[l, bp:]
        xp, c1, n1, m1, b1 = _prompt_layer(xp, mod_p, l, wl, gfin, tm=tm, final=final)
        for acc, val in zip(outs_p, (c1, n1, m1[:, :, 0], b1)):
            acc.append(val)
        p = _sample_in_proj(xs, mod_s, l, wl)
        hm, zc, stacked, n2, m2 = _sample_step(p, l, state_C, state_n2, state_m_pad, state_conv_t,
                                               stacked, wl)
        xs = _sample_out(xs, hm, zc, mod_s, l, wl, gfin, final=final)
        ns_l.append(n2)
        ms_l.append(m2)
    y_sample = xs.reshape(x_sample.shape)
    c_sample_new, conv_t_new = stacked
    return ((xp, y_sample) + tuple(jnp.stack(a) for a in outs_p)
            + (c_sample_new, jnp.stack(ns_l), jnp.stack(ms_l), jnp.swapaxes(conv_t_new, 1, 2)))


def kernel(x_prompt, x_sample, c_prompt, c_sample, state_C, state_n, state_m, state_conv, w_ada, b_ada, g_mix, w_in, b_gates, g_head, w_dw, b_dw, ln_g, ln_b, w_out, g_ffn, w_ffn13, w_ffn2, g_final):
    return _forward(x_prompt, x_sample, c_prompt, c_sample, state_C, state_n, state_m, state_conv,
                    w_ada, b_ada, g_mix, w_in, b_gates, g_head, w_dw, b_dw, ln_g, ln_b, w_out,
                    g_ffn, w_ffn13, w_ffn2, g_final, tm=256)
```

```python
import functools
import itertools

import jax
import jax.numpy as jnp
from jax import lax
from jax.experimental import pallas as pl
from jax.experimental.pallas import tpu as pltpu

F32 = jnp.float32
BF16 = jnp.bfloat16

D_MODEL = 1024
HEADS = 4
HEAD_DIM = 128
MLSTM_WIDTH = HEADS * HEAD_DIM
CONV_CH = D_MODEL - MLSTM_WIDTH
CONV_WIDTH = 31
HIST = CONV_WIDTH - 1
D_FF = 2816
CHUNK = 128
EPS = 1e-6

LANES = 128
SUBLANES = 8
MXU_N = 256

COL_Q = 0
COL_K = MLSTM_WIDTH
COL_V = 2 * MLSTM_WIDTH
COL_O = 3 * MLSTM_WIDTH
COL_A = 4 * MLSTM_WIDTH
COL_B = COL_A + CONV_CH
COL_G = COL_B + CONV_CH
PROJ_PAD = COL_G + LANES
HIST_PAD = 32
CONV_SLABS = CONV_CH // LANES
CONV_STRIDE = 2
D_OUT_PAD = D_MODEL + LANES
VMEM_LIMIT = 60 * 1024 * 1024


def _log_sigmoid(x):
    return jnp.minimum(x, 0.0) - jnp.log1p(jnp.exp(-jnp.abs(x)))


def _sigmoid(x):
    return 1.0 / (1.0 + jnp.exp(-x))


def _bdot(a, b):
    return jnp.dot(a, b, preferred_element_type=F32)


def _bdot_nt(a, b):
    return lax.dot_general(a, b, (((1,), (1,)), ((), ())), preferred_element_type=F32)


def _row_slices(rows, chunk):
    chunk = min(chunk, rows)
    assert rows % chunk == 0
    return [slice(r0, r0 + chunk) for r0 in range(0, rows, chunk)]


def _rms_mod(x, g, sc, sh):
    ms = jnp.mean(x * x, axis=-1, keepdims=True)
    return (x * lax.rsqrt(ms + EPS) * g) * (1.0 + sc) + sh


def _mod_kernel(c_ref, w_ref, b_ref, o_ref):
    o_ref[...] = _bdot(c_ref[...].astype(BF16), w_ref[...].astype(BF16)) + b_ref[...]


def _modulation(c_all, w_ada, b_ada):
    depth, d, n6 = w_ada.shape
    rows = c_all.shape[0]
    tn = 1024
    return pl.pallas_call(
        _mod_kernel,
        grid=(depth, n6 // tn),
        in_specs=[
            pl.BlockSpec((rows, d), lambda l, n: (0, 0)),
            pl.BlockSpec((None, d, tn), lambda l, n: (l, 0, n)),
            pl.BlockSpec((None, 1, tn), lambda l, n: (l, 0, n)),
        ],
        out_specs=pl.BlockSpec((None, rows, tn), lambda l, n: (l, 0, n)),
        out_shape=jax.ShapeDtypeStruct((depth, rows, n6), F32),
        compiler_params=pltpu.CompilerParams(
            dimension_semantics=("arbitrary", "arbitrary"), vmem_limit_bytes=VMEM_LIMIT),
        name="adaln_mod",
    )(c_all, w_ada, b_ada.reshape(depth, 1, n6))


def _norm_steps(src_ref, dst_ref, g_ref, mod, sc, sh, *, rows, row_chunk):
    for r in _row_slices(rows, row_chunk):
        h = _rms_mod(src_ref[r, :], g_ref[...], mod(sc, slice(None), r), mod(sh, slice(None), r))
        dst_ref[r, :] = h.astype(BF16)
        yield


def _in_proj_steps(h_ref, w_refs, p_ref):
    for w_ref, col0, width in zip(w_refs, (COL_Q, COL_A, COL_G), (COL_A, 2 * CONV_CH, LANES)):
        for n0 in range(0, width, MXU_N):
            n1 = min(n0 + MXU_N, width)
            p_ref[:, col0 + n0:col0 + n1] = _bdot(h_ref[...], w_ref[:, n0:n1])
            yield


def _out_proj_steps(x_ref, mix_ref, mod, wout_ref, x1_ref):
    for n0 in range(0, D_MODEL, MXU_N):
        cs = slice(n0, n0 + MXU_N)
        mx = _bdot(mix_ref[...], wout_ref[:, cs])
        x1_ref[:, cs] = x_ref[:, cs] + mod("ga1", cs, slice(None)) * mx
        yield


def _ffn_gate_steps(h2_ref, w13_ref, act_ref):
    for n0 in range(0, D_FF, MXU_N):
        gg = _bdot(h2_ref[...], w13_ref[:, n0:n0 + MXU_N])
        gg = gg * _sigmoid(gg)
        yield
        uu = _bdot(h2_ref[...], w13_ref[:, D_FF + n0:D_FF + n0 + MXU_N])
        act_ref[:, n0:n0 + MXU_N] = (gg * uu).astype(BF16)
        yield


def _ffn_out_steps(x1_ref, act_ref, ga2, w2_ref, gfin_ref, y_ref, *, rows, row_chunk, final):
    dst = x1_ref if final else y_ref
    for n0 in range(0, D_MODEL, MXU_N):
        cs = slice(n0, n0 + MXU_N)
        ff = _bdot(act_ref[...], w2_ref[:, cs])
        dst[:, cs] = x1_ref[:, cs] + ga2(cs) * ff
        yield
    if final:
        for r in _row_slices(rows, row_chunk):
            xb = x1_ref[r, :]
            ms = jnp.mean(xb * xb, axis=-1, keepdims=True)
            y_ref[r, :] = xb * lax.rsqrt(ms + EPS) * gfin_ref[...]
            yield


def _run(*gens):
    live = [g if isinstance(g, tuple) else (g, 1) for g in gens]
    while live:
        for entry in list(live):
            g, k = entry
            for _ in range(k):
                try:
                    next(g)
                except StopIteration:
                    live.remove(entry)
                    break


def _layernorm_silu(z, lng, lnb):
    mu = jnp.mean(z, axis=-1, keepdims=True)
    zc = z - mu
    var = jnp.mean(zc * zc, axis=-1, keepdims=True)
    zn = zc * lax.rsqrt(var + EPS) * lng + lnb
    return zn * _sigmoid(zn)


_MOD_ROW = {"sh1": 0, "sc1": 1, "ga1": 2, "sh2": 3, "sc2": 4, "ga2": 5}


def _cumsum_chunk(tril_bf, x):
    hi = x.astype(BF16)
    r1 = x - hi.astype(F32)
    mid = r1.astype(BF16)
    lo = (r1 - mid.astype(F32)).astype(BF16)
    return _bdot(tril_bf, hi) + _bdot(tril_bf, mid) + _bdot(tril_bf, lo)


def _mlstm_chunk(p_ref, r0, bg_ref, ghead_ref, mix_ref, C_ref, n_ref, m_ref, tril, tril_bf):
    rs = slice(r0, r0 + CHUNK)
    gates = p_ref[rs, COL_G:COL_G + LANES] + bg_ref[...]
    fcum = _cumsum_chunk(tril_bf, _log_sigmoid(gates))
    gates_t = gates.T
    fcum_t = fcum.T
    yield
    heads = range(HEADS)
    col = lambda base, h: slice(base + h * HEAD_DIM, base + (h + 1) * HEAD_DIM)
    st = [dict() for _ in heads]
    for h in heads:
        d = st[h]
        d["q"] = p_ref[rs, col(COL_Q, h)]
        d["k"] = p_ref[rs, col(COL_K, h)] * (HEAD_DIM ** -0.5)
        d["v_bf"] = p_ref[rs, col(COL_V, h)].astype(BF16)
        d["q_bf"] = d["q"].astype(BF16)
        d["f_col"] = fcum[:, HEADS + h:HEADS + h + 1]
        i_col = gates[:, h:h + 1]
        f_row = fcum_t[HEADS + h:HEADS + h + 1, :]
        i_row = gates_t[h:h + 1, :]
        g = f_row[:, CHUNK - 1:CHUNK]
        d["g"] = g
        d["dlog"] = jnp.where(tril, d["f_col"] - f_row + i_row, -jnp.inf)
        d["dmax"] = jnp.max(d["dlog"], axis=-1, keepdims=True)
        d["wlog"] = g - d["f_col"] + i_col
        d["wmax"] = jnp.max(g - f_row + i_row, axis=-1, keepdims=True)
        d["qk"] = _bdot_nt(d["q_bf"], d["k"].astype(BF16))
        yield
    for h in heads:
        d = st[h]
        m_prev = m_ref[h:h + 1, 0:1]
        inter = d["f_col"] + m_prev
        m_t = jnp.maximum(inter, d["dmax"])
        d["a_inter"] = jnp.exp(inter - m_t)
        d["ebound"] = jnp.exp(-m_t)
        s = d["qk"] * jnp.exp(d["dlog"] - m_t)
        d["ssum"] = jnp.sum(s, axis=-1, keepdims=True)
        d["sv"] = _bdot(s.astype(BF16), d["v_bf"])
        m_new = jnp.maximum(d["g"] + m_prev, d["wmax"])
        d["decay"] = jnp.exp(d["g"] + m_prev - m_new)
        kw = d["k"] * jnp.exp(d["wlog"] - m_new)
        d["kwsum"] = jnp.sum(kw, axis=0, keepdims=True)
        d["kv"] = _bdot(kw.T.astype(BF16), d["v_bf"])
        m_ref[h:h + 1, :] = jnp.broadcast_to(m_new, (1, LANES))
        yield
    for h in heads:
        d = st[h]
        n_row = n_ref[h:h + 1, :]
        c_mat = C_ref[h]
        num = d["sv"] + _bdot(d["q_bf"], c_mat.astype(BF16)) * d["a_inter"]
        qn = jnp.sum(d["q"] * n_row, axis=-1, keepdims=True)
        den = d["ssum"] + d["a_inter"] * qn
        bound = jnp.maximum(jnp.abs(den), d["ebound"])
        hh = num * (1.0 / bound)
        C_ref[h] = d["decay"] * c_mat + d["kv"]
        n_ref[h:h + 1, :] = d["decay"] * n_row + d["kwsum"]
        hn = hh * lax.rsqrt(jnp.mean(hh * hh, axis=-1, keepdims=True) + EPS) * ghead_ref[:, col(0, h)]
        mix_ref[rs, col(0, h)] = (hn * _sigmoid(p_ref[rs, col(COL_O, h)])).astype(BF16)
        yield


def _prompt_layer_kernel(x_ref, mod1_ref, mod2_ref, gmix_ref, wqkvo_ref, wglu_ref, wgate_ref, bg_ref,
                         ghead_ref, wdw_ref, bdw_ref, lng_ref, lnb_ref, wout_ref, gffn_ref, w13_ref,
                         w2_ref, gfin_ref,
                         y_ref, c_out, n_out, m_out, conv_out,
                         h_ref, p_ref, mix_ref, xres_ref, x1_ref, h2_ref, ga2_ref, act_ref, ubuf_ref,
                         zc_ref, C_ref, n_ref, m_ref, *, tm, tiles_per_seq, final):
    s = pl.program_id(0)
    n_tiles = pl.num_programs(0) - 2
    j = jnp.clip(s - 1, 0, n_tiles - 1) % tiles_per_seq
    last = tiles_per_seq - 1
    row_chunk = 32

    @pl.when(s == 0)
    def _():
        for ref in (p_ref, xres_ref, x1_ref, h2_ref, ga2_ref):
            ref[...] = jnp.zeros_like(ref)

    @pl.when(j == 0)
    def _():
        C_ref[...] = jnp.zeros_like(C_ref)
        n_ref[...] = jnp.zeros_like(n_ref)
        m_ref[...] = jnp.zeros_like(m_ref)
        ubuf_ref[:, 0:HIST_PAD, :] = jnp.zeros((CONV_SLABS, HIST_PAD, LANES), F32)

    @pl.when(j > 0)
    def _():
        ubuf_ref[:, 0:HIST_PAD, :] = ubuf_ref[:, tm:tm + HIST_PAD, :]

    def mod1(name, cs, r=None):
        i = _MOD_ROW[name]
        return mod1_ref[i:i + 1, cs]

    def mod2(name, cs, r=None):
        i = _MOD_ROW[name]
        return mod2_ref[i:i + 1, cs]

    row_id = lax.broadcasted_iota(jnp.int32, (CHUNK, CHUNK), 0)
    col_id = lax.broadcasted_iota(jnp.int32, (CHUNK, CHUNK), 1)
    tril = row_id >= col_id
    tril_bf = jnp.where(tril, 1.0, 0.0).astype(BF16)

    def mlstm_steps():
        for c in range(tm // CHUNK):
            yield from _mlstm_chunk(p_ref, c * CHUNK, bg_ref, ghead_ref, mix_ref, C_ref, n_ref,
                                    m_ref, tril, tril_bf)

    def glu_steps():
        for r in _row_slices(tm, row_chunk):
            ro = slice(r.start + HIST_PAD, r.stop + HIST_PAD)
            u = p_ref[r, COL_A:COL_A + CONV_CH] * _sigmoid(p_ref[r, COL_B:COL_B + CONV_CH])
            for g in range(CONV_SLABS):
                ubuf_ref[g, ro, :] = u[:, g * LANES:(g + 1) * LANES]
            yield

    def conv_steps(slabs):
        n_sets = row_chunk // SUBLANES
        firsts = [(t0 // CONV_STRIDE) * CONV_STRIDE * SUBLANES + t0 % CONV_STRIDE
                  for t0 in range(n_sets)]
        for g in slabs:
            gs = slice(g * LANES, (g + 1) * LANES)
            bias = jnp.broadcast_to(bdw_ref[:, gs], (SUBLANES, LANES))
            for base in range(0, tm, row_chunk):
                acc = [bias for _ in range(n_sets)]
                for w in range(CONV_WIDTH):
                    tap = wdw_ref[w:w + 1, gs]
                    for t0 in range(n_sets):
                        src = pl.ds(base + firsts[t0] + HIST_PAD - HIST + w, SUBLANES,
                                    stride=CONV_STRIDE)
                        acc[t0] = acc[t0] + ubuf_ref[g, src, :] * tap
                for t0 in range(n_sets):
                    zc_ref[g, pl.ds(base + firsts[t0], SUBLANES, stride=CONV_STRIDE), :] = acc[t0]
                yield

    def conv_norm_steps():
        for r in _row_slices(tm, row_chunk):
            z = jnp.concatenate([zc_ref[g, r, :] for g in range(CONV_SLABS)], axis=-1)
            mix_ref[r, MLSTM_WIDTH:] = _layernorm_silu(z, lng_ref[...], lnb_ref[...]).astype(BF16)
            yield

    def keep_steps():
        xres_ref[...] = x_ref[...]
        ga2_ref[...] = mod2("ga2", slice(None))
        yield

    _run(mlstm_steps(), _ffn_gate_steps(h2_ref, w13_ref, act_ref),
         (itertools.chain(glu_steps(), conv_steps(range(CONV_SLABS))), 2),
         _norm_steps(x_ref, h_ref, gmix_ref, mod1, "sc1", "sh1", rows=tm, row_chunk=row_chunk))

    @pl.when(jnp.logical_and(j == last, jnp.logical_and(s >= 1, s <= n_tiles)))
    def _():
        c_out[...] = C_ref[...]
        n_out[...] = n_ref[0:HEADS, :]
        m_out[...] = m_ref[0:HEADS, :]
        for g in range(CONV_SLABS):
            conv_out[:, g * LANES:(g + 1) * LANES] = ubuf_ref[g, tm + HIST_PAD - HIST:tm + HIST_PAD, :]

    @pl.when(s >= 0)
    def _():
        _run(_ffn_out_steps(x1_ref, act_ref, lambda cs: ga2_ref[:, cs], w2_ref, gfin_ref, y_ref,
                            rows=tm, row_chunk=row_chunk, final=final),
             (conv_norm_steps(), 2))

    @pl.when(s >= 0)
    def _():
        _run(_out_proj_steps(xres_ref, mix_ref, mod2, wout_ref, x1_ref))
        _run(_in_proj_steps(h_ref, (wqkvo_ref, wglu_ref, wgate_ref), p_ref),
             _norm_steps(x1_ref, h2_ref, gffn_ref, mod2, "sc2", "sh2", rows=tm, row_chunk=row_chunk))
        _run(keep_steps())


def _resident(shape):
    nd = len(shape)
    return pl.BlockSpec(shape, lambda *g: (0,) * nd, pipeline_mode=pl.Buffered(1))


def _layer_resident(layer, shape):
    nd = len(shape)
    return pl.BlockSpec((None,) + shape, lambda *g: (layer,) + (0,) * nd,
                        pipeline_mode=pl.Buffered(1))


def _prompt_layer(x, mod, layer, wl, g_final, *, tm, final):
    bsz, seq, d = x.shape
    lw = functools.partial(_layer_resident, layer)
    assert seq % tm == 0 and tm % CHUNK == 0 and tm >= HIST_PAD
    nt = seq // tm
    n_tiles = bsz * nt
    kern = functools.partial(_prompt_layer_kernel, tm=tm, tiles_per_seq=nt, final=final)

    def tile(s, lag):
        return jnp.clip(s - lag, 0, n_tiles - 1)

    def cur(s):
        return tile(s, 1)

    def prev(s):
        return tile(s, 2)

    in_specs = [
        pl.BlockSpec((None, tm, d), lambda s: (tile(s, 0) // nt, tile(s, 0) % nt, 0)),
        pl.BlockSpec((None, 6, d), lambda s: (tile(s, 0) // nt, 0, 0)),
        pl.BlockSpec((None, 6, d), lambda s: (cur(s) // nt, 0, 0)),
        lw((1, d)),
        lw((d, COL_A)),
        lw((d, D_OUT_PAD)),
        lw((d, LANES)),
        lw((1, LANES)),
        lw((1, MLSTM_WIDTH)),
        lw((CONV_WIDTH, CONV_CH)),
        lw((1, CONV_CH)),
        lw((1, CONV_CH)),
        lw((1, CONV_CH)),
        lw((d, D_OUT_PAD)),
        lw((1, d)),
        lw((d, 2 * D_FF)),
        lw((D_FF, D_OUT_PAD)),
        _resident((1, d)),
    ]
    out_specs = [
        pl.BlockSpec((None, tm, d), lambda s: (prev(s) // nt, prev(s) % nt, 0)),
        pl.BlockSpec((None, HEADS, HEAD_DIM, HEAD_DIM), lambda s: (cur(s) // nt, 0, 0, 0)),
        pl.BlockSpec((None, HEADS, HEAD_DIM), lambda s: (cur(s) // nt, 0, 0)),
        pl.BlockSpec((None, HEADS, LANES), lambda s: (cur(s) // nt, 0, 0)),
        pl.BlockSpec((None, HIST, CONV_CH), lambda s: (cur(s) // nt, 0, 0)),
    ]
    out_shape = [
        jax.ShapeDtypeStruct((bsz, seq, d), F32),
        jax.ShapeDtypeStruct((bsz, HEADS, HEAD_DIM, HEAD_DIM), F32),
        jax.ShapeDtypeStruct((bsz, HEADS, HEAD_DIM), F32),
        jax.ShapeDtypeStruct((bsz, HEADS, LANES), F32),
        jax.ShapeDtypeStruct((bsz, HIST, CONV_CH), F32),
    ]
    scratch = [
        pltpu.VMEM((tm, d), BF16),
        pltpu.VMEM((tm, PROJ_PAD), F32),
        pltpu.VMEM((tm, d), BF16),
        pltpu.VMEM((tm, d), F32),
        pltpu.VMEM((tm, d), F32),
        pltpu.VMEM((tm, d), BF16),
        pltpu.VMEM((1, d), F32),
        pltpu.VMEM((tm, D_FF), BF16),
        pltpu.VMEM((CONV_SLABS, tm + HIST_PAD, LANES), F32),
        pltpu.VMEM((CONV_SLABS, tm, LANES), F32),
        pltpu.VMEM((HEADS, HEAD_DIM, HEAD_DIM), F32),
        pltpu.VMEM((SUBLANES, HEAD_DIM), F32),
        pltpu.VMEM((SUBLANES, LANES), F32),
    ]
    return pl.pallas_call(
        kern,
        grid=(n_tiles + 2,),
        in_specs=in_specs,
        out_specs=out_specs,
        out_shape=out_shape,
        scratch_shapes=scratch,
        compiler_params=pltpu.CompilerParams(
            dimension_semantics=("arbitrary",), vmem_limit_bytes=VMEM_LIMIT),
        name="prompt_layer",
    )(x, mod, mod, wl["g_mix"], wl["w_qkvo"], wl["w_glu"], wl["w_gate"], wl["b_gates"], wl["g_head"], wl["w_dw"], wl["b_dw"],
      wl["ln_g"], wl["ln_b"], wl["w_out"], wl["g_ffn"], wl["w_ffn13"], wl["w_ffn2"], g_final)


def _sample_in_kernel(x_ref, mod_ref, gmix_ref, wqkvo_ref, wglu_ref, wgate_ref, p_ref, h_ref):
    h = _rms_mod(x_ref[...], gmix_ref[...], mod_ref[:, D_MODEL:2 * D_MODEL], mod_ref[:, 0:D_MODEL])
    h_ref[...] = h.astype(BF16)
    _run(_in_proj_steps(h_ref, (wqkvo_ref, wglu_ref, wgate_ref), p_ref))


def _sample_in_proj(x, mod, layer, wl):
    rows, d = x.shape
    lw = functools.partial(_layer_resident, layer)
    return pl.pallas_call(
        _sample_in_kernel,
        grid=(1,),
        in_specs=[_resident((rows, d)), _resident(mod.shape), lw((1, d)), lw((d, COL_A)),
                  lw((d, D_OUT_PAD)), lw((d, LANES))],
        out_specs=pl.BlockSpec((rows, PROJ_PAD), lambda i: (0, 0)),
        out_shape=jax.ShapeDtypeStruct((rows, PROJ_PAD), F32),
        scratch_shapes=[pltpu.VMEM((rows, D_MODEL), BF16)],
        compiler_params=pltpu.CompilerParams(
            dimension_semantics=("arbitrary",), vmem_limit_bytes=VMEM_LIMIT),
        name="sample_in_proj",
    )(x, mod, wl["g_mix"], wl["w_qkvo"], wl["w_glu"], wl["w_gate"])


def _sample_step_kernel(p_ref, c_ref, n_ref, m_ref, conv_ref, bg_ref, ghead_ref, wdw_ref, bdw_ref,
                        lng_ref, lnb_ref, *rest, bt, n_alias, fill_other):
    hm_ref, zc_ref, c_out, n_out, m_out, conv_out = rest[n_alias:]
    slot = pl.program_id(0)

    @pl.when(slot == 0)
    def _():
        scale = HEAD_DIM ** -0.5
        gates = p_ref[:, COL_G:COL_G + LANES] + bg_ref[...]
        logf = pltpu.roll(_log_sigmoid(gates), LANES - HEADS, axis=1)
        inter = logf + m_ref[...]
        m_t = jnp.maximum(inter, gates)
        dw_all = jnp.exp(gates - m_t)
        a_all = jnp.exp(inter - m_t)
        eb_all = jnp.exp(-m_t)
        m_out[...] = m_t
        for h in range(HEADS):
            hs = slice(h * HEAD_DIM, (h + 1) * HEAD_DIM)
            q = p_ref[:, COL_Q + h * HEAD_DIM:COL_Q + (h + 1) * HEAD_DIM]
            k = p_ref[:, COL_K + h * HEAD_DIM:COL_K + (h + 1) * HEAD_DIM] * scale
            v = p_ref[:, COL_V + h * HEAD_DIM:COL_V + (h + 1) * HEAD_DIM]
            o = p_ref[:, COL_O + h * HEAD_DIM:COL_O + (h + 1) * HEAD_DIM]
            dw = dw_all[:, h:h + 1]
            a_in = a_all[:, h:h + 1]
            n_h = n_ref[:, hs]
            kw = k * dw
            s = jnp.sum(q * k, axis=-1, keepdims=True) * dw
            qn = jnp.sum(q * n_h, axis=-1, keepdims=True)
            n_out[:, hs] = a_in * n_h + kw
            q_t = q.T
            pad_rows = jnp.zeros((HEAD_DIM - bt, HEAD_DIM), F32)
            kw_t = jnp.concatenate([kw, pad_rows], axis=0).T.astype(BF16)
            v_wide = jnp.concatenate([v] * bt, axis=1)
            own = (lax.broadcasted_iota(jnp.int32, v_wide.shape, 1) // HEAD_DIM
                   == lax.broadcasted_iota(jnp.int32, v_wide.shape, 0))
            v_bd = jnp.concatenate([jnp.where(own, v_wide, 0.0),
                                    jnp.zeros((HEAD_DIM - bt, bt * HEAD_DIM), F32)], axis=0)
            outer = _bdot(kw_t, v_bd.astype(BF16))
            a_rows = jnp.broadcast_to(a_in, (bt, HEAD_DIM))
            qc_rows = []
            for b in range(bt):
                c_mat = c_ref[b, h]
                qc_rows.append(jnp.sum(q_t[:, b:b + 1] * c_mat, axis=0, keepdims=True))
                c_out[b, h] = a_rows[b:b + 1, :] * c_mat + outer[:, b * HEAD_DIM:(b + 1) * HEAD_DIM]
            qc = jnp.concatenate(qc_rows, axis=0)
            num = s * v + qc * a_in
            den = s + a_in * qn
            bound = jnp.maximum(jnp.abs(den), eb_all[:, h:h + 1])
            hh = num * (1.0 / bound)
            hn = hh * lax.rsqrt(jnp.mean(hh * hh, axis=-1, keepdims=True) + EPS) * ghead_ref[:, hs]
            hm_ref[:, hs] = hn * _sigmoid(o)

        u = p_ref[:, COL_A:COL_A + CONV_CH] * _sigmoid(p_ref[:, COL_B:COL_B + CONV_CH])
        z = u * wdw_ref[HIST:HIST + 1, :] + bdw_ref[...]
        for w in range(HIST):
            z = z + conv_ref[w] * wdw_ref[w:w + 1, :]
        zc_ref[...] = _layernorm_silu(z, lng_ref[...], lnb_ref[...])
        conv_out[0:HIST - 1] = conv_ref[1:HIST]
        conv_out[HIST - 1] = u

    if fill_other:
        @pl.when(slot != 0)
        def _():
            c_out[...] = jnp.zeros_like(c_out)
            conv_out[...] = jnp.zeros_like(conv_out)


def _sample_step(p, layer, state_c, state_n2, state_m_pad, state_conv, stacked, wl, *, bt=8):
    rows = p.shape[0]
    depth = state_c.shape[0]
    assert rows % bt == 0
    nblk = rows // bt
    first = stacked is None
    n_slots = depth if first else 1

    def blk_i(slot, i):
        return jnp.where(slot == 0, i, nblk - 1)

    def lay(*s):
        return pl.BlockSpec((None, bt) + s, lambda slot, i: (layer, blk_i(slot, i)) + (0,) * len(s))

    def row(*s):
        return pl.BlockSpec((bt,) + s, lambda slot, i: (blk_i(slot, i),) + (0,) * len(s))

    def stk(*s):
        return pl.BlockSpec((None, bt) + s,
                            lambda slot, i: ((slot if first else layer), i) + (0,) * len(s))

    conv_in = pl.BlockSpec((None, HIST, bt, CONV_CH),
                           lambda slot, i: (layer, 0, blk_i(slot, i), 0))
    conv_stk = pl.BlockSpec((None, HIST, bt, CONV_CH),
                            lambda slot, i: ((slot if first else layer), 0, i, 0))
    lw = functools.partial(_layer_resident, layer)
    in_specs = [
        row(PROJ_PAD), lay(HEADS, HEAD_DIM, HEAD_DIM), lay(MLSTM_WIDTH), lay(LANES), conv_in,
        lw((1, LANES)), lw((1, MLSTM_WIDTH)), lw((CONV_WIDTH, CONV_CH)),
        lw((1, CONV_CH)), lw((1, CONV_CH)), lw((1, CONV_CH)),
    ]
    args = [p, state_c, state_n2, state_m_pad, state_conv, wl["b_gates"], wl["g_head"], wl["w_dw"],
            wl["b_dw"], wl["ln_g"], wl["ln_b"]]
    aliases = {}
    if not first:
        aliases = {len(args): 2, len(args) + 1: 5}
        in_specs += [pl.BlockSpec(memory_space=pl.ANY), pl.BlockSpec(memory_space=pl.ANY)]
        args += list(stacked)
    out_shape = [
        jax.ShapeDtypeStruct((rows, MLSTM_WIDTH), F32),
        jax.ShapeDtypeStruct((rows, CONV_CH), F32),
        jax.ShapeDtypeStruct(state_c.shape, F32),
        jax.ShapeDtypeStruct((rows, MLSTM_WIDTH), F32),
        jax.ShapeDtypeStruct((rows, LANES), F32),
        jax.ShapeDtypeStruct(state_conv.shape, F32),
    ]
    hm, zc, c_new, n_new, m_new, conv_new = pl.pallas_call(
        functools.partial(_sample_step_kernel, bt=bt, n_alias=len(aliases),
                          fill_other=first and depth > 1),
        grid=(n_slots, nblk),
        in_specs=in_specs,
        out_specs=[row(MLSTM_WIDTH), row(CONV_CH), stk(HEADS, HEAD_DIM, HEAD_DIM),
                   row(MLSTM_WIDTH), row(LANES), conv_stk],
        out_shape=out_shape,
        input_output_aliases=aliases,
        compiler_params=pltpu.CompilerParams(
            dimension_semantics=("arbitrary", "arbitrary"), vmem_limit_bytes=VMEM_LIMIT),
        name="sample_step",
    )(*args)
    return hm, zc, (c_new, conv_new), n_new.reshape(rows, HEADS, HEAD_DIM), m_new[:, :HEADS]


def _sample_out_kernel(x_ref, hm_ref, zc_ref, mod_ref, wout_ref, gffn_ref, w13_ref, w2_ref,
                       gfin_ref, y_ref, mix_ref, x1_ref, h2_ref, act_ref, *, rows, final):
    mix_ref[:, 0:MLSTM_WIDTH] = hm_ref[...].astype(BF16)
    mix_ref[:, MLSTM_WIDTH:] = zc_ref[...].astype(BF16)

    def mod_rows(name, cs, r=slice(None)):
        i = _MOD_ROW[name]
        if isinstance(cs, slice) and cs == slice(None):
            cs = slice(0, D_MODEL)
        return mod_ref[r, i * D_MODEL + cs.start:i * D_MODEL + cs.stop]

    _run(itertools.chain(
        _out_proj_steps(x_ref, mix_ref, mod_rows, wout_ref, x1_ref),
        _norm_steps(x1_ref, h2_ref, gffn_ref, mod_rows, "sc2", "sh2", rows=rows, row_chunk=32),
        _ffn_gate_steps(h2_ref, w13_ref, act_ref),
        _ffn_out_steps(x1_ref, act_ref, lambda cs: mod_rows("ga2", cs), w2_ref, gfin_ref, y_ref,
                       rows=rows, row_chunk=32, final=final)))


def _sample_out(x, hm, zc, mod, layer, wl, g_final, *, final):
    rows, d = x.shape
    lw = functools.partial(_layer_resident, layer)
    return pl.pallas_call(
        functools.partial(_sample_out_kernel, rows=rows, final=final),
        grid=(1,),
        in_specs=[_resident((rows, d)), _resident(hm.shape), _resident(zc.shape),
                  _resident(mod.shape), lw((d, D_OUT_PAD)), lw((1, d)), lw((d, 2 * D_FF)),
                  lw((D_FF, D_OUT_PAD)),
                  _resident((1, d))],
        out_specs=pl.BlockSpec((rows, d), lambda i: (0, 0)),
        out_shape=jax.ShapeDtypeStruct((rows, D_MODEL), F32),
        scratch_shapes=[
            pltpu.VMEM((rows, D_MODEL), BF16),
            pltpu.VMEM((rows, D_MODEL), F32),
            pltpu.VMEM((rows, D_MODEL), BF16),
            pltpu.VMEM((rows, D_FF), BF16),
        ],
        compiler_params=pltpu.CompilerParams(
            dimension_semantics=("arbitrary",), vmem_limit_bytes=VMEM_LIMIT),
        name="sample_out",
    )(x, hm, zc, mod, wl["w_out"], wl["g_ffn"], wl["w_ffn13"], wl["w_ffn2"], g_final)


def _prep_weights(g_mix, w_in, b_gates, g_head, w_dw, b_dw, ln_g, ln_b, w_out, g_ffn, w_ffn13,
                  w_ffn2):
    depth, d, _ = w_in.shape
    n_gate = 2 * HEADS
    gate0 = 4 * MLSTM_WIDTH
    lane_pad = lambda w, n: jnp.pad(w.astype(BF16), ((0, 0), (0, 0), (0, n - w.shape[2])))
    return {
        "g_mix": g_mix.reshape(depth, 1, d),
        "w_qkvo": w_in[:, :, :gate0].astype(BF16),
        "w_glu": lane_pad(w_in[:, :, gate0 + n_gate:], D_OUT_PAD),
        "w_gate": lane_pad(w_in[:, :, gate0:gate0 + n_gate], LANES),
        "b_gates": jnp.pad(b_gates, ((0, 0), (0, LANES - n_gate))).reshape(depth, 1, LANES),
        "g_head": g_head.reshape(depth, 1, MLSTM_WIDTH),
        "w_dw": w_dw,
        "b_dw": b_dw.reshape(depth, 1, CONV_CH),
        "ln_g": ln_g.reshape(depth, 1, CONV_CH),
        "ln_b": ln_b.reshape(depth, 1, CONV_CH),
        "w_out": jnp.pad(w_out.astype(BF16), ((0, 0), (0, 0), (0, D_OUT_PAD - d))),
        "g_ffn": g_ffn.reshape(depth, 1, d),
        "w_ffn13": w_ffn13.astype(BF16),
        "w_ffn2": jnp.pad(w_ffn2.astype(BF16), ((0, 0), (0, 0), (0, D_OUT_PAD - d))),
    }


def _forward(x_prompt, x_sample, c_prompt, c_sample, state_C, state_n, state_m, state_conv,
             w_ada, b_ada, g_mix, w_in, b_gates, g_head, w_dw, b_dw, ln_g, ln_b, w_out,
             g_ffn, w_ffn13, w_ffn2, g_final, *, tm):
    depth = w_ada.shape[0]
    bp = x_prompt.shape[0]
    d = x_prompt.shape[-1]
    mod = _modulation(jnp.concatenate([c_prompt, c_sample], axis=0), w_ada, b_ada)
    gfin = g_final.reshape(1, d)
    xp = x_prompt
    xs = x_sample.reshape(x_sample.shape[0], d)
    outs_p = [[], [], [], []]
    ns_l, ms_l = [], []
    stacked = None
    n_seq = xs.shape[0]
    state_n2 = state_n.reshape(depth, n_seq, MLSTM_WIDTH)
    state_m_pad = jnp.pad(state_m, ((0, 0), (0, 0), (0, LANES - HEADS)))
    state_conv_t = jnp.swapaxes(state_conv, 1, 2)
    wl = _prep_weights(g_mix, w_in, b_gates, g_head, w_dw, b_dw, ln_g, ln_b, w_out, g_ffn,
                       w_ffn13, w_ffn2)
    for l in range(depth):
        final = l == depth - 1
        mod_p = mod[l, :bp].reshape(bp, 6, d)
        mod_s = mod[l, bp:]
        xp, c1, n1, m1, b1 = _prompt_layer(xp, mod_p, l, wl, gfin, tm=tm, final=final)
        for acc, val in zip(outs_p, (c1, n1, m1[:, :, 0], b1)):
            acc.append(val)
        p = _sample_in_proj(xs, mod_s, l, wl)
        hm, zc, stacked, n2, m2 = _sample_step(p, l, state_C, state_n2, state_m_pad, state_conv_t,
                                               stacked, wl)
        xs = _sample_out(xs, hm, zc, mod_s, l, wl, gfin, final=final)
        ns_l.append(n2)
        ms_l.append(m2)
    y_sample = xs.reshape(x_sample.shape)
    c_sample_new, conv_t_new = stacked
    return ((xp, y_sample) + tuple(jnp.stack(a) for a in outs_p)
            + (c_sample_new, jnp.stack(ns_l), jnp.stack(ms_l), jnp.swapaxes(conv_t_new, 1, 2)))


def kernel(x_prompt, x_sample, c_prompt, c_sample, state_C, state_n, state_m, state_conv, w_ada, b_ada, g_mix, w_in, b_gates, g_head, w_dw, b_dw, ln_g, ln_b, w_out, g_ffn, w_ffn13, w_ffn2, g_final):
    return _forward(x_prompt, x_sample, c_prompt, c_sample, state_C, state_n, state_m, state_conv,
                    w_ada, b_ada, g_mix, w_in, b_gates, g_head, w_dw, b_dw, ln_g, ln_b, w_out,
                    g_ffn, w_ffn13, w_ffn2, g_final, tm=256)
```

```python
import functools
import itertools

import jax
import jax.numpy as jnp
from jax import lax
from jax.experimental import pallas as pl
from jax.experimental.pallas import tpu as pltpu

F32 = jnp.float32
BF16 = jnp.bfloat16

D_MODEL = 1024
HEADS = 4
HEAD_DIM = 128
MLSTM_WIDTH = HEADS * HEAD_DIM
CONV_CH = D_MODEL - MLSTM_WIDTH
CONV_WIDTH = 31
HIST = CONV_WIDTH - 1
D_FF = 2816
CHUNK = 128
EPS = 1e-6

LANES = 128
SUBLANES = 8
MXU_N = 256

COL_Q = 0
COL_K = MLSTM_WIDTH
COL_V = 2 * MLSTM_WIDTH
COL_O = 3 * MLSTM_WIDTH
COL_A = 4 * MLSTM_WIDTH
COL_B = COL_A + CONV_CH
COL_G = COL_B + CONV_CH
PROJ_PAD = COL_G + LANES
HIST_PAD = 32
CONV_SLABS = CONV_CH // LANES
CONV_STRIDE = 2
D_OUT_PAD = D_MODEL + LANES
VMEM_LIMIT = 60 * 1024 * 1024


def _log_sigmoid(x):
    return jnp.minimum(x, 0.0) - jnp.log1p(jnp.exp(-jnp.abs(x)))


def _sigmoid(x):
    return 1.0 / (1.0 + jnp.exp(-x))


def _bdot(a, b):
    return jnp.dot(a, b, preferred_element_type=F32)


def _bdot_nt(a, b):
    return lax.dot_general(a, b, (((1,), (1,)), ((), ())), preferred_element_type=F32)


def _row_slices(rows, chunk):
    chunk = min(chunk, rows)
    assert rows % chunk == 0
    return [slice(r0, r0 + chunk) for r0 in range(0, rows, chunk)]


def _rms_mod(x, g, sc, sh):
    ms = jnp.mean(x * x, axis=-1, keepdims=True)
    return (x * lax.rsqrt(ms + EPS) * g) * (1.0 + sc) + sh


def _mod_kernel(c_ref, w_ref, b_ref, o_ref):
    o_ref[...] = _bdot(c_ref[...].astype(BF16), w_ref[...].astype(BF16)) + b_ref[...]


def _modulation(c_all, w_ada, b_ada):
    depth, d, n6 = w_ada.shape
    rows = c_all.shape[0]
    tn = 1024
    return pl.pallas_call(
        _mod_kernel,
        grid=(depth, n6 // tn),
        in_specs=[
            pl.BlockSpec((rows, d), lambda l, n: (0, 0)),
            pl.BlockSpec((None, d, tn), lambda l, n: (l, 0, n)),
            pl.BlockSpec((None, 1, tn), lambda l, n: (l, 0, n)),
        ],
        out_specs=pl.BlockSpec((None, rows, tn), lambda l, n: (l, 0, n)),
        out_shape=jax.ShapeDtypeStruct((depth, rows, n6), F32),
        compiler_params=pltpu.CompilerParams(
            dimension_semantics=("arbitrary", "arbitrary"), vmem_limit_bytes=VMEM_LIMIT),
        name="adaln_mod",
    )(c_all, w_ada, b_ada.reshape(depth, 1, n6))


def _norm_steps(src_ref, dst_ref, g_ref, mod, sc, sh, *, rows, row_chunk):
    for r in _row_slices(rows, row_chunk):
        h = _rms_mod(src_ref[r, :], g_ref[...], mod(sc, slice(None), r), mod(sh, slice(None), r))
        dst_ref[r, :] = h.astype(BF16)
        yield


def _in_proj_steps(h_ref, w_refs, p_ref):
    for w_ref, col0, width in zip(w_refs, (COL_Q, COL_A, COL_G), (COL_A, 2 * CONV_CH, LANES)):
        for n0 in range(0, width, MXU_N):
            n1 = min(n0 + MXU_N, width)
            p_ref[:, col0 + n0:col0 + n1] = _bdot(h_ref[...], w_ref[:, n0:n1])
            yield


def _out_proj_steps(x_ref, mix_ref, mod, wout_ref, x1_ref):
    for n0 in range(0, D_MODEL, MXU_N):
        cs = slice(n0, n0 + MXU_N)
        mx = _bdot(mix_ref[...], wout_ref[:, cs])
        x1_ref[:, cs] = x_ref[:, cs] + mod("ga1", cs, slice(None)) * mx
        yield


def _ffn_gate_steps(h2_ref, w13_ref, act_ref):
    for n0 in range(0, D_FF, MXU_N):
        gg = _bdot(h2_ref[...], w13_ref[:, n0:n0 + MXU_N])
        gg = gg * _sigmoid(gg)
        yield
        uu = _bdot(h2_ref[...], w13_ref[:, D_FF + n0:D_FF + n0 + MXU_N])
        act_ref[:, n0:n0 + MXU_N] = (gg * uu).astype(BF16)
        yield


def _ffn_out_steps(x1_ref, act_ref, ga2, w2_ref, gfin_ref, y_ref, *, rows, row_chunk, final):
    dst = x1_ref if final else y_ref
    for n0 in range(0, D_MODEL, MXU_N):
        cs = slice(n0, n0 + MXU_N)
        ff = _bdot(act_ref[...], w2_ref[:, cs])
        dst[:, cs] = x1_ref[:, cs] + ga2(cs) * ff
        yield
    if final:
        for r in _row_slices(rows, row_chunk):
            xb = x1_ref[r, :]
            ms = jnp.mean(xb * xb, axis=-1, keepdims=True)
            y_ref[r, :] = xb * lax.rsqrt(ms + EPS) * gfin_ref[...]
            yield


def _run(*gens):
    live = [g if isinstance(g, tuple) else (g, 1) for g in gens]
    while live:
        for entry in list(live):
            g, k = entry
            for _ in range(k):
                try:
                    next(g)
                except StopIteration:
                    live.remove(entry)
                    break


def _layernorm_silu(z, lng, lnb):
    mu = jnp.mean(z, axis=-1, keepdims=True)
    zc = z - mu
    var = jnp.mean(zc * zc, axis=-1, keepdims=True)
    zn = zc * lax.rsqrt(var + EPS) * lng + lnb
    return zn * _sigmoid(zn)


_MOD_ROW = {"sh1": 0, "sc1": 1, "ga1": 2, "sh2": 3, "sc2": 4, "ga2": 5}


def _cumsum_chunk(tril_bf, x):
    hi = x.astype(BF16)
    r1 = x - hi.astype(F32)
    mid = r1.astype(BF16)
    lo = (r1 - mid.astype(F32)).astype(BF16)
    return _bdot(tril_bf, hi) + _bdot(tril_bf, mid) + _bdot(tril_bf, lo)


def _mlstm_chunk(p_ref, r0, bg_ref, ghead_ref, mix_ref, C_ref, n_ref, m_ref, tril, tril_bf):
    rs = slice(r0, r0 + CHUNK)
    gates = p_ref[rs, COL_G:COL_G + LANES] + bg_ref[...]
    fcum = _cumsum_chunk(tril_bf, _log_sigmoid(gates))
    gates_t = gates.T
    fcum_t = fcum.T
    yield
    heads = range(HEADS)
    col = lambda base, h: slice(base + h * HEAD_DIM, base + (h + 1) * HEAD_DIM)
    st = [dict() for _ in heads]
    for h in heads:
        d = st[h]
        d["q"] = p_ref[rs, col(COL_Q, h)]
        d["k"] = p_ref[rs, col(COL_K, h)] * (HEAD_DIM ** -0.5)
        d["v_bf"] = p_ref[rs, col(COL_V, h)].astype(BF16)
        d["q_bf"] = d["q"].astype(BF16)
        d["f_col"] = fcum[:, HEADS + h:HEADS + h + 1]
        i_col = gates[:, h:h + 1]
        f_row = fcum_t[HEADS + h:HEADS + h + 1, :]
        i_row = gates_t[h:h + 1, :]
        g = f_row[:, CHUNK - 1:CHUNK]
        d["g"] = g
        d["dlog"] = jnp.where(tril, d["f_col"] - f_row + i_row, -jnp.inf)
        d["dmax"] = jnp.max(d["dlog"], axis=-1, keepdims=True)
        d["wlog"] = g - d["f_col"] + i_col
        d["wmax"] = jnp.max(g - f_row + i_row, axis=-1, keepdims=True)
        d["qk"] = _bdot_nt(d["q_bf"], d["k"].astype(BF16))
        yield
    for h in heads:
        d = st[h]
        m_prev = m_ref[h:h + 1, 0:1]
        inter = d["f_col"] + m_prev
        m_t = jnp.maximum(inter, d["dmax"])
        d["a_inter"] = jnp.exp(inter - m_t)
        d["ebound"] = jnp.exp(-m_t)
        s = d["qk"] * jnp.exp(d["dlog"] - m_t)
        d["ssum"] = jnp.sum(s, axis=-1, keepdims=True)
        d["sv"] = _bdot(s.astype(BF16), d["v_bf"])
        m_new = jnp.maximum(d["g"] + m_prev, d["wmax"])
        d["decay"] = jnp.exp(d["g"] + m_prev - m_new)
        kw = d["k"] * jnp.exp(d["wlog"] - m_new)
        d["kwsum"] = jnp.sum(kw, axis=0, keepdims=True)
        d["kv"] = _bdot(kw.T.astype(BF16), d["v_bf"])
        m_ref[h:h + 1, :] = jnp.broadcast_to(m_new, (1, LANES))
        yield
    for h in heads:
        d = st[h]
        n_row = n_ref[h:h + 1, :]
        c_mat = C_ref[h]
        num = d["sv"] + _bdot(d["q_bf"], c_mat.astype(BF16)) * d["a_inter"]
        qn = jnp.sum(d["q"] * n_row, axis=-1, keepdims=True)
        den = d["ssum"] + d["a_inter"] * qn
        bound = jnp.maximum(jnp.abs(den), d["ebound"])
        hh = num * (1.0 / bound)
        C_ref[h] = d["decay"] * c_mat + d["kv"]
        n_ref[h:h + 1, :] = d["decay"] * n_row + d["kwsum"]
        hn = hh * lax.rsqrt(jnp.mean(hh * hh, axis=-1, keepdims=True) + EPS) * ghead_ref[:, col(0, h)]
        mix_ref[rs, col(0, h)] = (hn * _sigmoid(p_ref[rs, col(COL_O, h)])).astype(BF16)
        yield


def _prompt_layer_kernel(x_ref, mod1_ref, mod2_ref, gmix_ref, wqkvo_ref, wglu_ref, wgate_ref, bg_ref,
                         ghead_ref, wdw_ref, bdw_ref, lng_ref, lnb_ref, wout_ref, gffn_ref, w13_ref,
                         w2_ref, gfin_ref,
                         y_ref, c_out, n_out, m_out, conv_out,
                         h_ref, p_ref, mix_ref, xres_ref, x1_ref, h2_ref, ga2_ref, act_ref, ubuf_ref,
                         zc_ref, C_ref, n_ref, m_ref, *, tm, tiles_per_seq, final):
    s = pl.program_id(0)
    n_tiles = pl.num_programs(0) - 2
    j = jnp.clip(s - 1, 0, n_tiles - 1) % tiles_per_seq
    last = tiles_per_seq - 1
    row_chunk = 32

    @pl.when(s == 0)
    def _():
        for ref in (mix_ref, xres_ref):
            ref[...] = jnp.zeros_like(ref)

    @pl.when(j == 0)
    def _():
        C_ref[...] = jnp.zeros_like(C_ref)
        n_ref[...] = jnp.zeros_like(n_ref)
        m_ref[...] = jnp.zeros_like(m_ref)
        ubuf_ref[:, 0:HIST_PAD, :] = jnp.zeros((CONV_SLABS, HIST_PAD, LANES), F32)

    @pl.when(j > 0)
    def _():
        ubuf_ref[:, 0:HIST_PAD, :] = ubuf_ref[:, tm:tm + HIST_PAD, :]

    def mod1(name, cs, r=None):
        i = _MOD_ROW[name]
        return mod1_ref[i:i + 1, cs]

    def mod2(name, cs, r=None):
        i = _MOD_ROW[name]
        return mod2_ref[i:i + 1, cs]

    row_id = lax.broadcasted_iota(jnp.int32, (CHUNK, CHUNK), 0)
    col_id = lax.broadcasted_iota(jnp.int32, (CHUNK, CHUNK), 1)
    tril = row_id >= col_id
    tril_bf = jnp.where(tril, 1.0, 0.0).astype(BF16)

    def mlstm_steps():
        for c in range(tm // CHUNK):
            yield from _mlstm_chunk(p_ref, c * CHUNK, bg_ref, ghead_ref, mix_ref, C_ref, n_ref,
                                    m_ref, tril, tril_bf)

    def glu_steps():
        for r in _row_slices(tm, row_chunk):
            ro = slice(r.start + HIST_PAD, r.stop + HIST_PAD)
            u = p_ref[r, COL_A:COL_A + CONV_CH] * _sigmoid(p_ref[r, COL_B:COL_B + CONV_CH])
            for g in range(CONV_SLABS):
                ubuf_ref[g, ro, :] = u[:, g * LANES:(g + 1) * LANES]
            yield

    def conv_steps(slabs):
        n_sets = row_chunk // SUBLANES
        firsts = [(t0 // CONV_STRIDE) * CONV_STRIDE * SUBLANES + t0 % CONV_STRIDE
                  for t0 in range(n_sets)]
        for g in slabs:
            gs = slice(g * LANES, (g + 1) * LANES)
            bias = jnp.broadcast_to(bdw_ref[:, gs], (SUBLANES, LANES))
            for base in range(0, tm, row_chunk):
                acc = [bias for _ in range(n_sets)]
                for w in range(CONV_WIDTH):
                    tap = wdw_ref[w:w + 1, gs]
                    for t0 in range(n_sets):
                        src = pl.ds(base + firsts[t0] + HIST_PAD - HIST + w, SUBLANES,
                                    stride=CONV_STRIDE)
                        acc[t0] = acc[t0] + ubuf_ref[g, src, :] * tap
                for t0 in range(n_sets):
                    zc_ref[g, pl.ds(base + firsts[t0], SUBLANES, stride=CONV_STRIDE), :] = acc[t0]
                yield

    def conv_norm_steps():
        for r in _row_slices(tm, row_chunk):
            z = jnp.concatenate([zc_ref[g, r, :] for g in range(CONV_SLABS)], axis=-1)
            mix_ref[r, MLSTM_WIDTH:] = _layernorm_silu(z, lng_ref[...], lnb_ref[...]).astype(BF16)
            yield

    def keep_steps():
        xres_ref[...] = x_ref[...]
        ga2_ref[...] = mod2("ga2", slice(None))
        yield

    @pl.when(s == 0)
    def _():
        _run(_norm_steps(x_ref, h_ref, gmix_ref, mod1, "sc1", "sh1", rows=tm, row_chunk=row_chunk))

    @pl.when(s > 0)
    def _():
        _run(mlstm_steps(), _ffn_gate_steps(h2_ref, w13_ref, act_ref),
             (itertools.chain(glu_steps(), conv_steps(range(CONV_SLABS))), 2),
             _norm_steps(x_ref, h_ref, gmix_ref, mod1, "sc1", "sh1", rows=tm, row_chunk=row_chunk))

    @pl.when(jnp.logical_and(j == last, jnp.logical_and(s >= 1, s <= n_tiles)))
    def _():
        c_out[...] = C_ref[...]
        n_out[...] = n_ref[0:HEADS, :]
        m_out[...] = m_ref[0:HEADS, :]
        for g in range(CONV_SLABS):
            conv_out[:, g * LANES:(g + 1) * LANES] = ubuf_ref[g, tm + HIST_PAD - HIST:tm + HIST_PAD, :]

    @pl.when(s > 0)
    def _():
        _run(_ffn_out_steps(x1_ref, act_ref, lambda cs: ga2_ref[:, cs], w2_ref, gfin_ref, y_ref,
                            rows=tm, row_chunk=row_chunk, final=final),
             (conv_norm_steps(), 2))

    @pl.when(s <= n_tiles)
    def _():
        _run(_out_proj_steps(xres_ref, mix_ref, mod2, wout_ref, x1_ref))
        _run(_in_proj_steps(h_ref, (wqkvo_ref, wglu_ref, wgate_ref), p_ref),
             _norm_steps(x1_ref, h2_ref, gffn_ref, mod2, "sc2", "sh2", rows=tm, row_chunk=row_chunk))
        _run(keep_steps())


def _resident(shape):
    nd = len(shape)
    return pl.BlockSpec(shape, lambda *g: (0,) * nd, pipeline_mode=pl.Buffered(1))


def _layer_resident(layer, shape):
    nd = len(shape)
    return pl.BlockSpec((None,) + shape, lambda *g: (layer,) + (0,) * nd,
                        pipeline_mode=pl.Buffered(1))


def _prompt_layer(x, mod, layer, wl, g_final, *, tm, final):
    bsz, seq, d = x.shape
    lw = functools.partial(_layer_resident, layer)
    assert seq % tm == 0 and tm % CHUNK == 0 and tm >= HIST_PAD
    nt = seq // tm
    n_tiles = bsz * nt
    kern = functools.partial(_prompt_layer_kernel, tm=tm, tiles_per_seq=nt, final=final)

    def tile(s, lag):
        return jnp.clip(s - lag, 0, n_tiles - 1)

    def cur(s):
        return tile(s, 1)

    def prev(s):
        return tile(s, 2)

    in_specs = [
        pl.BlockSpec((None, tm, d), lambda s: (tile(s, 0) // nt, tile(s, 0) % nt, 0)),
        pl.BlockSpec((None, 6, d), lambda s: (tile(s, 0) // nt, 0, 0)),
        pl.BlockSpec((None, 6, d), lambda s: (cur(s) // nt, 0, 0)),
        lw((1, d)),
        lw((d, COL_A)),
        lw((d, D_OUT_PAD)),
        lw((d, LANES)),
        lw((1, LANES)),
        lw((1, MLSTM_WIDTH)),
        lw((CONV_WIDTH, CONV_CH)),
        lw((1, CONV_CH)),
        lw((1, CONV_CH)),
        lw((1, CONV_CH)),
        lw((d, D_OUT_PAD)),
        lw((1, d)),
        lw((d, 2 * D_FF)),
        lw((D_FF, D_OUT_PAD)),
        _resident((1, d)),
    ]
    out_specs = [
        pl.BlockSpec((None, tm, d), lambda s: (prev(s) // nt, prev(s) % nt, 0)),
        pl.BlockSpec((None, HEADS, HEAD_DIM, HEAD_DIM), lambda s: (cur(s) // nt, 0, 0, 0)),
        pl.BlockSpec((None, HEADS, HEAD_DIM), lambda s: (cur(s) // nt, 0, 0)),
        pl.BlockSpec((None, HEADS, LANES), lambda s: (cur(s) // nt, 0, 0)),
        pl.BlockSpec((None, HIST, CONV_CH), lambda s: (cur(s) // nt, 0, 0)),
    ]
    out_shape = [
        jax.ShapeDtypeStruct((bsz, seq, d), F32),
        jax.ShapeDtypeStruct((bsz, HEADS, HEAD_DIM, HEAD_DIM), F32),
        jax.ShapeDtypeStruct((bsz, HEADS, HEAD_DIM), F32),
        jax.ShapeDtypeStruct((bsz, HEADS, LANES), F32),
        jax.ShapeDtypeStruct((bsz, HIST, CONV_CH), F32),
    ]
    scratch = [
        pltpu.VMEM((tm, d), BF16),
        pltpu.VMEM((tm, PROJ_PAD), F32),
        pltpu.VMEM((tm, d), BF16),
        pltpu.VMEM((tm, d), F32),
        pltpu.VMEM((tm, d), F32),
        pltpu.VMEM((tm, d), BF16),
        pltpu.VMEM((1, d), F32),
        pltpu.VMEM((tm, D_FF), BF16),
        pltpu.VMEM((CONV_SLABS, tm + HIST_PAD, LANES), F32),
        pltpu.VMEM((CONV_SLABS, tm, LANES), F32),
        pltpu.VMEM((HEADS, HEAD_DIM, HEAD_DIM), F32),
        pltpu.VMEM((SUBLANES, HEAD_DIM), F32),
        pltpu.VMEM((SUBLANES, LANES), F32),
    ]
    return pl.pallas_call(
        kern,
        grid=(n_tiles + 2,),
        in_specs=in_specs,
        out_specs=out_specs,
        out_shape=out_shape,
        scratch_shapes=scratch,
        compiler_params=pltpu.CompilerParams(
            dimension_semantics=("arbitrary",), vmem_limit_bytes=VMEM_LIMIT),
        name="prompt_layer",
    )(x, mod, mod, wl["g_mix"], wl["w_qkvo"], wl["w_glu"], wl["w_gate"], wl["b_gates"], wl["g_head"], wl["w_dw"], wl["b_dw"],
      wl["ln_g"], wl["ln_b"], wl["w_out"], wl["g_ffn"], wl["w_ffn13"], wl["w_ffn2"], g_final)


def _sample_in_kernel(x_ref, mod_ref, gmix_ref, wqkvo_ref, wglu_ref, wgate_ref, p_ref, h_ref):
    h = _rms_mod(x_ref[...], gmix_ref[...], mod_ref[:, D_MODEL:2 * D_MODEL], mod_ref[:, 0:D_MODEL])
    h_ref[...] = h.astype(BF16)
    _run(_in_proj_steps(h_ref, (wqkvo_ref, wglu_ref, wgate_ref), p_ref))


def _sample_in_proj(x, mod, layer, wl):
    rows, d = x.shape
    lw = functools.partial(_layer_resident, layer)
    return pl.pallas_call(
        _sample_in_kernel,
        grid=(1,),
        in_specs=[_resident((rows, d)), _resident(mod.shape), lw((1, d)), lw((d, COL_A)),
                  lw((d, D_OUT_PAD)), lw((d, LANES))],
        out_specs=pl.BlockSpec((rows, PROJ_PAD), lambda i: (0, 0)),
        out_shape=jax.ShapeDtypeStruct((rows, PROJ_PAD), F32),
        scratch_shapes=[pltpu.VMEM((rows, D_MODEL), BF16)],
        compiler_params=pltpu.CompilerParams(
            dimension_semantics=("arbitrary",), vmem_limit_bytes=VMEM_LIMIT),
        name="sample_in_proj",
    )(x, mod, wl["g_mix"], wl["w_qkvo"], wl["w_glu"], wl["w_gate"])


def _sample_step_kernel(p_ref, c_ref, n_ref, m_ref, conv_ref, bg_ref, ghead_ref, wdw_ref, bdw_ref,
                        lng_ref, lnb_ref, *rest, bt, n_alias, fill_other):
    hm_ref, zc_ref, c_out, n_out, m_out, conv_out = rest[n_alias:]
    slot = pl.program_id(0)

    @pl.when(slot == 0)
    def _():
        scale = HEAD_DIM ** -0.5
        gates = p_ref[:, COL_G:COL_G + LANES] + bg_ref[...]
        logf = pltpu.roll(_log_sigmoid(gates), LANES - HEADS, axis=1)
        inter = logf + m_ref[...]
        m_t = jnp.maximum(inter, gates)
        dw_all = jnp.exp(gates - m_t)
        a_all = jnp.exp(inter - m_t)
        eb_all = jnp.exp(-m_t)
        m_out[...] = m_t
        for h in range(HEADS):
            hs = slice(h * HEAD_DIM, (h + 1) * HEAD_DIM)
            q = p_ref[:, COL_Q + h * HEAD_DIM:COL_Q + (h + 1) * HEAD_DIM]
            k = p_ref[:, COL_K + h * HEAD_DIM:COL_K + (h + 1) * HEAD_DIM] * scale
            v = p_ref[:, COL_V + h * HEAD_DIM:COL_V + (h + 1) * HEAD_DIM]
            o = p_ref[:, COL_O + h * HEAD_DIM:COL_O + (h + 1) * HEAD_DIM]
            dw = dw_all[:, h:h + 1]
            a_in = a_all[:, h:h + 1]
            n_h = n_ref[:, hs]
            kw = k * dw
            s = jnp.sum(q * k, axis=-1, keepdims=True) * dw
            qn = jnp.sum(q * n_h, axis=-1, keepdims=True)
            n_out[:, hs] = a_in * n_h + kw
            q_t = q.T
            pad_rows = jnp.zeros((HEAD_DIM - bt, HEAD_DIM), F32)
            kw_t = jnp.concatenate([kw, pad_rows], axis=0).T.astype(BF16)
            v_wide = jnp.concatenate([v] * bt, axis=1)
            own = (lax.broadcasted_iota(jnp.int32, v_wide.shape, 1) // HEAD_DIM
                   == lax.broadcasted_iota(jnp.int32, v_wide.shape, 0))
            v_bd = jnp.concatenate([jnp.where(own, v_wide, 0.0),
                                    jnp.zeros((HEAD_DIM - bt, bt * HEAD_DIM), F32)], axis=0)
            outer = _bdot(kw_t, v_bd.astype(BF16))
            a_rows = jnp.broadcast_to(a_in, (bt, HEAD_DIM))
            qc_rows = []
            for b in range(bt):
                c_mat = c_ref[b, h]
                qc_rows.append(jnp.sum(q_t[:, b:b + 1] * c_mat, axis=0, keepdims=True))
                c_out[b, h] = a_rows[b:b + 1, :] * c_mat + outer[:, b * HEAD_DIM:(b + 1) * HEAD_DIM]
            qc = jnp.concatenate(qc_rows, axis=0)
            num = s * v + qc * a_in
            den = s + a_in * qn
            bound = jnp.maximum(jnp.abs(den), eb_all[:, h:h + 1])
            hh = num * (1.0 / bound)
            hn = hh * lax.rsqrt(jnp.mean(hh * hh, axis=-1, keepdims=True) + EPS) * ghead_ref[:, hs]
            hm_ref[:, hs] = hn * _sigmoid(o)

        u = p_ref[:, COL_A:COL_A + CONV_CH] * _sigmoid(p_ref[:, COL_B:COL_B + CONV_CH])
        z = u * wdw_ref[HIST:HIST + 1, :] + bdw_ref[...]
        for w in range(HIST):
            z = z + conv_ref[w] * wdw_ref[w:w + 1, :]
        zc_ref[...] = _layernorm_silu(z, lng_ref[...], lnb_ref[...])
        conv_out[0:HIST - 1] = conv_ref[1:HIST]
        conv_out[HIST - 1] = u

    if fill_other:
        @pl.when(slot != 0)
        def _():
            c_out[...] = jnp.zeros_like(c_out)
            conv_out[...] = jnp.zeros_like(conv_out)


def _sample_step(p, layer, state_c, state_n2, state_m_pad, state_conv, stacked, wl, *, bt=8):
    rows = p.shape[0]
    depth = state_c.shape[0]
    assert rows % bt == 0
    nblk = rows // bt
    first = stacked is None
    n_slots = depth if first else 1

    def blk_i(slot, i):
        return jnp.where(slot == 0, i, nblk - 1)

    def lay(*s):
        return pl.BlockSpec((None, bt) + s, lambda slot, i: (layer, blk_i(slot, i)) + (0,) * len(s))

    def row(*s):
        return pl.BlockSpec((bt,) + s, lambda slot, i: (blk_i(slot, i),) + (0,) * len(s))

    def stk(*s):
        return pl.BlockSpec((None, bt) + s,
                            lambda slot, i: ((slot if first else layer), i) + (0,) * len(s))

    conv_in = pl.BlockSpec((None, HIST, bt, CONV_CH),
                           lambda slot, i: (layer, 0, blk_i(slot, i), 0))
    conv_stk = pl.BlockSpec((None, HIST, bt, CONV_CH),
                            lambda slot, i: ((slot if first else layer), 0, i, 0))
    lw = functools.partial(_layer_resident, layer)
    in_specs = [
        row(PROJ_PAD), lay(HEADS, HEAD_DIM, HEAD_DIM), lay(MLSTM_WIDTH), lay(LANES), conv_in,
        lw((1, LANES)), lw((1, MLSTM_WIDTH)), lw((CONV_WIDTH, CONV_CH)),
        lw((1, CONV_CH)), lw((1, CONV_CH)), lw((1, CONV_CH)),
    ]
    args = [p, state_c, state_n2, state_m_pad, state_conv, wl["b_gates"], wl["g_head"], wl["w_dw"],
            wl["b_dw"], wl["ln_g"], wl["ln_b"]]
    aliases = {}
    if not first:
        aliases = {len(args): 2, len(args) + 1: 5}
        in_specs += [pl.BlockSpec(memory_space=pl.ANY), pl.BlockSpec(memory_space=pl.ANY)]
        args += list(stacked)
    out_shape = [
        jax.ShapeDtypeStruct((rows, MLSTM_WIDTH), F32),
        jax.ShapeDtypeStruct((rows, CONV_CH), F32),
        jax.ShapeDtypeStruct(state_c.shape, F32),
        jax.ShapeDtypeStruct((rows, MLSTM_WIDTH), F32),
        jax.ShapeDtypeStruct((rows, LANES), F32),
        jax.ShapeDtypeStruct(state_conv.shape, F32),
    ]
    hm, zc, c_new, n_new, m_new, conv_new = pl.pallas_call(
        functools.partial(_sample_step_kernel, bt=bt, n_alias=len(aliases),
                          fill_other=first and depth > 1),
        grid=(n_slots, nblk),
        in_specs=in_specs,
        out_specs=[row(MLSTM_WIDTH), row(CONV_CH), stk(HEADS, HEAD_DIM, HEAD_DIM),
                   row(MLSTM_WIDTH), row(LANES), conv_stk],
        out_shape=out_shape,
        input_output_aliases=aliases,
        compiler_params=pltpu.CompilerParams(
            dimension_semantics=("arbitrary", "arbitrary"), vmem_limit_bytes=VMEM_LIMIT),
        name="sample_step",
    )(*args)
    return hm, zc, (c_new, conv_new), n_new.reshape(rows, HEADS, HEAD_DIM), m_new[:, :HEADS]


def _sample_out_kernel(x_ref, hm_ref, zc_ref, mod_ref, wout_ref, gffn_ref, w13_ref, w2_ref,
                       gfin_ref, y_ref, mix_ref, x1_ref, h2_ref, act_ref, *, rows, final):
    mix_ref[:, 0:MLSTM_WIDTH] = hm_ref[...].astype(BF16)
    mix_ref[:, MLSTM_WIDTH:] = zc_ref[...].astype(BF16)

    def mod_rows(name, cs, r=slice(None)):
        i = _MOD_ROW[name]
        if isinstance(cs, slice) and cs == slice(None):
            cs = slice(0, D_MODEL)
        return mod_ref[r, i * D_MODEL + cs.start:i * D_MODEL + cs.stop]

    _run(itertools.chain(
        _out_proj_steps(x_ref, mix_ref, mod_rows, wout_ref, x1_ref),
        _norm_steps(x1_ref, h2_ref, gffn_ref, mod_rows, "sc2", "sh2", rows=rows, row_chunk=32),
        _ffn_gate_steps(h2_ref, w13_ref, act_ref),
        _ffn_out_steps(x1_ref, act_ref, lambda cs: mod_rows("ga2", cs), w2_ref, gfin_ref, y_ref,
                       rows=rows, row_chunk=32, final=final)))


def _sample_out(x, hm, zc, mod, layer, wl, g_final, *, final):
    rows, d = x.shape
    lw = functools.partial(_layer_resident, layer)
    return pl.pallas_call(
        functools.partial(_sample_out_kernel, rows=rows, final=final),
        grid=(1,),
        in_specs=[_resident((rows, d)), _resident(hm.shape), _resident(zc.shape),
                  _resident(mod.shape), lw((d, D_OUT_PAD)), lw((1, d)), lw((d, 2 * D_FF)),
                  lw((D_FF, D_OUT_PAD)),
                  _resident((1, d))],
        out_specs=pl.BlockSpec((rows, d), lambda i: (0, 0)),
        out_shape=jax.ShapeDtypeStruct((rows, D_MODEL), F32),
        scratch_shapes=[
            pltpu.VMEM((rows, D_MODEL), BF16),
            pltpu.VMEM((rows, D_MODEL), F32),
            pltpu.VMEM((rows, D_MODEL), BF16),
            pltpu.VMEM((rows, D_FF), BF16),
        ],
        compiler_params=pltpu.CompilerParams(
            dimension_semantics=("arbitrary",), vmem_limit_bytes=VMEM_LIMIT),
        name="sample_out",
    )(x, hm, zc, mod, wl["w_out"], wl["g_ffn"], wl["w_ffn13"], wl["w_ffn2"], g_final)


def _prep_weights(g_mix, w_in, b_gates, g_head, w_dw, b_dw, ln_g, ln_b, w_out, g_ffn, w_ffn13,
                  w_ffn2):
    depth, d, _ = w_in.shape
    n_gate = 2 * HEADS
    gate0 = 4 * MLSTM_WIDTH
    lane_pad = lambda w, n: jnp.pad(w.astype(BF16), ((0, 0), (0, 0), (0, n - w.shape[2])))
    return {
        "g_mix": g_mix.reshape(depth, 1, d),
        "w_qkvo": w_in[:, :, :gate0].astype(BF16),
        "w_glu": lane_pad(w_in[:, :, gate0 + n_gate:], D_OUT_PAD),
        "w_gate": lane_pad(w_in[:, :, gate0:gate0 + n_gate], LANES),
        "b_gates": jnp.pad(b_gates, ((0, 0), (0, LANES - n_gate))).reshape(depth, 1, LANES),
        "g_head": g_head.reshape(depth, 1, MLSTM_WIDTH),
        "w_dw": w_dw,
        "b_dw": b_dw.reshape(depth, 1, CONV_CH),
        "ln_g": ln_g.reshape(depth, 1, CONV_CH),
        "ln_b": ln_b.reshape(depth, 1, CONV_CH),
        "w_out": jnp.pad(w_out.astype(BF16), ((0, 0), (0, 0), (0, D_OUT_PAD - d))),
        "g_ffn": g_ffn.reshape(depth, 1, d),
        "w_ffn13": w_ffn13.astype(BF16),
        "w_ffn2": jnp.pad(w_ffn2.astype(BF16), ((0, 0), (0, 0), (0, D_OUT_PAD - d))),
    }


def _forward(x_prompt, x_sample, c_prompt, c_sample, state_C, state_n, state_m, state_conv,
             w_ada, b_ada, g_mix, w_in, b_gates, g_head, w_dw, b_dw, ln_g, ln_b, w_out,
             g_ffn, w_ffn13, w_ffn2, g_final, *, tm):
    depth = w_ada.shape[0]
    bp = x_prompt.shape[0]
    d = x_prompt.shape[-1]
    mod = _modulation(jnp.concatenate([c_prompt, c_sample], axis=0), w_ada, b_ada)
    gfin = g_final.reshape(1, d)
    xp = x_prompt
    xs = x_sample.reshape(x_sample.shape[0], d)
    outs_p = [[], [], [], []]
    ns_l, ms_l = [], []
    stacked = None
    n_seq = xs.shape[0]
    state_n2 = state_n.reshape(depth, n_seq, MLSTM_WIDTH)
    state_m_pad = jnp.pad(state_m, ((0, 0), (0, 0), (0, LANES - HEADS)))
    state_conv_t = jnp.swapaxes(state_conv, 1, 2)
    wl = _prep_weights(g_mix, w_in, b_gates, g_head, w_dw, b_dw, ln_g, ln_b, w_out, g_ffn,
                       w_ffn13, w_ffn2)
    for l in range(depth):
        final = l == depth - 1
        mod_p = mod[l, :bp].reshape(bp, 6, d)
        mod_s = mod[l, bp:]
        xp, c1, n1, m1, b1 = _prompt_layer(xp, mod_p, l, wl, gfin, tm=tm, final=final)
        for acc, val in zip(outs_p, (c1, n1, m1[:, :, 0], b1)):
            acc.append(val)
        p = _sample_in_proj(xs, mod_s, l, wl)
        hm, zc, stacked, n2, m2 = _sample_step(p, l, state_C, state_n2, state_m_pad, state_conv_t,
                                               stacked, wl)
        xs = _sample_out(xs, hm, zc, mod_s, l, wl, gfin, final=final)
        ns_l.append(n2)
        ms_l.append(m2)
    y_sample = xs.reshape(x_sample.shape)
    c_sample_new, conv_t_new = stacked
    return ((xp, y_sample) + tuple(jnp.stack(a) for a in outs_p)
            + (c_sample_new, jnp.stack(ns_l), jnp.stack(ms_l), jnp.swapaxes(conv_t_new, 1, 2)))


def kernel(x_prompt, x_sample, c_prompt, c_sample, state_C, state_n, state_m, state_conv, w_ada, b_ada, g_mix, w_in, b_gates, g_head, w_dw, b_dw, ln_g, ln_b, w_out, g_ffn, w_ffn13, w_ffn2, g_final):
    return _forward(x_prompt, x_sample, c_prompt, c_sample, state_C, state_n, state_m, state_conv,
                    w_ada, b_ada, g_mix, w_in, b_gates, g_head, w_dw, b_dw, ln_g, ln_b, w_out,
                    g_ffn, w_ffn13, w_ffn2, g_final, tm=256)
```

```python
import functools
import itertools

import jax
import jax.numpy as jnp
from jax import lax
from jax.experimental import pallas as pl
from jax.experimental.pallas import tpu as pltpu

F32 = jnp.float32
BF16 = jnp.bfloat16

D_MODEL = 1024
HEADS = 4
HEAD_DIM = 128
MLSTM_WIDTH = HEADS * HEAD_DIM
CONV_CH = D_MODEL - MLSTM_WIDTH
CONV_WIDTH = 31
HIST = CONV_WIDTH - 1
D_FF = 2816
CHUNK = 128
EPS = 1e-6

LANES = 128
SUBLANES = 8
MXU_N = 256

COL_Q = 0
COL_K = MLSTM_WIDTH
COL_V = 2 * MLSTM_WIDTH
COL_O = 3 * MLSTM_WIDTH
COL_A = 4 * MLSTM_WIDTH
COL_B = COL_A + CONV_CH
COL_G = COL_B + CONV_CH
PROJ_PAD = COL_G + LANES
HIST_PAD = 32
CONV_SLABS = CONV_CH // LANES
CONV_STRIDE = 2
D_OUT_PAD = D_MODEL + LANES
VMEM_LIMIT = 60 * 1024 * 1024


def _log_sigmoid(x):
    return jnp.minimum(x, 0.0) - jnp.log1p(jnp.exp(-jnp.abs(x)))


def _sigmoid(x):
    return 0.5 * jnp.tanh(0.5 * x) + 0.5


def _bdot(a, b):
    return jnp.dot(a, b, preferred_element_type=F32)


def _bdot_nt(a, b):
    return lax.dot_general(a, b, (((1,), (1,)), ((), ())), preferred_element_type=F32)


def _row_slices(rows, chunk):
    chunk = min(chunk, rows)
    assert rows % chunk == 0
    return [slice(r0, r0 + chunk) for r0 in range(0, rows, chunk)]


def _rms_mod(x, g, sc, sh):
    ms = jnp.mean(x * x, axis=-1, keepdims=True)
    return (x * lax.rsqrt(ms + EPS) * g) * (1.0 + sc) + sh


def _mod_kernel(c_ref, w_ref, b_ref, o_ref):
    o_ref[...] = _bdot(c_ref[...].astype(BF16), w_ref[...].astype(BF16)) + b_ref[...]


def _modulation(c_all, w_ada, b_ada):
    depth, d, n6 = w_ada.shape
    rows = c_all.shape[0]
    tn = 1024
    return pl.pallas_call(
        _mod_kernel,
        grid=(depth, n6 // tn),
        in_specs=[
            pl.BlockSpec((rows, d), lambda l, n: (0, 0)),
            pl.BlockSpec((None, d, tn), lambda l, n: (l, 0, n)),
            pl.BlockSpec((None, 1, tn), lambda l, n: (l, 0, n)),
        ],
        out_specs=pl.BlockSpec((None, rows, tn), lambda l, n: (l, 0, n)),
        out_shape=jax.ShapeDtypeStruct((depth, rows, n6), F32),
        compiler_params=pltpu.CompilerParams(
            dimension_semantics=("arbitrary", "arbitrary"), vmem_limit_bytes=VMEM_LIMIT),
        name="adaln_mod",
    )(c_all, w_ada, b_ada.reshape(depth, 1, n6))


def _norm_steps(src_ref, dst_ref, g_ref, mod, sc, sh, *, rows, row_chunk):
    for r in _row_slices(rows, row_chunk):
        h = _rms_mod(src_ref[r, :], g_ref[...], mod(sc, slice(None), r), mod(sh, slice(None), r))
        dst_ref[r, :] = h.astype(BF16)
        yield


def _in_proj_steps(h_ref, w_refs, p_ref):
    for w_ref, col0, width in zip(w_refs, (COL_Q, COL_A, COL_G), (COL_A, 2 * CONV_CH, LANES)):
        for n0 in range(0, width, MXU_N):
            n1 = min(n0 + MXU_N, width)
            p_ref[:, col0 + n0:col0 + n1] = _bdot(h_ref[...], w_ref[:, n0:n1])
            yield


def _out_proj_steps(x_ref, mix_ref, mod, wout_ref, x1_ref):
    for n0 in range(0, D_MODEL, MXU_N):
        cs = slice(n0, n0 + MXU_N)
        mx = _bdot(mix_ref[...], wout_ref[:, cs])
        x1_ref[:, cs] = x_ref[:, cs] + mod("ga1", cs, slice(None)) * mx
        yield


def _ffn_gate_steps(h2_ref, w13_ref, act_ref):
    for n0 in range(0, D_FF, MXU_N):
        gg = _bdot(h2_ref[...], w13_ref[:, n0:n0 + MXU_N])
        gg = gg * _sigmoid(gg)
        yield
        uu = _bdot(h2_ref[...], w13_ref[:, D_FF + n0:D_FF + n0 + MXU_N])
        act_ref[:, n0:n0 + MXU_N] = (gg * uu).astype(BF16)
        yield


def _ffn_out_steps(x1_ref, act_ref, ga2, w2_ref, gfin_ref, y_ref, *, rows, row_chunk, final):
    dst = x1_ref if final else y_ref
    for n0 in range(0, D_MODEL, MXU_N):
        cs = slice(n0, n0 + MXU_N)
        ff = _bdot(act_ref[...], w2_ref[:, cs])
        dst[:, cs] = x1_ref[:, cs] + ga2(cs) * ff
        yield
    if final:
        for r in _row_slices(rows, row_chunk):
            xb = x1_ref[r, :]
            ms = jnp.mean(xb * xb, axis=-1, keepdims=True)
            y_ref[r, :] = xb * lax.rsqrt(ms + EPS) * gfin_ref[...]
            yield


def _run(*gens):
    live = [g if isinstance(g, tuple) else (g, 1) for g in gens]
    while live:
        for entry in list(live):
            g, k = entry
            for _ in range(k):
                try:
                    next(g)
                except StopIteration:
                    live.remove(entry)
                    break


def _layernorm_silu(z, lng, lnb):
    mu = jnp.mean(z, axis=-1, keepdims=True)
    zc = z - mu
    var = jnp.mean(zc * zc, axis=-1, keepdims=True)
    zn = zc * lax.rsqrt(var + EPS) * lng + lnb
    return zn * _sigmoid(zn)


_MOD_ROW = {"sh1": 0, "sc1": 1, "ga1": 2, "sh2": 3, "sc2": 4, "ga2": 5}


def _cumsum_chunk(tril_bf, x):
    hi = x.astype(BF16)
    r1 = x - hi.astype(F32)
    mid = r1.astype(BF16)
    lo = (r1 - mid.astype(F32)).astype(BF16)
    return _bdot(tril_bf, hi) + _bdot(tril_bf, mid) + _bdot(tril_bf, lo)


def _mlstm_chunk(p_ref, r0, bg_ref, ghead_ref, mix_ref, C_ref, n_ref, m_ref, tril, tril_bf):
    rs = slice(r0, r0 + CHUNK)
    gates = p_ref[rs, COL_G:COL_G + LANES] + bg_ref[...]
    fcum = _cumsum_chunk(tril_bf, _log_sigmoid(gates))
    gates_t = gates.T
    fcum_t = fcum.T
    yield
    heads = range(HEADS)
    col = lambda base, h: slice(base + h * HEAD_DIM, base + (h + 1) * HEAD_DIM)
    st = [dict() for _ in heads]
    for h in heads:
        d = st[h]
        d["q"] = p_ref[rs, col(COL_Q, h)]
        d["k"] = p_ref[rs, col(COL_K, h)] * (HEAD_DIM ** -0.5)
        d["v_bf"] = p_ref[rs, col(COL_V, h)].astype(BF16)
        d["q_bf"] = d["q"].astype(BF16)
        d["f_col"] = fcum[:, HEADS + h:HEADS + h + 1]
        i_col = gates[:, h:h + 1]
        f_row = fcum_t[HEADS + h:HEADS + h + 1, :]
        i_row = gates_t[h:h + 1, :]
        g = f_row[:, CHUNK - 1:CHUNK]
        d["g"] = g
        d["dlog"] = jnp.where(tril, d["f_col"] - f_row + i_row, -jnp.inf)
        d["dmax"] = jnp.max(d["dlog"], axis=-1, keepdims=True)
        d["wlog"] = g - d["f_col"] + i_col
        d["wmax"] = jnp.max(g - f_row + i_row, axis=-1, keepdims=True)
        d["qk"] = _bdot_nt(d["q_bf"], d["k"].astype(BF16))
        yield
    for h in heads:
        d = st[h]
        m_prev = m_ref[h:h + 1, 0:1]
        inter = d["f_col"] + m_prev
        m_t = jnp.maximum(inter, d["dmax"])
        d["a_inter"] = jnp.exp(inter - m_t)
        d["ebound"] = jnp.exp(-m_t)
        s = d["qk"] * jnp.exp(d["dlog"] - m_t)
        d["ssum"] = jnp.sum(s, axis=-1, keepdims=True)
        d["sv"] = _bdot(s.astype(BF16), d["v_bf"])
        m_new = jnp.maximum(d["g"] + m_prev, d["wmax"])
        d["decay"] = jnp.exp(d["g"] + m_prev - m_new)
        kw = d["k"] * jnp.exp(d["wlog"] - m_new)
        d["kwsum"] = jnp.sum(kw, axis=0, keepdims=True)
        d["kv"] = _bdot(kw.T.astype(BF16), d["v_bf"])
        m_ref[h:h + 1, :] = jnp.broadcast_to(m_new, (1, LANES))
        yield
    for h in heads:
        d = st[h]
        n_row = n_ref[h:h + 1, :]
        c_mat = C_ref[h]
        num = d["sv"] + _bdot(d["q_bf"], c_mat.astype(BF16)) * d["a_inter"]
        qn = jnp.sum(d["q"] * n_row, axis=-1, keepdims=True)
        den = d["ssum"] + d["a_inter"] * qn
        bound = jnp.maximum(jnp.abs(den), d["ebound"])
        hh = num * (1.0 / bound)
        C_ref[h] = d["decay"] * c_mat + d["kv"]
        n_ref[h:h + 1, :] = d["decay"] * n_row + d["kwsum"]
        hn = hh * lax.rsqrt(jnp.mean(hh * hh, axis=-1, keepdims=True) + EPS) * ghead_ref[:, col(0, h)]
        mix_ref[rs, col(0, h)] = (hn * _sigmoid(p_ref[rs, col(COL_O, h)])).astype(BF16)
        yield


def _prompt_layer_kernel(x_ref, mod1_ref, mod2_ref, gmix_ref, wqkvo_ref, wglu_ref, wgate_ref, bg_ref,
                         ghead_ref, wdw_ref, bdw_ref, lng_ref, lnb_ref, wout_ref, gffn_ref, w13_ref,
                         w2_ref, gfin_ref,
                         y_ref, c_out, n_out, m_out, conv_out,
                         h_ref, p_ref, mix_ref, xres_ref, x1_ref, h2_ref, ga2_ref, act_ref, ubuf_ref,
                         zc_ref, C_ref, n_ref, m_ref, *, tm, tiles_per_seq, final):
    s = pl.program_id(0)
    n_tiles = pl.num_programs(0) - 2
    j = jnp.clip(s - 1, 0, n_tiles - 1) % tiles_per_seq
    last = tiles_per_seq - 1
    row_chunk = 32

    @pl.when(s == 0)
    def _():
        for ref in (p_ref, xres_ref, x1_ref, h2_ref, ga2_ref):
            ref[...] = jnp.zeros_like(ref)

    @pl.when(j == 0)
    def _():
        C_ref[...] = jnp.zeros_like(C_ref)
        n_ref[...] = jnp.zeros_like(n_ref)
        m_ref[...] = jnp.zeros_like(m_ref)
        ubuf_ref[:, 0:HIST_PAD, :] = jnp.zeros((CONV_SLABS, HIST_PAD, LANES), F32)

    @pl.when(j > 0)
    def _():
        ubuf_ref[:, 0:HIST_PAD, :] = ubuf_ref[:, tm:tm + HIST_PAD, :]

    def mod1(name, cs, r=None):
        i = _MOD_ROW[name]
        return mod1_ref[i:i + 1, cs]

    def mod2(name, cs, r=None):
        i = _MOD_ROW[name]
        return mod2_ref[i:i + 1, cs]

    row_id = lax.broadcasted_iota(jnp.int32, (CHUNK, CHUNK), 0)
    col_id = lax.broadcasted_iota(jnp.int32, (CHUNK, CHUNK), 1)
    tril = row_id >= col_id
    tril_bf = jnp.where(tril, 1.0, 0.0).astype(BF16)

    def mlstm_steps():
        for c in range(tm // CHUNK):
            yield from _mlstm_chunk(p_ref, c * CHUNK, bg_ref, ghead_ref, mix_ref, C_ref, n_ref,
                                    m_ref, tril, tril_bf)

    def glu_steps():
        for r in _row_slices(tm, row_chunk):
            ro = slice(r.start + HIST_PAD, r.stop + HIST_PAD)
            u = p_ref[r, COL_A:COL_A + CONV_CH] * _sigmoid(p_ref[r, COL_B:COL_B + CONV_CH])
            for g in range(CONV_SLABS):
                ubuf_ref[g, ro, :] = u[:, g * LANES:(g + 1) * LANES]
            yield

    def conv_steps(slabs):
        n_sets = row_chunk // SUBLANES
        firsts = [(t0 // CONV_STRIDE) * CONV_STRIDE * SUBLANES + t0 % CONV_STRIDE
                  for t0 in range(n_sets)]
        for g in slabs:
            gs = slice(g * LANES, (g + 1) * LANES)
            bias = jnp.broadcast_to(bdw_ref[:, gs], (SUBLANES, LANES))
            for base in range(0, tm, row_chunk):
                acc = [bias for _ in range(n_sets)]
                for w in range(CONV_WIDTH):
                    tap = wdw_ref[w:w + 1, gs]
                    for t0 in range(n_sets):
                        src = pl.ds(base + firsts[t0] + HIST_PAD - HIST + w, SUBLANES,
                                    stride=CONV_STRIDE)
                        acc[t0] = acc[t0] + ubuf_ref[g, src, :] * tap
                for t0 in range(n_sets):
                    zc_ref[g, pl.ds(base + firsts[t0], SUBLANES, stride=CONV_STRIDE), :] = acc[t0]
                yield

    def conv_norm_steps():
        for r in _row_slices(tm, row_chunk):
            z = jnp.concatenate([zc_ref[g, r, :] for g in range(CONV_SLABS)], axis=-1)
            mix_ref[r, MLSTM_WIDTH:] = _layernorm_silu(z, lng_ref[...], lnb_ref[...]).astype(BF16)
            yield

    def keep_steps():
        xres_ref[...] = x_ref[...]
        ga2_ref[...] = mod2("ga2", slice(None))
        yield

    _run(mlstm_steps(), _ffn_gate_steps(h2_ref, w13_ref, act_ref),
         (itertools.chain(glu_steps(), conv_steps(range(CONV_SLABS))), 2),
         _norm_steps(x_ref, h_ref, gmix_ref, mod1, "sc1", "sh1", rows=tm, row_chunk=row_chunk))

    @pl.when(jnp.logical_and(j == last, jnp.logical_and(s >= 1, s <= n_tiles)))
    def _():
        c_out[...] = C_ref[...]
        n_out[...] = n_ref[0:HEADS, :]
        m_out[...] = m_ref[0:HEADS, :]
        for g in range(CONV_SLABS):
            conv_out[:, g * LANES:(g + 1) * LANES] = ubuf_ref[g, tm + HIST_PAD - HIST:tm + HIST_PAD, :]

    @pl.when(s >= 0)
    def _():
        _run(_ffn_out_steps(x1_ref, act_ref, lambda cs: ga2_ref[:, cs], w2_ref, gfin_ref, y_ref,
                            rows=tm, row_chunk=row_chunk, final=final),
             (conv_norm_steps(), 2))

    @pl.when(s >= 0)
    def _():
        _run(_out_proj_steps(xres_ref, mix_ref, mod2, wout_ref, x1_ref))
        _run(_in_proj_steps(h_ref, (wqkvo_ref, wglu_ref, wgate_ref), p_ref),
             _norm_steps(x1_ref, h2_ref, gffn_ref, mod2, "sc2", "sh2", rows=tm, row_chunk=row_chunk))
        _run(keep_steps())


def _resident(shape):
    nd = len(shape)
    return pl.BlockSpec(shape, lambda *g: (0,) * nd, pipeline_mode=pl.Buffered(1))


def _layer_resident(layer, shape):
    nd = len(shape)
    return pl.BlockSpec((None,) + shape, lambda *g: (layer,) + (0,) * nd,
                        pipeline_mode=pl.Buffered(1))


def _prompt_layer(x, mod, layer, wl, g_final, *, tm, final):
    bsz, seq, d = x.shape
    lw = functools.partial(_layer_resident, layer)
    assert seq % tm == 0 and tm % CHUNK == 0 and tm >= HIST_PAD
    nt = seq // tm
    n_tiles = bsz * nt
    kern = functools.partial(_prompt_layer_kernel, tm=tm, tiles_per_seq=nt, final=final)

    def tile(s, lag):
        return jnp.clip(s - lag, 0, n_tiles - 1)

    def cur(s):
        return tile(s, 1)

    def prev(s):
        return tile(s, 2)

    in_specs = [
        pl.BlockSpec((None, tm, d), lambda s: (tile(s, 0) // nt, tile(s, 0) % nt, 0)),
        pl.BlockSpec((None, 6, d), lambda s: (tile(s, 0) // nt, 0, 0)),
        pl.BlockSpec((None, 6, d), lambda s: (cur(s) // nt, 0, 0)),
        lw((1, d)),
        lw((d, COL_A)),
        lw((d, D_OUT_PAD)),
        lw((d, LANES)),
        lw((1, LANES)),
        lw((1, MLSTM_WIDTH)),
        lw((CONV_WIDTH, CONV_CH)),
        lw((1, CONV_CH)),
        lw((1, CONV_CH)),
        lw((1, CONV_CH)),
        lw((d, D_OUT_PAD)),
        lw((1, d)),
        lw((d, 2 * D_FF)),
        lw((D_FF, D_OUT_PAD)),
        _resident((1, d)),
    ]
    out_specs = [
        pl.BlockSpec((None, tm, d), lambda s: (prev(s) // nt, prev(s) % nt, 0)),
        pl.BlockSpec((None, HEADS, HEAD_DIM, HEAD_DIM), lambda s: (cur(s) // nt, 0, 0, 0)),
        pl.BlockSpec((None, HEADS, HEAD_DIM), lambda s: (cur(s) // nt, 0, 0)),
        pl.BlockSpec((None, HEADS, LANES), lambda s: (cur(s) // nt, 0, 0)),
        pl.BlockSpec((None, HIST, CONV_CH), lambda s: (cur(s) // nt, 0, 0)),
    ]
    out_shape = [
        jax.ShapeDtypeStruct((bsz, seq, d), F32),
        jax.ShapeDtypeStruct((bsz, HEADS, HEAD_DIM, HEAD_DIM), F32),
        jax.ShapeDtypeStruct((bsz, HEADS, HEAD_DIM), F32),
        jax.ShapeDtypeStruct((bsz, HEADS, LANES), F32),
        jax.ShapeDtypeStruct((bsz, HIST, CONV_CH), F32),
    ]
    scratch = [
        pltpu.VMEM((tm, d), BF16),
        pltpu.VMEM((tm, PROJ_PAD), F32),
        pltpu.VMEM((tm, d), BF16),
        pltpu.VMEM((tm, d), F32),
        pltpu.VMEM((tm, d), F32),
        pltpu.VMEM((tm, d), BF16),
        pltpu.VMEM((1, d), F32),
        pltpu.VMEM((tm, D_FF), BF16),
        pltpu.VMEM((CONV_SLABS, tm + HIST_PAD, LANES), F32),
        pltpu.VMEM((CONV_SLABS, tm, LANES), F32),
        pltpu.VMEM((HEADS, HEAD_DIM, HEAD_DIM), F32),
        pltpu.VMEM((SUBLANES, HEAD_DIM), F32),
        pltpu.VMEM((SUBLANES, LANES), F32),
    ]
    return pl.pallas_call(
        kern,
        grid=(n_tiles + 2,),
        in_specs=in_specs,
        out_specs=out_specs,
        out_shape=out_shape,
        scratch_shapes=scratch,
        compiler_params=pltpu.CompilerParams(
            dimension_semantics=("arbitrary",), vmem_limit_bytes=VMEM_LIMIT),
        name="prompt_layer",
    )(x, mod, mod, wl["g_mix"], wl["w_qkvo"], wl["w_glu"], wl["w_gate"], wl["b_gates"], wl["g_head"], wl["w_dw"], wl["b_dw"],
      wl["ln_g"], wl["ln_b"], wl["w_out"], wl["g_ffn"], wl["w_ffn13"], wl["w_ffn2"], g_final)


def _sample_in_kernel(x_ref, mod_ref, gmix_ref, wqkvo_ref, wglu_ref, wgate_ref, p_ref, h_ref):
    h = _rms_mod(x_ref[...], gmix_ref[...], mod_ref[:, D_MODEL:2 * D_MODEL], mod_ref[:, 0:D_MODEL])
    h_ref[...] = h.astype(BF16)
    _run(_in_proj_steps(h_ref, (wqkvo_ref, wglu_ref, wgate_ref), p_ref))


def _sample_in_proj(x, mod, layer, wl):
    rows, d = x.shape
    lw = functools.partial(_layer_resident, layer)
    return pl.pallas_call(
        _sample_in_kernel,
        grid=(1,),
        in_specs=[_resident((rows, d)), _resident(mod.shape), lw((1, d)), lw((d, COL_A)),
                  lw((d, D_OUT_PAD)), lw((d, LANES))],
        out_specs=pl.BlockSpec((rows, PROJ_PAD), lambda i: (0, 0)),
        out_shape=jax.ShapeDtypeStruct((rows, PROJ_PAD), F32),
        scratch_shapes=[pltpu.VMEM((rows, D_MODEL), BF16)],
        compiler_params=pltpu.CompilerParams(
            dimension_semantics=("arbitrary",), vmem_limit_bytes=VMEM_LIMIT),
        name="sample_in_proj",
    )(x, mod, wl["g_mix"], wl["w_qkvo"], wl["w_glu"], wl["w_gate"])


def _sample_step_kernel(p_ref, c_ref, n_ref, m_ref, conv_ref, bg_ref, ghead_ref, wdw_ref, bdw_ref,
                        lng_ref, lnb_ref, *rest, bt, n_alias, fill_other):
    hm_ref, zc_ref, c_out, n_out, m_out, conv_out = rest[n_alias:]
    slot = pl.program_id(0)

    @pl.when(slot == 0)
    def _():
        scale = HEAD_DIM ** -0.5
        gates = p_ref[:, COL_G:COL_G + LANES] + bg_ref[...]
        logf = pltpu.roll(_log_sigmoid(gates), LANES - HEADS, axis=1)
        inter = logf + m_ref[...]
        m_t = jnp.maximum(inter, gates)
        dw_all = jnp.exp(gates - m_t)
        a_all = jnp.exp(inter - m_t)
        eb_all = jnp.exp(-m_t)
        m_out[...] = m_t
        for h in range(HEADS):
            hs = slice(h * HEAD_DIM, (h + 1) * HEAD_DIM)
            q = p_ref[:, COL_Q + h * HEAD_DIM:COL_Q + (h + 1) * HEAD_DIM]
            k = p_ref[:, COL_K + h * HEAD_DIM:COL_K + (h + 1) * HEAD_DIM] * scale
            v = p_ref[:, COL_V + h * HEAD_DIM:COL_V + (h + 1) * HEAD_DIM]
            o = p_ref[:, COL_O + h * HEAD_DIM:COL_O + (h + 1) * HEAD_DIM]
            dw = dw_all[:, h:h + 1]
            a_in = a_all[:, h:h + 1]
            n_h = n_ref[:, hs]
            kw = k * dw
            s = jnp.sum(q * k, axis=-1, keepdims=True) * dw
            qn = jnp.sum(q * n_h, axis=-1, keepdims=True)
            n_out[:, hs] = a_in * n_h + kw
            q_t = q.T
            pad_rows = jnp.zeros((HEAD_DIM - bt, HEAD_DIM), F32)
            kw_t = jnp.concatenate([kw, pad_rows], axis=0).T.astype(BF16)
            v_wide = jnp.concatenate([v] * bt, axis=1)
            own = (lax.broadcasted_iota(jnp.int32, v_wide.shape, 1) // HEAD_DIM
                   == lax.broadcasted_iota(jnp.int32, v_wide.shape, 0))
            v_bd = jnp.concatenate([jnp.where(own, v_wide, 0.0),
                                    jnp.zeros((HEAD_DIM - bt, bt * HEAD_DIM), F32)], axis=0)
            outer = _bdot(kw_t, v_bd.astype(BF16))
            a_rows = jnp.broadcast_to(a_in, (bt, HEAD_DIM))
            qc_rows = []
            for b in range(bt):
                c_mat = c_ref[b, h]
                qc_rows.append(jnp.sum(q_t[:, b:b + 1] * c_mat, axis=0, keepdims=True))
                c_out[b, h] = a_rows[b:b + 1, :] * c_mat + outer[:, b * HEAD_DIM:(b + 1) * HEAD_DIM]
            qc = jnp.concatenate(qc_rows, axis=0)
            num = s * v + qc * a_in
            den = s + a_in * qn
            bound = jnp.maximum(jnp.abs(den), eb_all[:, h:h + 1])
            hh = num * (1.0 / bound)
            hn = hh * lax.rsqrt(jnp.mean(hh * hh, axis=-1, keepdims=True) + EPS) * ghead_ref[:, hs]
            hm_ref[:, hs] = hn * _sigmoid(o)

        u = p_ref[:, COL_A:COL_A + CONV_CH] * _sigmoid(p_ref[:, COL_B:COL_B + CONV_CH])
        z = u * wdw_ref[HIST:HIST + 1, :] + bdw_ref[...]
        for w in range(HIST):
            z = z + conv_ref[w] * wdw_ref[w:w + 1, :]
        zc_ref[...] = _layernorm_silu(z, lng_ref[...], lnb_ref[...])
        conv_out[0:HIST - 1] = conv_ref[1:HIST]
        conv_out[HIST - 1] = u

    if fill_other:
        @pl.when(slot != 0)
        def _():
            c_out[...] = jnp.zeros_like(c_out)
            conv_out[...] = jnp.zeros_like(conv_out)


def _sample_step(p, layer, state_c, state_n2, state_m_pad, state_conv, stacked, wl, *, bt=8):
    rows = p.shape[0]
    depth = state_c.shape[0]
    assert rows % bt == 0
    nblk = rows // bt
    first = stacked is None
    n_slots = depth if first else 1

    def blk_i(slot, i):
        return jnp.where(slot == 0, i, nblk - 1)

    def lay(*s):
        return pl.BlockSpec((None, bt) + s, lambda slot, i: (layer, blk_i(slot, i)) + (0,) * len(s))

    def row(*s):
        return pl.BlockSpec((bt,) + s, lambda slot, i: (blk_i(slot, i),) + (0,) * len(s))

    def stk(*s):
        return pl.BlockSpec((None, bt) + s,
                            lambda slot, i: ((slot if first else layer), i) + (0,) * len(s))

    conv_in = pl.BlockSpec((None, HIST, bt, CONV_CH),
                           lambda slot, i: (layer, 0, blk_i(slot, i), 0))
    conv_stk = pl.BlockSpec((None, HIST, bt, CONV_CH),
                            lambda slot, i: ((slot if first else layer), 0, i, 0))
    lw = functools.partial(_layer_resident, layer)
    in_specs = [
        row(PROJ_PAD), lay(HEADS, HEAD_DIM, HEAD_DIM), lay(MLSTM_WIDTH), lay(LANES), conv_in,
        lw((1, LANES)), lw((1, MLSTM_WIDTH)), lw((CONV_WIDTH, CONV_CH)),
        lw((1, CONV_CH)), lw((1, CONV_CH)), lw((1, CONV_CH)),
    ]
    args = [p, state_c, state_n2, state_m_pad, state_conv, wl["b_gates"], wl["g_head"], wl["w_dw"],
            wl["b_dw"], wl["ln_g"], wl["ln_b"]]
    aliases = {}
    if not first:
        aliases = {len(args): 2, len(args) + 1: 5}
        in_specs += [pl.BlockSpec(memory_space=pl.ANY), pl.BlockSpec(memory_space=pl.ANY)]
        args += list(stacked)
    out_shape = [
        jax.ShapeDtypeStruct((rows, MLSTM_WIDTH), F32),
        jax.ShapeDtypeStruct((rows, CONV_CH), F32),
        jax.ShapeDtypeStruct(state_c.shape, F32),
        jax.ShapeDtypeStruct((rows, MLSTM_WIDTH), F32),
        jax.ShapeDtypeStruct((rows, LANES), F32),
        jax.ShapeDtypeStruct(state_conv.shape, F32),
    ]
    hm, zc, c_new, n_new, m_new, conv_new = pl.pallas_call(
        functools.partial(_sample_step_kernel, bt=bt, n_alias=len(aliases),
                          fill_other=first and depth > 1),
        grid=(n_slots, nblk),
        in_specs=in_specs,
        out_specs=[row(MLSTM_WIDTH), row(CONV_CH), stk(HEADS, HEAD_DIM, HEAD_DIM),
                   row(MLSTM_WIDTH), row(LANES), conv_stk],
        out_shape=out_shape,
        input_output_aliases=aliases,
        compiler_params=pltpu.CompilerParams(
            dimension_semantics=("arbitrary", "arbitrary"), vmem_limit_bytes=VMEM_LIMIT),
        name="sample_step",
    )(*args)
    return hm, zc, (c_new, conv_new), n_new.reshape(rows, HEADS, HEAD_DIM), m_new[:, :HEADS]


def _sample_out_kernel(x_ref, hm_ref, zc_ref, mod_ref, wout_ref, gffn_ref, w13_ref, w2_ref,
                       gfin_ref, y_ref, mix_ref, x1_ref, h2_ref, act_ref, *, rows, final):
    mix_ref[:, 0:MLSTM_WIDTH] = hm_ref[...].astype(BF16)
    mix_ref[:, MLSTM_WIDTH:] = zc_ref[...].astype(BF16)

    def mod_rows(name, cs, r=slice(None)):
        i = _MOD_ROW[name]
        if isinstance(cs, slice) and cs == slice(None):
            cs = slice(0, D_MODEL)
        return mod_ref[r, i * D_MODEL + cs.start:i * D_MODEL + cs.stop]

    _run(itertools.chain(
        _out_proj_steps(x_ref, mix_ref, mod_rows, wout_ref, x1_ref),
        _norm_steps(x1_ref, h2_ref, gffn_ref, mod_rows, "sc2", "sh2", rows=rows, row_chunk=32),
        _ffn_gate_steps(h2_ref, w13_ref, act_ref),
        _ffn_out_steps(x1_ref, act_ref, lambda cs: mod_rows("ga2", cs), w2_ref, gfin_ref, y_ref,
                       rows=rows, row_chunk=32, final=final)))


def _sample_out(x, hm, zc, mod, layer, wl, g_final, *, final):
    rows, d = x.shape
    lw = functools.partial(_layer_resident, layer)
    return pl.pallas_call(
        functools.partial(_sample_out_kernel, rows=rows, final=final),
        grid=(1,),
        in_specs=[_resident((rows, d)), _resident(hm.shape), _resident(zc.shape),
                  _resident(mod.shape), lw((d, D_OUT_PAD)), lw((1, d)), lw((d, 2 * D_FF)),
                  lw((D_FF, D_OUT_PAD)),
                  _resident((1, d))],
        out_specs=pl.BlockSpec((rows, d), lambda i: (0, 0)),
        out_shape=jax.ShapeDtypeStruct((rows, D_MODEL), F32),
        scratch_shapes=[
            pltpu.VMEM((rows, D_MODEL), BF16),
            pltpu.VMEM((rows, D_MODEL), F32),
            pltpu.VMEM((rows, D_MODEL), BF16),
            pltpu.VMEM((rows, D_FF), BF16),
        ],
        compiler_params=pltpu.CompilerParams(
            dimension_semantics=("arbitrary",), vmem_limit_bytes=VMEM_LIMIT),
        name="sample_out",
    )(x, hm, zc, mod, wl["w_out"], wl["g_ffn"], wl["w_ffn13"], wl["w_ffn2"], g_final)


def _prep_weights(g_mix, w_in, b_gates, g_head, w_dw, b_dw, ln_g, ln_b, w_out, g_ffn, w_ffn13,
                  w_ffn2):
    depth, d, _ = w_in.shape
    n_gate = 2 * HEADS
    gate0 = 4 * MLSTM_WIDTH
    lane_pad = lambda w, n: jnp.pad(w.astype(BF16), ((0, 0), (0, 0), (0, n - w.shape[2])))
    return {
        "g_mix": g_mix.reshape(depth, 1, d),
        "w_qkvo": w_in[:, :, :gate0].astype(BF16),
        "w_glu": lane_pad(w_in[:, :, gate0 + n_gate:], D_OUT_PAD),
        "w_gate": lane_pad(w_in[:, :, gate0:gate0 + n_gate], LANES),
        "b_gates": jnp.pad(b_gates, ((0, 0), (0, LANES - n_gate))).reshape(depth, 1, LANES),
        "g_head": g_head.reshape(depth, 1, MLSTM_WIDTH),
        "w_dw": w_dw,
        "b_dw": b_dw.reshape(depth, 1, CONV_CH),
        "ln_g": ln_g.reshape(depth, 1, CONV_CH),
        "ln_b": ln_b.reshape(depth, 1, CONV_CH),
        "w_out": jnp.pad(w_out.astype(BF16), ((0, 0), (0, 0), (0, D_OUT_PAD - d))),
        "g_ffn": g_ffn.reshape(depth, 1, d),
        "w_ffn13": w_ffn13.astype(BF16),
        "w_ffn2": jnp.pad(w_ffn2.astype(BF16), ((0, 0), (0, 0), (0, D_OUT_PAD - d))),
    }


def _forward(x_prompt, x_sample, c_prompt, c_sample, state_C, state_n, state_m, state_conv,
             w_ada, b_ada, g_mix, w_in, b_gates, g_head, w_dw, b_dw, ln_g, ln_b, w_out,
             g_ffn, w_ffn13, w_ffn2, g_final, *, tm):
    depth = w_ada.shape[0]
    bp = x_prompt.shape[0]
    d = x_prompt.shape[-1]
    mod = _modulation(jnp.concatenate([c_prompt, c_sample], axis=0), w_ada, b_ada)
    gfin = g_final.reshape(1, d)
    xp = x_prompt
    xs = x_sample.reshape(x_sample.shape[0], d)
    outs_p = [[], [], [], []]
    ns_l, ms_l = [], []
    stacked = None
    n_seq = xs.shape[0]
    state_n2 = state_n.reshape(depth, n_seq, MLSTM_WIDTH)
    state_m_pad = jnp.pad(state_m, ((0, 0), (0, 0), (0, LANES - HEADS)))
    state_conv_t = jnp.swapaxes(state_conv, 1, 2)
    wl = _prep_weights(g_mix, w_in, b_gates, g_head, w_dw, b_dw, ln_g, ln_b, w_out, g_ffn,
                       w_ffn13, w_ffn2)
    for l in range(depth):
        final = l == depth - 1
        mod_p = mod[l, :bp].reshape(bp, 6, d)
        mod_s = mod[l, bp:]
        xp, c1, n1, m1, b1 = _prompt_layer(xp, mod_p, l, wl, gfin, tm=tm, final=final)
        for acc, val in zip(outs_p, (c1, n1, m1[:, :, 0], b1)):
            acc.append(val)
        p = _sample_in_proj(xs, mod_s, l, wl)
        hm, zc, stacked, n2, m2 = _sample_step(p, l, state_C, state_n2, state_m_pad, state_conv_t,
                                               stacked, wl)
        xs = _sample_out(xs, hm, zc, mod_s, l, wl, gfin, final=final)
        ns_l.append(n2)
        ms_l.append(m2)
    y_sample = xs.reshape(x_sample.shape)
    c_sample_new, conv_t_new = stacked
    return ((xp, y_sample) + tuple(jnp.stack(a) for a in outs_p)
            + (c_sample_new, jnp.stack(ns_l), jnp.stack(ms_l), jnp.swapaxes(conv_t_new, 1, 2)))


def kernel(x_prompt, x_sample, c_prompt, c_sample, state_C, state_n, state_m, state_conv, w_ada, b_ada, g_mix, w_in, b_gates, g_head, w_dw, b_dw, ln_g, ln_b, w_out, g_ffn, w_ffn13, w_ffn2, g_final):
    return _forward(x_prompt, x_sample, c_prompt, c_sample, state_C, state_n, state_m, state_conv,
                    w_ada, b_ada, g_mix, w_in, b_gates, g_head, w_dw, b_dw, ln_g, ln_b, w_out,
                    g_ffn, w_ffn13, w_ffn2, g_final, tm=256)
```

```python
import functools
import itertools

import jax
import jax.numpy as jnp
from jax import lax
from jax.experimental import pallas as pl
from jax.experimental.pallas import tpu as pltpu

F32 = jnp.float32
BF16 = jnp.bfloat16

D_MODEL = 1024
HEADS = 4
HEAD_DIM = 128
MLSTM_WIDTH = HEADS * HEAD_DIM
CONV_CH = D_MODEL - MLSTM_WIDTH
CONV_WIDTH = 31
HIST = CONV_WIDTH - 1
D_FF = 2816
CHUNK = 128
EPS = 1e-6

LANES = 128
SUBLANES = 8
MXU_N = 256

COL_Q = 0
COL_K = MLSTM_WIDTH
COL_V = 2 * MLSTM_WIDTH
COL_O = 3 * MLSTM_WIDTH
COL_A = 4 * MLSTM_WIDTH
COL_B = COL_A + CONV_CH
COL_G = COL_B + CONV_CH
PROJ_PAD = COL_G + LANES
HIST_PAD = 32
CONV_SLABS = CONV_CH // LANES
CONV_STRIDE = 2
D_OUT_PAD = D_MODEL + LANES
VMEM_LIMIT = 60 * 1024 * 1024


def _log_sigmoid(x):
    return jnp.minimum(x, 0.0) - jnp.log1p(jnp.exp(-jnp.abs(x)))


def _sigmoid(x):
    return 0.5 * jnp.tanh(0.5 * x) + 0.5


def _bdot(a, b):
    return jnp.dot(a, b, preferred_element_type=F32)


def _bdot_nt(a, b):
    return lax.dot_general(a, b, (((1,), (1,)), ((), ())), preferred_element_type=F32)


def _row_slices(rows, chunk):
    chunk = min(chunk, rows)
    assert rows % chunk == 0
    return [slice(r0, r0 + chunk) for r0 in range(0, rows, chunk)]


def _rms_mod(x, g, sc, sh):
    ms = jnp.mean(x * x, axis=-1, keepdims=True)
    return (x * lax.rsqrt(ms + EPS) * g) * (1.0 + sc) + sh


def _mod_kernel(c_ref, w_ref, b_ref, o_ref):
    o_ref[...] = _bdot(c_ref[...].astype(BF16), w_ref[...].astype(BF16)) + b_ref[...]


def _modulation(c_all, w_ada, b_ada):
    depth, d, n6 = w_ada.shape
    rows = c_all.shape[0]
    tn = 2048
    return pl.pallas_call(
        _mod_kernel,
        grid=(depth, n6 // tn),
        in_specs=[
            pl.BlockSpec((rows, d), lambda l, n: (0, 0)),
            pl.BlockSpec((None, d, tn), lambda l, n: (l, 0, n)),
            pl.BlockSpec((None, 1, tn), lambda l, n: (l, 0, n)),
        ],
        out_specs=pl.BlockSpec((None, rows, tn), lambda l, n: (l, 0, n)),
        out_shape=jax.ShapeDtypeStruct((depth, rows, n6), F32),
        compiler_params=pltpu.CompilerParams(
            dimension_semantics=("arbitrary", "arbitrary"), vmem_limit_bytes=VMEM_LIMIT),
        name="adaln_mod",
    )(c_all, w_ada, b_ada.reshape(depth, 1, n6))


def _norm_steps(src_ref, dst_ref, g_ref, mod, sc, sh, *, rows, row_chunk):
    for r in _row_slices(rows, row_chunk):
        h = _rms_mod(src_ref[r, :], g_ref[...], mod(sc, slice(None), r), mod(sh, slice(None), r))
        dst_ref[r, :] = h.astype(BF16)
        yield


def _in_proj_steps(h_ref, w_refs, p_ref):
    for w_ref, col0, width in zip(w_refs, (COL_Q, COL_A, COL_G), (COL_A, 2 * CONV_CH, LANES)):
        for n0 in range(0, width, MXU_N):
            n1 = min(n0 + MXU_N, width)
            p_ref[:, col0 + n0:col0 + n1] = _bdot(h_ref[...], w_ref[:, n0:n1])
            yield


def _out_proj_steps(x_ref, mix_ref, mod, wout_ref, x1_ref):
    for n0 in range(0, D_MODEL, MXU_N):
        cs = slice(n0, n0 + MXU_N)
        mx = _bdot(mix_ref[...], wout_ref[:, cs])
        x1_ref[:, cs] = x_ref[:, cs] + mod("ga1", cs, slice(None)) * mx
        yield


def _ffn_gate_steps(h2_ref, w13_ref, act_ref):
    for n0 in range(0, D_FF, MXU_N):
        gg = _bdot(h2_ref[...], w13_ref[:, n0:n0 + MXU_N])
        gg = gg * _sigmoid(gg)
        yield
        uu = _bdot(h2_ref[...], w13_ref[:, D_FF + n0:D_FF + n0 + MXU_N])
        act_ref[:, n0:n0 + MXU_N] = (gg * uu).astype(BF16)
        yield


def _ffn_out_steps(x1_ref, act_ref, ga2, w2_ref, gfin_ref, y_ref, *, rows, row_chunk, final):
    dst = x1_ref if final else y_ref
    for n0 in range(0, D_MODEL, MXU_N):
        cs = slice(n0, n0 + MXU_N)
        ff = _bdot(act_ref[...], w2_ref[:, cs])
        dst[:, cs] = x1_ref[:, cs] + ga2(cs) * ff
        yield
    if final:
        for r in _row_slices(rows, row_chunk):
            xb = x1_ref[r, :]
            ms = jnp.mean(xb * xb, axis=-1, keepdims=True)
            y_ref[r, :] = xb * lax.rsqrt(ms + EPS) * gfin_ref[...]
            yield


def _run(*gens):
    live = [g if isinstance(g, tuple) else (g, 1) for g in gens]
    while live:
        for entry in list(live):
            g, k = entry
            for _ in range(k):
                try:
                    next(g)
                except StopIteration:
                    live.remove(entry)
                    break


def _layernorm_silu(z, lng, lnb):
    mu = jnp.mean(z, axis=-1, keepdims=True)
    zc = z - mu
    var = jnp.mean(zc * zc, axis=-1, keepdims=True)
    zn = zc * lax.rsqrt(var + EPS) * lng + lnb
    return zn * _sigmoid(zn)


_MOD_ROW = {"sh1": 0, "sc1": 1, "ga1": 2, "sh2": 3, "sc2": 4, "ga2": 5}


def _cumsum_chunk(tril_bf, x):
    hi = x.astype(BF16)
    r1 = x - hi.astype(F32)
    mid = r1.astype(BF16)
    lo = (r1 - mid.astype(F32)).astype(BF16)
    return _bdot(tril_bf, hi) + _bdot(tril_bf, mid) + _bdot(tril_bf, lo)


def _mlstm_chunk(p_ref, r0, bg_ref, ghead_ref, mix_ref, C_ref, n_ref, m_ref, tril, tril_bf):
    rs = slice(r0, r0 + CHUNK)
    gates = p_ref[rs, COL_G:COL_G + LANES] + bg_ref[...]
    fcum = _cumsum_chunk(tril_bf, _log_sigmoid(gates))
    gates_t = gates.T
    fcum_t = fcum.T
    yield
    heads = range(HEADS)
    col = lambda base, h: slice(base + h * HEAD_DIM, base + (h + 1) * HEAD_DIM)
    st = [dict() for _ in heads]
    for h in heads:
        d = st[h]
        d["q"] = p_ref[rs, col(COL_Q, h)]
        d["k"] = p_ref[rs, col(COL_K, h)] * (HEAD_DIM ** -0.5)
        d["v_bf"] = p_ref[rs, col(COL_V, h)].astype(BF16)
        d["q_bf"] = d["q"].astype(BF16)
        d["f_col"] = fcum[:, HEADS + h:HEADS + h + 1]
        i_col = gates[:, h:h + 1]
        f_row = fcum_t[HEADS + h:HEADS + h + 1, :]
        i_row = gates_t[h:h + 1, :]
        g = f_row[:, CHUNK - 1:CHUNK]
        d["g"] = g
        d["dlog"] = jnp.where(tril, d["f_col"] - f_row + i_row, -jnp.inf)
        d["dmax"] = jnp.max(d["dlog"], axis=-1, keepdims=True)
        d["wlog"] = g - d["f_col"] + i_col
        d["wmax"] = jnp.max(g - f_row + i_row, axis=-1, keepdims=True)
        d["qk"] = _bdot_nt(d["q_bf"], d["k"].astype(BF16))
        yield
    for h in heads:
        d = st[h]
        m_prev = m_ref[h:h + 1, 0:1]
        inter = d["f_col"] + m_prev
        m_t = jnp.maximum(inter, d["dmax"])
        d["a_inter"] = jnp.exp(inter - m_t)
        d["ebound"] = jnp.exp(-m_t)
        s = d["qk"] * jnp.exp(d["dlog"] - m_t)
        d["ssum"] = jnp.sum(s, axis=-1, keepdims=True)
        d["sv"] = _bdot(s.astype(BF16), d["v_bf"])
        m_new = jnp.maximum(d["g"] + m_prev, d["wmax"])
        d["decay"] = jnp.exp(d["g"] + m_prev - m_new)
        kw = d["k"] * jnp.exp(d["wlog"] - m_new)
        d["kwsum"] = jnp.sum(kw, axis=0, keepdims=True)
        d["kv"] = _bdot(kw.T.astype(BF16), d["v_bf"])
        m_ref[h:h + 1, :] = jnp.broadcast_to(m_new, (1, LANES))
        yield
    for h in heads:
        d = st[h]
        n_row = n_ref[h:h + 1, :]
        c_mat = C_ref[h]
        num = d["sv"] + _bdot(d["q_bf"], c_mat.astype(BF16)) * d["a_inter"]
        qn = jnp.sum(d["q"] * n_row, axis=-1, keepdims=True)
        den = d["ssum"] + d["a_inter"] * qn
        bound = jnp.maximum(jnp.abs(den), d["ebound"])
        hh = num * (1.0 / bound)
        C_ref[h] = d["decay"] * c_mat + d["kv"]
        n_ref[h:h + 1, :] = d["decay"] * n_row + d["kwsum"]
        hn = hh * lax.rsqrt(jnp.mean(hh * hh, axis=-1, keepdims=True) + EPS) * ghead_ref[:, col(0, h)]
        mix_ref[rs, col(0, h)] = (hn * _sigmoid(p_ref[rs, col(COL_O, h)])).astype(BF16)
        yield


def _prompt_layer_kernel(x_ref, mod1_ref, mod2_ref, gmix_ref, wqkvo_ref, wglu_ref, wgate_ref, bg_ref,
                         ghead_ref, wdw_ref, bdw_ref, lng_ref, lnb_ref, wout_ref, gffn_ref, w13_ref,
                         w2_ref, gfin_ref,
                         y_ref, c_out, n_out, m_out, conv_out,
                         h_ref, p_ref, mix_ref, xres_ref, x1_ref, h2_ref, ga2_ref, act_ref, ubuf_ref,
                         zc_ref, C_ref, n_ref, m_ref, *, tm, tiles_per_seq, final):
    s = pl.program_id(0)
    n_tiles = pl.num_programs(0) - 2
    j = jnp.clip(s - 1, 0, n_tiles - 1) % tiles_per_seq
    last = tiles_per_seq - 1
    row_chunk = 32

    @pl.when(s == 0)
    def _():
        for ref in (p_ref, xres_ref, x1_ref, h2_ref, ga2_ref):
            ref[...] = jnp.zeros_like(ref)

    @pl.when(j == 0)
    def _():
        C_ref[...] = jnp.zeros_like(C_ref)
        n_ref[...] = jnp.zeros_like(n_ref)
        m_ref[...] = jnp.zeros_like(m_ref)
        ubuf_ref[:, 0:HIST_PAD, :] = jnp.zeros((CONV_SLABS, HIST_PAD, LANES), F32)

    @pl.when(j > 0)
    def _():
        ubuf_ref[:, 0:HIST_PAD, :] = ubuf_ref[:, tm:tm + HIST_PAD, :]

    def mod1(name, cs, r=None):
        i = _MOD_ROW[name]
        return mod1_ref[i:i + 1, cs]

    def mod2(name, cs, r=None):
        i = _MOD_ROW[name]
        return mod2_ref[i:i + 1, cs]

    row_id = lax.broadcasted_iota(jnp.int32, (CHUNK, CHUNK), 0)
    col_id = lax.broadcasted_iota(jnp.int32, (CHUNK, CHUNK), 1)
    tril = row_id >= col_id
    tril_bf = jnp.where(tril, 1.0, 0.0).astype(BF16)

    def mlstm_steps():
        for c in range(tm // CHUNK):
            yield from _mlstm_chunk(p_ref, c * CHUNK, bg_ref, ghead_ref, mix_ref, C_ref, n_ref,
                                    m_ref, tril, tril_bf)

    def glu_steps():
        for r in _row_slices(tm, row_chunk):
            ro = slice(r.start + HIST_PAD, r.stop + HIST_PAD)
            u = p_ref[r, COL_A:COL_A + CONV_CH] * _sigmoid(p_ref[r, COL_B:COL_B + CONV_CH])
            for g in range(CONV_SLABS):
                ubuf_ref[g, ro, :] = u[:, g * LANES:(g + 1) * LANES]
            yield

    def conv_steps(slabs):
        n_sets = row_chunk // SUBLANES
        firsts = [(t0 // CONV_STRIDE) * CONV_STRIDE * SUBLANES + t0 % CONV_STRIDE
                  for t0 in range(n_sets)]
        for g in slabs:
            gs = slice(g * LANES, (g + 1) * LANES)
            bias = jnp.broadcast_to(bdw_ref[:, gs], (SUBLANES, LANES))
            for base in range(0, tm, row_chunk):
                acc = [bias for _ in range(n_sets)]
                for w in range(CONV_WIDTH):
                    tap = wdw_ref[w:w + 1, gs]
                    for t0 in range(n_sets):
                        src = pl.ds(base + firsts[t0] + HIST_PAD - HIST + w, SUBLANES,
                                    stride=CONV_STRIDE)
                        acc[t0] = acc[t0] + ubuf_ref[g, src, :] * tap
                for t0 in range(n_sets):
                    zc_ref[g, pl.ds(base + firsts[t0], SUBLANES, stride=CONV_STRIDE), :] = acc[t0]
                yield

    def conv_norm_steps():
        for r in _row_slices(tm, row_chunk):
            z = jnp.concatenate([zc_ref[g, r, :] for g in range(CONV_SLABS)], axis=-1)
            mix_ref[r, MLSTM_WIDTH:] = _layernorm_silu(z, lng_ref[...], lnb_ref[...]).astype(BF16)
            yield

    def keep_steps():
        xres_ref[...] = x_ref[...]
        ga2_ref[...] = mod2("ga2", slice(None))
        yield

    _run(mlstm_steps(), _ffn_gate_steps(h2_ref, w13_ref, act_ref),
         (itertools.chain(glu_steps(), conv_steps(range(CONV_SLABS))), 2),
         _norm_steps(x_ref, h_ref, gmix_ref, mod1, "sc1", "sh1", rows=tm, row_chunk=row_chunk))

    @pl.when(jnp.logical_and(j == last, jnp.logical_and(s >= 1, s <= n_tiles)))
    def _():
        c_out[...] = C_ref[...]
        n_out[...] = n_ref[0:HEADS, :]
        m_out[...] = m_ref[0:HEADS, :]
        for g in range(CONV_SLABS):
            conv_out[:, g * LANES:(g + 1) * LANES] = ubuf_ref[g, tm + HIST_PAD - HIST:tm + HIST_PAD, :]

    @pl.when(s >= 0)
    def _():
        _run(_ffn_out_steps(x1_ref, act_ref, lambda cs: ga2_ref[:, cs], w2_ref, gfin_ref, y_ref,
                            rows=tm, row_chunk=row_chunk, final=final),
             (conv_norm_steps(), 2))

    @pl.when(s >= 0)
    def _():
        _run(_out_proj_steps(xres_ref, mix_ref, mod2, wout_ref, x1_ref))
        _run(_in_proj_steps(h_ref, (wqkvo_ref, wglu_ref, wgate_ref), p_ref),
             _norm_steps(x1_ref, h2_ref, gffn_ref, mod2, "sc2", "sh2", rows=tm, row_chunk=row_chunk))
        _run(keep_steps())


def _resident(shape):
    nd = len(shape)
    return pl.BlockSpec(shape, lambda *g: (0,) * nd, pipeline_mode=pl.Buffered(1))


def _layer_resident(layer, shape):
    nd = len(shape)
    return pl.BlockSpec((None,) + shape, lambda *g: (layer,) + (0,) * nd,
                        pipeline_mode=pl.Buffered(1))


def _prompt_layer(x, mod, layer, wl, g_final, *, tm, final):
    bsz, seq, d = x.shape
    lw = functools.partial(_layer_resident, layer)
    assert seq % tm == 0 and tm % CHUNK == 0 and tm >= HIST_PAD
    nt = seq // tm
    n_tiles = bsz * nt
    kern = functools.partial(_prompt_layer_kernel, tm=tm, tiles_per_seq=nt, final=final)

    def tile(s, lag):
        return jnp.clip(s - lag, 0, n_tiles - 1)

    def cur(s):
        return tile(s, 1)

    def prev(s):
        return tile(s, 2)

    in_specs = [
        pl.BlockSpec((None, tm, d), lambda s: (tile(s, 0) // nt, tile(s, 0) % nt, 0)),
        pl.BlockSpec((None, 6, d), lambda s: (tile(s, 0) // nt, 0, 0)),
        pl.BlockSpec((None, 6, d), lambda s: (cur(s) // nt, 0, 0)),
        lw((1, d)),
        lw((d, COL_A)),
        lw((d, D_OUT_PAD)),
        lw((d, LANES)),
        lw((1, LANES)),
        lw((1, MLSTM_WIDTH)),
        lw((CONV_WIDTH, CONV_CH)),
        lw((1, CONV_CH)),
        lw((1, CONV_CH)),
        lw((1, CONV_CH)),
        lw((d, D_OUT_PAD)),
        lw((1, d)),
        lw((d, 2 * D_FF)),
        lw((D_FF, D_OUT_PAD)),
        _resident((1, d)),
    ]
    out_specs = [
        pl.BlockSpec((None, tm, d), lambda s: (prev(s) // nt, prev(s) % nt, 0)),
        pl.BlockSpec((None, HEADS, HEAD_DIM, HEAD_DIM), lambda s: (cur(s) // nt, 0, 0, 0)),
        pl.BlockSpec((None, HEADS, HEAD_DIM), lambda s: (cur(s) // nt, 0, 0)),
        pl.BlockSpec((None, HEADS, LANES), lambda s: (cur(s) // nt, 0, 0)),
        pl.BlockSpec((None, HIST, CONV_CH), lambda s: (cur(s) // nt, 0, 0)),
    ]
    out_shape = [
        jax.ShapeDtypeStruct((bsz, seq, d), F32),
        jax.ShapeDtypeStruct((bsz, HEADS, HEAD_DIM, HEAD_DIM), F32),
        jax.ShapeDtypeStruct((bsz, HEADS, HEAD_DIM), F32),
        jax.ShapeDtypeStruct((bsz, HEADS, LANES), F32),
        jax.ShapeDtypeStruct((bsz, HIST, CONV_CH), F32),
    ]
    scratch = [
        pltpu.VMEM((tm, d), BF16),
        pltpu.VMEM((tm, PROJ_PAD), F32),
        pltpu.VMEM((tm, d), BF16),
        pltpu.VMEM((tm, d), F32),
        pltpu.VMEM((tm, d), F32),
        pltpu.VMEM((tm, d), BF16),
        pltpu.VMEM((1, d), F32),
        pltpu.VMEM((tm, D_FF), BF16),
        pltpu.VMEM((CONV_SLABS, tm + HIST_PAD, LANES), F32),
        pltpu.VMEM((CONV_SLABS, tm, LANES), F32),
        pltpu.VMEM((HEADS, HEAD_DIM, HEAD_DIM), F32),
        pltpu.VMEM((SUBLANES, HEAD_DIM), F32),
        pltpu.VMEM((SUBLANES, LANES), F32),
    ]
    return pl.pallas_call(
        kern,
        grid=(n_tiles + 2,),
        in_specs=in_specs,
        out_specs=out_specs,
        out_shape=out_shape,
        scratch_shapes=scratch,
        compiler_params=pltpu.CompilerParams(
            dimension_semantics=("arbitrary",), vmem_limit_bytes=VMEM_LIMIT),
        name="prompt_layer",
    )(x, mod, mod, wl["g_mix"], wl["w_qkvo"], wl["w_glu"], wl["w_gate"], wl["b_gates"], wl["g_head"], wl["w_dw"], wl["b_dw"],
      wl["ln_g"], wl["ln_b"], wl["w_out"], wl["g_ffn"], wl["w_ffn13"], wl["w_ffn2"], g_final)


def _sample_in_kernel(x_ref, mod_ref, gmix_ref, wqkvo_ref, wglu_ref, wgate_ref, p_ref, h_ref):
    h = _rms_mod(x_ref[...], gmix_ref[...], mod_ref[:, D_MODEL:2 * D_MODEL], mod_ref[:, 0:D_MODEL])
    h_ref[...] = h.astype(BF16)
    _run(_in_proj_steps(h_ref, (wqkvo_ref, wglu_ref, wgate_ref), p_ref))


def _sample_in_proj(x, mod, layer, wl):
    rows, d = x.shape
    lw = functools.partial(_layer_resident, layer)
    return pl.pallas_call(
        _sample_in_kernel,
        grid=(1,),
        in_specs=[_resident((rows, d)), _resident(mod.shape), lw((1, d)), lw((d, COL_A)),
                  lw((d, D_OUT_PAD)), lw((d, LANES))],
        out_specs=pl.BlockSpec((rows, PROJ_PAD), lambda i: (0, 0)),
        out_shape=jax.ShapeDtypeStruct((rows, PROJ_PAD), F32),
        scratch_shapes=[pltpu.VMEM((rows, D_MODEL), BF16)],
        compiler_params=pltpu.CompilerParams(
            dimension_semantics=("arbitrary",), vmem_limit_bytes=VMEM_LIMIT),
        name="sample_in_proj",
    )(x, mod, wl["g_mix"], wl["w_qkvo"], wl["w_glu"], wl["w_gate"])


def _sample_step_kernel(p_ref, c_ref, n_ref, m_ref, conv_ref, bg_ref, ghead_ref, wdw_ref, bdw_ref,
                        lng_ref, lnb_ref, *rest, bt, n_alias, fill_other):
    hm_ref, zc_ref, c_out, n_out, m_out, conv_out = rest[n_alias:]
    slot = pl.program_id(0)

    @pl.when(slot == 0)
    def _():
        scale = HEAD_DIM ** -0.5
        gates = p_ref[:, COL_G:COL_G + LANES] + bg_ref[...]
        logf = pltpu.roll(_log_sigmoid(gates), LANES - HEADS, axis=1)
        inter = logf + m_ref[...]
        m_t = jnp.maximum(inter, gates)
        dw_all = jnp.exp(gates - m_t)
        a_all = jnp.exp(inter - m_t)
        eb_all = jnp.exp(-m_t)
        m_out[...] = m_t
        for h in range(HEADS):
            hs = slice(h * HEAD_DIM, (h + 1) * HEAD_DIM)
            q = p_ref[:, COL_Q + h * HEAD_DIM:COL_Q + (h + 1) * HEAD_DIM]
            k = p_ref[:, COL_K + h * HEAD_DIM:COL_K + (h + 1) * HEAD_DIM] * scale
            v = p_ref[:, COL_V + h * HEAD_DIM:COL_V + (h + 1) * HEAD_DIM]
            o = p_ref[:, COL_O + h * HEAD_DIM:COL_O + (h + 1) * HEAD_DIM]
            dw = dw_all[:, h:h + 1]
            a_in = a_all[:, h:h + 1]
            n_h = n_ref[:, hs]
            kw = k * dw
            s = jnp.sum(q * k, axis=-1, keepdims=True) * dw
            qn = jnp.sum(q * n_h, axis=-1, keepdims=True)
            n_out[:, hs] = a_in * n_h + kw
            q_t = q.T
            pad_rows = jnp.zeros((HEAD_DIM - bt, HEAD_DIM), F32)
            kw_t = jnp.concatenate([kw, pad_rows], axis=0).T.astype(BF16)
            v_wide = jnp.concatenate([v] * bt, axis=1)
            own = (lax.broadcasted_iota(jnp.int32, v_wide.shape, 1) // HEAD_DIM
                   == lax.broadcasted_iota(jnp.int32, v_wide.shape, 0))
            v_bd = jnp.concatenate([jnp.where(own, v_wide, 0.0),
                                    jnp.zeros((HEAD_DIM - bt, bt * HEAD_DIM), F32)], axis=0)
            outer = _bdot(kw_t, v_bd.astype(BF16))
            a_rows = jnp.broadcast_to(a_in, (bt, HEAD_DIM))
            qc_rows = []
            for b in range(bt):
                c_mat = c_ref[b, h]
                qc_rows.append(jnp.sum(q_t[:, b:b + 1] * c_mat, axis=0, keepdims=True))
                c_out[b, h] = a_rows[b:b + 1, :] * c_mat + outer[:, b * HEAD_DIM:(b + 1) * HEAD_DIM]
            qc = jnp.concatenate(qc_rows, axis=0)
            num = s * v + qc * a_in
            den = s + a_in * qn
            bound = jnp.maximum(jnp.abs(den), eb_all[:, h:h + 1])
            hh = num * (1.0 / bound)
            hn = hh * lax.rsqrt(jnp.mean(hh * hh, axis=-1, keepdims=True) + EPS) * ghead_ref[:, hs]
            hm_ref[:, hs] = hn * _sigmoid(o)

        u = p_ref[:, COL_A:COL_A + CONV_CH] * _sigmoid(p_ref[:, COL_B:COL_B + CONV_CH])
        z = u * wdw_ref[HIST:HIST + 1, :] + bdw_ref[...]
        for w in range(HIST):
            z = z + conv_ref[w] * wdw_ref[w:w + 1, :]
        zc_ref[...] = _layernorm_silu(z, lng_ref[...], lnb_ref[...])
        conv_out[0:HIST - 1] = conv_ref[1:HIST]
        conv_out[HIST - 1] = u

    if fill_other:
        @pl.when(slot != 0)
        def _():
            c_out[...] = jnp.zeros_like(c_out)
            conv_out[...] = jnp.zeros_like(conv_out)


def _sample_step(p, layer, state_c, state_n2, state_m_pad, state_conv, stacked, wl, *, bt=8):
    rows = p.shape[0]
    depth = state_c.shape[0]
    assert rows % bt == 0
    nblk = rows // bt
    first = stacked is None
    n_slots = depth if first else 1

    def blk_i(slot, i):
        return jnp.where(slot == 0, i, nblk - 1)

    def lay(*s):
        return pl.BlockSpec((None, bt) + s, lambda slot, i: (layer, blk_i(slot, i)) + (0,) * len(s))

    def row(*s):
        return pl.BlockSpec((bt,) + s, lambda slot, i: (blk_i(slot, i),) + (0,) * len(s))

    def stk(*s):
        return pl.BlockSpec((None, bt) + s,
                            lambda slot, i: ((slot if first else layer), i) + (0,) * len(s))

    conv_in = pl.BlockSpec((None, HIST, bt, CONV_CH),
                           lambda slot, i: (layer, 0, blk_i(slot, i), 0))
    conv_stk = pl.BlockSpec((None, HIST, bt, CONV_CH),
                            lambda slot, i: ((slot if first else layer), 0, i, 0))
    lw = functools.partial(_layer_resident, layer)
    in_specs = [
        row(PROJ_PAD), lay(HEADS, HEAD_DIM, HEAD_DIM), lay(MLSTM_WIDTH), lay(LANES), conv_in,
        lw((1, LANES)), lw((1, MLSTM_WIDTH)), lw((CONV_WIDTH, CONV_CH)),
        lw((1, CONV_CH)), lw((1, CONV_CH)), lw((1, CONV_CH)),
    ]
    args = [p, state_c, state_n2, state_m_pad, state_conv, wl["b_gates"], wl["g_head"], wl["w_dw"],
            wl["b_dw"], wl["ln_g"], wl["ln_b"]]
    aliases = {}
    if not first:
        aliases = {len(args): 2, len(args) + 1: 5}
        in_specs += [pl.BlockSpec(memory_space=pl.ANY), pl.BlockSpec(memory_space=pl.ANY)]
        args += list(stacked)
    out_shape = [
        jax.ShapeDtypeStruct((rows, MLSTM_WIDTH), F32),
        jax.ShapeDtypeStruct((rows, CONV_CH), F32),
        jax.ShapeDtypeStruct(state_c.shape, F32),
        jax.ShapeDtypeStruct((rows, MLSTM_WIDTH), F32),
        jax.ShapeDtypeStruct((rows, LANES), F32),
        jax.ShapeDtypeStruct(state_conv.shape, F32),
    ]
    hm, zc, c_new, n_new, m_new, conv_new = pl.pallas_call(
        functools.partial(_sample_step_kernel, bt=bt, n_alias=len(aliases),
                          fill_other=first and depth > 1),
        grid=(n_slots, nblk),
        in_specs=in_specs,
        out_specs=[row(MLSTM_WIDTH), row(CONV_CH), stk(HEADS, HEAD_DIM, HEAD_DIM),
                   row(MLSTM_WIDTH), row(LANES), conv_stk],
        out_shape=out_shape,
        input_output_aliases=aliases,
        compiler_params=pltpu.CompilerParams(
            dimension_semantics=("arbitrary", "arbitrary"), vmem_limit_bytes=VMEM_LIMIT),
        name="sample_step",
    )(*args)
    return hm, zc, (c_new, conv_new), n_new.reshape(rows, HEADS, HEAD_DIM), m_new[:, :HEADS]


def _sample_out_kernel(x_ref, hm_ref, zc_ref, mod_ref, wout_ref, gffn_ref, w13_ref, w2_ref,
                       gfin_ref, y_ref, mix_ref, x1_ref, h2_ref, act_ref, *, rows, final):
    mix_ref[:, 0:MLSTM_WIDTH] = hm_ref[...].astype(BF16)
    mix_ref[:, MLSTM_WIDTH:] = zc_ref[...].astype(BF16)

    def mod_rows(name, cs, r=slice(None)):
        i = _MOD_ROW[name]
        if isinstance(cs, slice) and cs == slice(None):
            cs = slice(0, D_MODEL)
        return mod_ref[r, i * D_MODEL + cs.start:i * D_MODEL + cs.stop]

    _run(itertools.chain(
        _out_proj_steps(x_ref, mix_ref, mod_rows, wout_ref, x1_ref),
        _norm_steps(x1_ref, h2_ref, gffn_ref, mod_rows, "sc2", "sh2", rows=rows, row_chunk=32),
        _ffn_gate_steps(h2_ref, w13_ref, act_ref),
        _ffn_out_steps(x1_ref, act_ref, lambda cs: mod_rows("ga2", cs), w2_ref, gfin_ref, y_ref,
                       rows=rows, row_chunk=32, final=final)))


def _sample_out(x, hm, zc, mod, layer, wl, g_final, *, final):
    rows, d = x.shape
    lw = functools.partial(_layer_resident, layer)
    return pl.pallas_call(
        functools.partial(_sample_out_kernel, rows=rows, final=final),
        grid=(1,),
        in_specs=[_resident((rows, d)), _resident(hm.shape), _resident(zc.shape),
                  _resident(mod.shape), lw((d, D_OUT_PAD)), lw((1, d)), lw((d, 2 * D_FF)),
                  lw((D_FF, D_OUT_PAD)),
                  _resident((1, d))],
        out_specs=pl.BlockSpec((rows, d), lambda i: (0, 0)),
        out_shape=jax.ShapeDtypeStruct((rows, D_MODEL), F32),
        scratch_shapes=[
            pltpu.VMEM((rows, D_MODEL), BF16),
            pltpu.VMEM((rows, D_MODEL), F32),
            pltpu.VMEM((rows, D_MODEL), BF16),
            pltpu.VMEM((rows, D_FF), BF16),
        ],
        compiler_params=pltpu.CompilerParams(
            dimension_semantics=("arbitrary",), vmem_limit_bytes=VMEM_LIMIT),
        name="sample_out",
    )(x, hm, zc, mod, wl["w_out"], wl["g_ffn"], wl["w_ffn13"], wl["w_ffn2"], g_final)


def _prep_weights(g_mix, w_in, b_gates, g_head, w_dw, b_dw, ln_g, ln_b, w_out, g_ffn, w_ffn13,
                  w_ffn2):
    depth, d, _ = w_in.shape
    n_gate = 2 * HEADS
    gate0 = 4 * MLSTM_WIDTH
    lane_pad = lambda w, n: jnp.pad(w.astype(BF16), ((0, 0), (0, 0), (0, n - w.shape[2])))
    return {
        "g_mix": g_mix.reshape(depth, 1, d),
        "w_qkvo": w_in[:, :, :gate0].astype(BF16),
        "w_glu": lane_pad(w_in[:, :, gate0 + n_gate:], D_OUT_PAD),
        "w_gate": lane_pad(w_in[:, :, gate0:gate0 + n_gate], LANES),
        "b_gates": jnp.pad(b_gates, ((0, 0), (0, LANES - n_gate))).reshape(depth, 1, LANES),
        "g_head": g_head.reshape(depth, 1, MLSTM_WIDTH),
        "w_dw": w_dw,
        "b_dw": b_dw.reshape(depth, 1, CONV_CH),
        "ln_g": ln_g.reshape(depth, 1, CONV_CH),
        "ln_b": ln_b.reshape(depth, 1, CONV_CH),
        "w_out": jnp.pad(w_out.astype(BF16), ((0, 0), (0, 0), (0, D_OUT_PAD - d))),
        "g_ffn": g_ffn.reshape(depth, 1, d),
        "w_ffn13": w_ffn13.astype(BF16),
        "w_ffn2": jnp.pad(w_ffn2.astype(BF16), ((0, 0), (0, 0), (0, D_OUT_PAD - d))),
    }


def _forward(x_prompt, x_sample, c_prompt, c_sample, state_C, state_n, state_m, state_conv,
             w_ada, b_ada, g_mix, w_in, b_gates, g_head, w_dw, b_dw, ln_g, ln_b, w_out,
             g_ffn, w_ffn13, w_ffn2, g_final, *, tm):
    depth = w_ada.shape[0]
    bp = x_prompt.shape[0]
    d = x_prompt.shape[-1]
    mod = _modulation(jnp.concatenate([c_prompt, c_sample], axis=0), w_ada, b_ada)
    gfin = g_final.reshape(1, d)
    xp = x_prompt
    xs = x_sample.reshape(x_sample.shape[0], d)
    outs_p = [[], [], [], []]
    ns_l, ms_l = [], []
    stacked = None
    n_seq = xs.shape[0]
    state_n2 = state_n.reshape(depth, n_seq, MLSTM_WIDTH)
    state_m_pad = jnp.pad(state_m, ((0, 0), (0, 0), (0, LANES - HEADS)))
    state_conv_t = jnp.swapaxes(state_conv, 1, 2)
    wl = _prep_weights(g_mix, w_in, b_gates, g_head, w_dw, b_dw, ln_g, ln_b, w_out, g_ffn,
                       w_ffn13, w_ffn2)
    for l in range(depth):
        final = l == depth - 1
        mod_p = mod[l, :bp].reshape(bp, 6, d)
        mod_s = mod[l, bp:]
        xp, c1, n1, m1, b1 = _prompt_layer(xp, mod_p, l, wl, gfin, tm=tm, final=final)
        for acc, val in zip(outs_p, (c1, n1, m1[:, :, 0], b1)):
            acc.append(val)
        p = _sample_in_proj(xs, mod_s, l, wl)
        hm, zc, stacked, n2, m2 = _sample_step(p, l, state_C, state_n2, state_m_pad, state_conv_t,
                                               stacked, wl)
        xs = _sample_out(xs, hm, zc, mod_s, l, wl, gfin, final=final)
        ns_l.append(n2)
        ms_l.append(m2)
    y_sample = xs.reshape(x_sample.shape)
    c_sample_new, conv_t_new = stacked
    return ((xp, y_sample) + tuple(jnp.stack(a) for a in outs_p)
            + (c_sample_new, jnp.stack(ns_l), jnp.stack(ms_l), jnp.swapaxes(conv_t_new, 1, 2)))


def kernel(x_prompt, x_sample, c_prompt, c_sample, state_C, state_n, state_m, state_conv, w_ada, b_ada, g_mix, w_in, b_gates, g_head, w_dw, b_dw, ln_g, ln_b, w_out, g_ffn, w_ffn13, w_ffn2, g_final):
    return _forward(x_prompt, x_sample, c_prompt, c_sample, state_C, state_n, state_m, state_conv,
                    w_ada, b_ada, g_mix, w_in, b_gates, g_head, w_dw, b_dw, ln_g, ln_b, w_out,
                    g_ffn, w_ffn13, w_ffn2, g_final, tm=256)
```

```python
import functools
import itertools

import jax
import jax.numpy as jnp
from jax import lax
from jax.experimental import pallas as pl
from jax.experimental.pallas import tpu as pltpu

F32 = jnp.float32
BF16 = jnp.bfloat16

D_MODEL = 1024
HEADS = 4
HEAD_DIM = 128
MLSTM_WIDTH = HEADS * HEAD_DIM
CONV_CH = D_MODEL - MLSTM_WIDTH
CONV_WIDTH = 31
HIST = CONV_WIDTH - 1
D_FF = 2816
CHUNK = 128
EPS = 1e-6

LANES = 128
SUBLANES = 8
MXU_N = 256

COL_Q = 0
COL_K = MLSTM_WIDTH
COL_V = 2 * MLSTM_WIDTH
COL_O = 3 * MLSTM_WIDTH
COL_A = 4 * MLSTM_WIDTH
COL_B = COL_A + CONV_CH
COL_G = COL_B + CONV_CH
PROJ_PAD = COL_G + LANES
HIST_PAD = 32
CONV_SLABS = CONV_CH // LANES
CONV_STRIDE = 2
D_OUT_PAD = D_MODEL + LANES
VMEM_LIMIT = 60 * 1024 * 1024


def _log_sigmoid(x):
    return jnp.minimum(x, 0.0) - jnp.log1p(jnp.exp(-jnp.abs(x)))


def _sigmoid(x):
    return 0.5 * jnp.tanh(0.5 * x) + 0.5


def _bdot(a, b):
    return jnp.dot(a, b, preferred_element_type=F32)


def _bdot_nt(a, b):
    return lax.dot_general(a, b, (((1,), (1,)), ((), ())), preferred_element_type=F32)


def _row_slices(rows, chunk):
    chunk = min(chunk, rows)
    assert rows % chunk == 0
    return [slice(r0, r0 + chunk) for r0 in range(0, rows, chunk)]


def _rms_mod(x, g, sc, sh):
    ms = jnp.mean(x * x, axis=-1, keepdims=True)
    return (x * lax.rsqrt(ms + EPS) * g) * (1.0 + sc) + sh


def _mod_kernel(c_ref, w_ref, b_ref, o_ref):
    o_ref[...] = _bdot(c_ref[...].astype(BF16), w_ref[...].astype(BF16)) + b_ref[...]


def _modulation(c_all, w_ada, b_ada):
    depth, d, n6 = w_ada.shape
    rows = c_all.shape[0]
    tn = 2048
    return pl.pallas_call(
        _mod_kernel,
        grid=(depth, n6 // tn),
        in_specs=[
            pl.BlockSpec((rows, d), lambda l, n: (0, 0)),
            pl.BlockSpec((None, d, tn), lambda l, n: (l, 0, n)),
            pl.BlockSpec((None, 1, tn), lambda l, n: (l, 0, n)),
        ],
        out_specs=pl.BlockSpec((None, rows, tn), lambda l, n: (l, 0, n)),
        out_shape=jax.ShapeDtypeStruct((depth, rows, n6), F32),
        compiler_params=pltpu.CompilerParams(
            dimension_semantics=("arbitrary", "arbitrary"), vmem_limit_bytes=VMEM_LIMIT),
        name="adaln_mod",
    )(c_all, w_ada, b_ada.reshape(depth, 1, n6))


def _norm_steps(src_ref, dst_ref, g_ref, mod, sc, sh, *, rows, row_chunk):
    for r in _row_slices(rows, row_chunk):
        h = _rms_mod(src_ref[r, :], g_ref[...], mod(sc, slice(None), r), mod(sh, slice(None), r))
        dst_ref[r, :] = h.astype(BF16)
        yield


def _in_proj_steps(h_ref, w_refs, p_ref):
    for w_ref, col0, width in zip(w_refs, (COL_Q, COL_A, COL_G), (COL_A, 2 * CONV_CH, LANES)):
        for n0 in range(0, width, MXU_N):
            n1 = min(n0 + MXU_N, width)
            p_ref[:, col0 + n0:col0 + n1] = _bdot(h_ref[...], w_ref[:, n0:n1])
            yield


def _out_proj_steps(x_ref, mix_ref, mod, wout_ref, x1_ref):
    for n0 in range(0, D_MODEL, MXU_N):
        cs = slice(n0, n0 + MXU_N)
        mx = _bdot(mix_ref[...], wout_ref[:, cs])
        x1_ref[:, cs] = x_ref[:, cs] + mod("ga1", cs, slice(None)) * mx
        yield


def _ffn_gate_steps(h2_ref, w13_ref, act_ref):
    for n0 in range(0, D_FF, MXU_N):
        gg = _bdot(h2_ref[...], w13_ref[:, n0:n0 + MXU_N])
        gg = gg * _sigmoid(gg)
        yield
        uu = _bdot(h2_ref[...], w13_ref[:, D_FF + n0:D_FF + n0 + MXU_N])
        act_ref[:, n0:n0 + MXU_N] = (gg * uu).astype(BF16)
        yield


def _ffn_out_steps(x1_ref, act_ref, ga2, w2_ref, gfin_ref, y_ref, *, rows, row_chunk, final):
    dst = x1_ref if final else y_ref
    for n0 in range(0, D_MODEL, MXU_N):
        cs = slice(n0, n0 + MXU_N)
        ff = _bdot(act_ref[...], w2_ref[:, cs])
        dst[:, cs] = x1_ref[:, cs] + ga2(cs) * ff
        yield
    if final:
        for r in _row_slices(rows, row_chunk):
            xb = x1_ref[r, :]
            ms = jnp.mean(xb * xb, axis=-1, keepdims=True)
            y_ref[r, :] = xb * lax.rsqrt(ms + EPS) * gfin_ref[...]
            yield


def _run(*gens):
    live = [g if isinstance(g, tuple) else (g, 1) for g in gens]
    while live:
        for entry in list(live):
            g, k = entry
            for _ in range(k):
                try:
                    next(g)
                except StopIteration:
                    live.remove(entry)
                    break


def _layernorm_silu(z, lng, lnb):
    mu = jnp.mean(z, axis=-1, keepdims=True)
    zc = z - mu
    var = jnp.mean(zc * zc, axis=-1, keepdims=True)
    zn = zc * lax.rsqrt(var + EPS) * lng + lnb
    return zn * _sigmoid(zn)


_MOD_ROW = {"sh1": 0, "sc1": 1, "ga1": 2, "sh2": 3, "sc2": 4, "ga2": 5}


def _cumsum_chunk(tril_bf, x):
    hi = x.astype(BF16)
    r1 = x - hi.astype(F32)
    mid = r1.astype(BF16)
    lo = (r1 - mid.astype(F32)).astype(BF16)
    return _bdot(tril_bf, hi) + _bdot(tril_bf, mid) + _bdot(tril_bf, lo)


def _mlstm_chunk(p_ref, r0, bg_ref, ghead_ref, mix_ref, C_ref, n_ref, m_ref, tril, tril_bf):
    rs = slice(r0, r0 + CHUNK)
    gates = p_ref[rs, COL_G:COL_G + LANES] + bg_ref[...]
    fcum = _cumsum_chunk(tril_bf, _log_sigmoid(gates))
    gates_t = gates.T
    fcum_t = fcum.T
    yield
    heads = range(HEADS)
    col = lambda base, h: slice(base + h * HEAD_DIM, base + (h + 1) * HEAD_DIM)
    st = [dict() for _ in heads]
    for h in heads:
        d = st[h]
        d["q"] = p_ref[rs, col(COL_Q, h)]
        d["k"] = p_ref[rs, col(COL_K, h)] * (HEAD_DIM ** -0.5)
        d["v_bf"] = p_ref[rs, col(COL_V, h)].astype(BF16)
        d["q_bf"] = d["q"].astype(BF16)
        d["f_col"] = fcum[:, HEADS + h:HEADS + h + 1]
        f_row = fcum_t[HEADS + h:HEADS + h + 1, :]
        i_row = gates_t[h:h + 1, :]
        g = f_row[:, CHUNK - 1:CHUNK]
        d["dlog"] = jnp.where(tril, d["f_col"] - f_row + i_row, -jnp.inf)
        d["dmax"] = jnp.max(d["dlog"], axis=-1, keepdims=True)
        d["wmax"] = jnp.max(g - f_row + i_row, axis=-1, keepdims=True)
        d["qk"] = _bdot_nt(d["q_bf"], d["k"].astype(BF16))
        yield
    lane = lax.broadcasted_iota(jnp.int32, (1, LANES), 1)
    m_row = jnp.zeros((1, LANES), F32)
    wmax_row = jnp.zeros((1, LANES), F32)
    dmax_t = jnp.zeros((CHUNK, LANES), F32)
    for h in heads:
        m_row = jnp.where(lane == HEADS + h, m_ref[h:h + 1, :], m_row)
        wmax_row = jnp.where(lane == HEADS + h, st[h]["wmax"], wmax_row)
        dmax_t = jnp.where(lane == HEADS + h, st[h]["dmax"], dmax_t)
    g_row = fcum[CHUNK - 1:CHUNK, :]
    inter_t = fcum + m_row
    m_t_t = jnp.maximum(inter_t, dmax_t)
    a_inter_t = jnp.exp(inter_t - m_t_t)
    ebound_t = jnp.exp(-m_t_t)
    m_new_row = jnp.maximum(g_row + m_row, wmax_row)
    decay_row = jnp.exp(g_row + m_row - m_new_row)
    i_shift = pltpu.roll(gates, HEADS, axis=1)
    wj_t = jnp.exp(g_row - fcum + i_shift - m_new_row)
    yield
    for h in heads:
        d = st[h]
        hl = slice(HEADS + h, HEADS + h + 1)
        d["a_inter"] = a_inter_t[:, hl]
        d["ebound"] = ebound_t[:, hl]
        d["decay"] = decay_row[:, hl]
        s = d["qk"] * jnp.exp(d["dlog"] - m_t_t[:, hl])
        d["ssum"] = jnp.sum(s, axis=-1, keepdims=True)
        d["sv"] = _bdot(s.astype(BF16), d["v_bf"])
        kw = d["k"] * wj_t[:, hl]
        d["kwsum"] = jnp.sum(kw, axis=0, keepdims=True)
        d["kv"] = _bdot(kw.T.astype(BF16), d["v_bf"])
        m_ref[h:h + 1, :] = jnp.broadcast_to(m_new_row[:, hl], (1, LANES))
        yield
    for h in heads:
        d = st[h]
        n_row = n_ref[h:h + 1, :]
        c_mat = C_ref[h]
        num = d["sv"] + _bdot(d["q_bf"], c_mat.astype(BF16)) * d["a_inter"]
        qn = jnp.sum(d["q"] * n_row, axis=-1, keepdims=True)
        den = d["ssum"] + d["a_inter"] * qn
        bound = jnp.maximum(jnp.abs(den), d["ebound"])
        hh = num * (1.0 / bound)
        C_ref[h] = d["decay"] * c_mat + d["kv"]
        n_ref[h:h + 1, :] = d["decay"] * n_row + d["kwsum"]
        hn = hh * lax.rsqrt(jnp.mean(hh * hh, axis=-1, keepdims=True) + EPS) * ghead_ref[:, col(0, h)]
        mix_ref[rs, col(0, h)] = (hn * _sigmoid(p_ref[rs, col(COL_O, h)])).astype(BF16)
        yield


def _prompt_layer_kernel(x_ref, mod1_ref, mod2_ref, gmix_ref, wqkvo_ref, wglu_ref, wgate_ref, bg_ref,
                         ghead_ref, wdw_ref, bdw_ref, lng_ref, lnb_ref, wout_ref, gffn_ref, w13_ref,
                         w2_ref, gfin_ref,
                         y_ref, c_out, n_out, m_out, conv_out,
                         h_ref, p_ref, mix_ref, xres_ref, x1_ref, h2_ref, ga2_ref, act_ref, ubuf_ref,
                         zc_ref, C_ref, n_ref, m_ref, *, tm, tiles_per_seq, final):
    s = pl.program_id(0)
    n_tiles = pl.num_programs(0) - 2
    j = jnp.clip(s - 1, 0, n_tiles - 1) % tiles_per_seq
    last = tiles_per_seq - 1
    row_chunk = 32

    @pl.when(s == 0)
    def _():
        for ref in (p_ref, xres_ref, x1_ref, h2_ref, ga2_ref):
            ref[...] = jnp.zeros_like(ref)

    @pl.when(j == 0)
    def _():
        C_ref[...] = jnp.zeros_like(C_ref)
        n_ref[...] = jnp.zeros_like(n_ref)
        m_ref[...] = jnp.zeros_like(m_ref)
        ubuf_ref[:, 0:HIST_PAD, :] = jnp.zeros((CONV_SLABS, HIST_PAD, LANES), F32)

    @pl.when(j > 0)
    def _():
        ubuf_ref[:, 0:HIST_PAD, :] = ubuf_ref[:, tm:tm + HIST_PAD, :]

    def mod1(name, cs, r=None):
        i = _MOD_ROW[name]
        return mod1_ref[i:i + 1, cs]

    def mod2(name, cs, r=None):
        i = _MOD_ROW[name]
        return mod2_ref[i:i + 1, cs]

    row_id = lax.broadcasted_iota(jnp.int32, (CHUNK, CHUNK), 0)
    col_id = lax.broadcasted_iota(jnp.int32, (CHUNK, CHUNK), 1)
    tril = row_id >= col_id
    tril_bf = jnp.where(tril, 1.0, 0.0).astype(BF16)

    def mlstm_steps():
        for c in range(tm // CHUNK):
            yield from _mlstm_chunk(p_ref, c * CHUNK, bg_ref, ghead_ref, mix_ref, C_ref, n_ref,
                                    m_ref, tril, tril_bf)

    def glu_steps():
        for r in _row_slices(tm, row_chunk):
            ro = slice(r.start + HIST_PAD, r.stop + HIST_PAD)
            u = p_ref[r, COL_A:COL_A + CONV_CH] * _sigmoid(p_ref[r, COL_B:COL_B + CONV_CH])
            for g in range(CONV_SLABS):
                ubuf_ref[g, ro, :] = u[:, g * LANES:(g + 1) * LANES]
            yield

    def conv_steps(slabs):
        n_sets = row_chunk // SUBLANES
        firsts = [(t0 // CONV_STRIDE) * CONV_STRIDE * SUBLANES + t0 % CONV_STRIDE
                  for t0 in range(n_sets)]
        for g in slabs:
            gs = slice(g * LANES, (g + 1) * LANES)
            bias = jnp.broadcast_to(bdw_ref[:, gs], (SUBLANES, LANES))
            for base in range(0, tm, row_chunk):
                acc = [bias for _ in range(n_sets)]
                for w in range(CONV_WIDTH):
                    tap = wdw_ref[w:w + 1, gs]
                    for t0 in range(n_sets):
                        src = pl.ds(base + firsts[t0] + HIST_PAD - HIST + w, SUBLANES,
                                    stride=CONV_STRIDE)
                        acc[t0] = acc[t0] + ubuf_ref[g, src, :] * tap
                for t0 in range(n_sets):
                    zc_ref[g, pl.ds(base + firsts[t0], SUBLANES, stride=CONV_STRIDE), :] = acc[t0]
                yield

    def conv_norm_steps():
        for r in _row_slices(tm, row_chunk):
            z = jnp.concatenate([zc_ref[g, r, :] for g in range(CONV_SLABS)], axis=-1)
            mix_ref[r, MLSTM_WIDTH:] = _layernorm_silu(z, lng_ref[...], lnb_ref[...]).astype(BF16)
            yield

    def keep_steps():
        xres_ref[...] = x_ref[...]
        ga2_ref[...] = mod2("ga2", slice(None))
        yield

    _run(mlstm_steps(), _ffn_gate_steps(h2_ref, w13_ref, act_ref),
         (itertools.chain(glu_steps(), conv_steps(range(CONV_SLABS))), 2),
         _norm_steps(x_ref, h_ref, gmix_ref, mod1, "sc1", "sh1", rows=tm, row_chunk=row_chunk))

    @pl.when(jnp.logical_and(j == last, jnp.logical_and(s >= 1, s <= n_tiles)))
    def _():
        c_out[...] = C_ref[...]
        n_out[...] = n_ref[0:HEADS, :]
        m_out[...] = m_ref[0:HEADS, :]
        for g in range(CONV_SLABS):
            conv_out[:, g * LANES:(g + 1) * LANES] = ubuf_ref[g, tm + HIST_PAD - HIST:tm + HIST_PAD, :]

    @pl.when(s >= 0)
    def _():
        _run(_ffn_out_steps(x1_ref, act_ref, lambda cs: ga2_ref[:, cs], w2_ref, gfin_ref, y_ref,
                            rows=tm, row_chunk=row_chunk, final=final),
             (conv_norm_steps(), 2))

    @pl.when(s >= 0)
    def _():
        _run(_out_proj_steps(xres_ref, mix_ref, mod2, wout_ref, x1_ref))
        _run(_in_proj_steps(h_ref, (wqkvo_ref, wglu_ref, wgate_ref), p_ref),
             _norm_steps(x1_ref, h2_ref, gffn_ref, mod2, "sc2", "sh2", rows=tm, row_chunk=row_chunk))
        _run(keep_steps())


def _resident(shape):
    nd = len(shape)
    return pl.BlockSpec(shape, lambda *g: (0,) * nd, pipeline_mode=pl.Buffered(1))


def _layer_resident(layer, shape):
    nd = len(shape)
    return pl.BlockSpec((None,) + shape, lambda *g: (layer,) + (0,) * nd,
                        pipeline_mode=pl.Buffered(1))


def _prompt_layer(x, mod, layer, wl, g_final, *, tm, final):
    bsz, seq, d = x.shape
    lw = functools.partial(_layer_resident, layer)
    assert seq % tm == 0 and tm % CHUNK == 0 and tm >= HIST_PAD
    nt = seq // tm
    n_tiles = bsz * nt
    kern = functools.partial(_prompt_layer_kernel, tm=tm, tiles_per_seq=nt, final=final)

    def tile(s, lag):
        return jnp.clip(s - lag, 0, n_tiles - 1)

    def cur(s):
        return tile(s, 1)

    def prev(s):
        return tile(s, 2)

    in_specs = [
        pl.BlockSpec((None, tm, d), lambda s: (tile(s, 0) // nt, tile(s, 0) % nt, 0)),
        pl.BlockSpec((None, 6, d), lambda s: (tile(s, 0) // nt, 0, 0)),
        pl.BlockSpec((None, 6, d), lambda s: (cur(s) // nt, 0, 0)),
        lw((1, d)),
        lw((d, COL_A)),
        lw((d, D_OUT_PAD)),
        lw((d, LANES)),
        lw((1, LANES)),
        lw((1, MLSTM_WIDTH)),
        lw((CONV_WIDTH, CONV_CH)),
        lw((1, CONV_CH)),
        lw((1, CONV_CH)),
        lw((1, CONV_CH)),
        lw((d, D_OUT_PAD)),
        lw((1, d)),
        lw((d, 2 * D_FF)),
        lw((D_FF, D_OUT_PAD)),
        _resident((1, d)),
    ]
    out_specs = [
        pl.BlockSpec((None, tm, d), lambda s: (prev(s) // nt, prev(s) % nt, 0)),
        pl.BlockSpec((None, HEADS, HEAD_DIM, HEAD_DIM), lambda s: (cur(s) // nt, 0, 0, 0)),
        pl.BlockSpec((None, HEADS, HEAD_DIM), lambda s: (cur(s) // nt, 0, 0)),
        pl.BlockSpec((None, HEADS, LANES), lambda s: (cur(s) // nt, 0, 0)),
        pl.BlockSpec((None, HIST, CONV_CH), lambda s: (cur(s) // nt, 0, 0)),
    ]
    out_shape = [
        jax.ShapeDtypeStruct((bsz, seq, d), F32),
        jax.ShapeDtypeStruct((bsz, HEADS, HEAD_DIM, HEAD_DIM), F32),
        jax.ShapeDtypeStruct((bsz, HEADS, HEAD_DIM), F32),
        jax.ShapeDtypeStruct((bsz, HEADS, LANES), F32),
        jax.ShapeDtypeStruct((bsz, HIST, CONV_CH), F32),
    ]
    scratch = [
        pltpu.VMEM((tm, d), BF16),
        pltpu.VMEM((tm, PROJ_PAD), F32),
        pltpu.VMEM((tm, d), BF16),
        pltpu.VMEM((tm, d), F32),
        pltpu.VMEM((tm, d), F32),
        pltpu.VMEM((tm, d), BF16),
        pltpu.VMEM((1, d), F32),
        pltpu.VMEM((tm, D_FF), BF16),
        pltpu.VMEM((CONV_SLABS, tm + HIST_PAD, LANES), F32),
        pltpu.VMEM((CONV_SLABS, tm, LANES), F32),
        pltpu.VMEM((HEADS, HEAD_DIM, HEAD_DIM), F32),
        pltpu.VMEM((SUBLANES, HEAD_DIM), F32),
        pltpu.VMEM((SUBLANES, LANES), F32),
    ]
    return pl.pallas_call(
        kern,
        grid=(n_tiles + 2,),
        in_specs=in_specs,
        out_specs=out_specs,
        out_shape=out_shape,
        scratch_shapes=scratch,
        compiler_params=pltpu.CompilerParams(
            dimension_semantics=("arbitrary",), vmem_limit_bytes=VMEM_LIMIT),
        name="prompt_layer",
    )(x, mod, mod, wl["g_mix"], wl["w_qkvo"], wl["w_glu"], wl["w_gate"], wl["b_gates"], wl["g_head"], wl["w_dw"], wl["b_dw"],
      wl["ln_g"], wl["ln_b"], wl["w_out"], wl["g_ffn"], wl["w_ffn13"], wl["w_ffn2"], g_final)


def _sample_in_kernel(x_ref, mod_ref, gmix_ref, wqkvo_ref, wglu_ref, wgate_ref, p_ref, h_ref):
    h = _rms_mod(x_ref[...], gmix_ref[...], mod_ref[:, D_MODEL:2 * D_MODEL], mod_ref[:, 0:D_MODEL])
    h_ref[...] = h.astype(BF16)
    _run(_in_proj_steps(h_ref, (wqkvo_ref, wglu_ref, wgate_ref), p_ref))


def _sample_in_proj(x, mod, layer, wl):
    rows, d = x.shape
    lw = functools.partial(_layer_resident, layer)
    return pl.pallas_call(
        _sample_in_kernel,
        grid=(1,),
        in_specs=[_resident((rows, d)), _resident(mod.shape), lw((1, d)), lw((d, COL_A)),
                  lw((d, D_OUT_PAD)), lw((d, LANES))],
        out_specs=pl.BlockSpec((rows, PROJ_PAD), lambda i: (0, 0)),
        out_shape=jax.ShapeDtypeStruct((rows, PROJ_PAD), F32),
        scratch_shapes=[pltpu.VMEM((rows, D_MODEL), BF16)],
        compiler_params=pltpu.CompilerParams(
            dimension_semantics=("arbitrary",), vmem_limit_bytes=VMEM_LIMIT),
        name="sample_in_proj",
    )(x, mod, wl["g_mix"], wl["w_qkvo"], wl["w_glu"], wl["w_gate"])


def _sample_step_kernel(p_ref, c_ref, n_ref, m_ref, conv_ref, bg_ref, ghead_ref, wdw_ref, bdw_ref,
                        lng_ref, lnb_ref, *rest, bt, n_alias, fill_other):
    hm_ref, zc_ref, c_out, n_out, m_out, conv_out = rest[n_alias:]
    slot = pl.program_id(0)

    @pl.when(slot == 0)
    def _():
        scale = HEAD_DIM ** -0.5
        gates = p_ref[:, COL_G:COL_G + LANES] + bg_ref[...]
        logf = pltpu.roll(_log_sigmoid(gates), LANES - HEADS, axis=1)
        inter = logf + m_ref[...]
        m_t = jnp.maximum(inter, gates)
        dw_all = jnp.exp(gates - m_t)
        a_all = jnp.exp(inter - m_t)
        eb_all = jnp.exp(-m_t)
        m_out[...] = m_t
        for h in range(HEADS):
            hs = slice(h * HEAD_DIM, (h + 1) * HEAD_DIM)
            q = p_ref[:, COL_Q + h * HEAD_DIM:COL_Q + (h + 1) * HEAD_DIM]
            k = p_ref[:, COL_K + h * HEAD_DIM:COL_K + (h + 1) * HEAD_DIM] * scale
            v = p_ref[:, COL_V + h * HEAD_DIM:COL_V + (h + 1) * HEAD_DIM]
            o = p_ref[:, COL_O + h * HEAD_DIM:COL_O + (h + 1) * HEAD_DIM]
            dw = dw_all[:, h:h + 1]
            a_in = a_all[:, h:h + 1]
            n_h = n_ref[:, hs]
            kw = k * dw
            s = jnp.sum(q * k, axis=-1, keepdims=True) * dw
            qn = jnp.sum(q * n_h, axis=-1, keepdims=True)
            n_out[:, hs] = a_in * n_h + kw
            q_t = q.T
            pad_rows = jnp.zeros((HEAD_DIM - bt, HEAD_DIM), F32)
            kw_t = jnp.concatenate([kw, pad_rows], axis=0).T.astype(BF16)
            v_wide = jnp.concatenate([v] * bt, axis=1)
            own = (lax.broadcasted_iota(jnp.int32, v_wide.shape, 1) // HEAD_DIM
                   == lax.broadcasted_iota(jnp.int32, v_wide.shape, 0))
            v_bd = jnp.concatenate([jnp.where(own, v_wide, 0.0),
                                    jnp.zeros((HEAD_DIM - bt, bt * HEAD_DIM), F32)], axis=0)
            outer = _bdot(kw_t, v_bd.astype(BF16))
            a_rows = jnp.broadcast_to(a_in, (bt, HEAD_DIM))
            qc_rows = []
            for b in range(bt):
                c_mat = c_ref[b, h]
                qc_rows.append(jnp.sum(q_t[:, b:b + 1] * c_mat, axis=0, keepdims=True))
                c_out[b, h] = a_rows[b:b + 1, :] * c_mat + outer[:, b * HEAD_DIM:(b + 1) * HEAD_DIM]
            qc = jnp.concatenate(qc_rows, axis=0)
            num = s * v + qc * a_in
            den = s + a_in * qn
            bound = jnp.maximum(jnp.abs(den), eb_all[:, h:h + 1])
            hh = num * (1.0 / bound)
            hn = hh * lax.rsqrt(jnp.mean(hh * hh, axis=-1, keepdims=True) + EPS) * ghead_ref[:, hs]
            hm_ref[:, hs] = hn * _sigmoid(o)

        u = p_ref[:, COL_A:COL_A + CONV_CH] * _sigmoid(p_ref[:, COL_B:COL_B + CONV_CH])
        z = u * wdw_ref[HIST:HIST + 1, :] + bdw_ref[...]
        for w in range(HIST):
            z = z + conv_ref[w] * wdw_ref[w:w + 1, :]
        zc_ref[...] = _layernorm_silu(z, lng_ref[...], lnb_ref[...])
        conv_out[0:HIST - 1] = conv_ref[1:HIST]
        conv_out[HIST - 1] = u

    if fill_other:
        @pl.when(slot != 0)
        def _():
            c_out[...] = jnp.zeros_like(c_out)
            conv_out[...] = jnp.zeros_like(conv_out)


def _sample_step(p, layer, state_c, state_n2, state_m_pad, state_conv, stacked, wl, *, bt=8):
    rows = p.shape[0]
    depth = state_c.shape[0]
    assert rows % bt == 0
    nblk = rows // bt
    first = stacked is None
    n_slots = depth if first else 1

    def blk_i(slot, i):
        return jnp.where(slot == 0, i, nblk - 1)

    def lay(*s):
        return pl.BlockSpec((None, bt) + s, lambda slot, i: (layer, blk_i(slot, i)) + (0,) * len(s))

    def row(*s):
        return pl.BlockSpec((bt,) + s, lambda slot, i: (blk_i(slot, i),) + (0,) * len(s))

    def stk(*s):
        return pl.BlockSpec((None, bt) + s,
                            lambda slot, i: ((slot if first else layer), i) + (0,) * len(s))

    conv_in = pl.BlockSpec((None, HIST, bt, CONV_CH),
                           lambda slot, i: (layer, 0, blk_i(slot, i), 0))
    conv_stk = pl.BlockSpec((None, HIST, bt, CONV_CH),
                            lambda slot, i: ((slot if first else layer), 0, i, 0))
    lw = functools.partial(_layer_resident, layer)
    in_specs = [
        row(PROJ_PAD), lay(HEADS, HEAD_DIM, HEAD_DIM), lay(MLSTM_WIDTH), lay(LANES), conv_in,
        lw((1, LANES)), lw((1, MLSTM_WIDTH)), lw((CONV_WIDTH, CONV_CH)),
        lw((1, CONV_CH)), lw((1, CONV_CH)), lw((1, CONV_CH)),
    ]
    args = [p, state_c, state_n2, state_m_pad, state_conv, wl["b_gates"], wl["g_head"], wl["w_dw"],
            wl["b_dw"], wl["ln_g"], wl["ln_b"]]
    aliases = {}
    if not first:
        aliases = {len(args): 2, len(args) + 1: 5}
        in_specs += [pl.BlockSpec(memory_space=pl.ANY), pl.BlockSpec(memory_space=pl.ANY)]
        args += list(stacked)
    out_shape = [
        jax.ShapeDtypeStruct((rows, MLSTM_WIDTH), F32),
        jax.ShapeDtypeStruct((rows, CONV_CH), F32),
        jax.ShapeDtypeStruct(state_c.shape, F32),
        jax.ShapeDtypeStruct((rows, MLSTM_WIDTH), F32),
        jax.ShapeDtypeStruct((rows, LANES), F32),
        jax.ShapeDtypeStruct(state_conv.shape, F32),
    ]
    hm, zc, c_new, n_new, m_new, conv_new = pl.pallas_call(
        functools.partial(_sample_step_kernel, bt=bt, n_alias=len(aliases),
                          fill_other=first and depth > 1),
        grid=(n_slots, nblk),
        in_specs=in_specs,
        out_specs=[row(MLSTM_WIDTH), row(CONV_CH), stk(HEADS, HEAD_DIM, HEAD_DIM),
                   row(MLSTM_WIDTH), row(LANES), conv_stk],
        out_shape=out_shape,
        input_output_aliases=aliases,
        compiler_params=pltpu.CompilerParams(
            dimension_semantics=("arbitrary", "arbitrary"), vmem_limit_bytes=VMEM_LIMIT),
        name="sample_step",
    )(*args)
    return hm, zc, (c_new, conv_new), n_new.reshape(rows, HEADS, HEAD_DIM), m_new[:, :HEADS]


def _sample_out_kernel(x_ref, hm_ref, zc_ref, mod_ref, wout_ref, gffn_ref, w13_ref, w2_ref,
                       gfin_ref, y_ref, mix_ref, x1_ref, h2_ref, act_ref, *, rows, final):
    mix_ref[:, 0:MLSTM_WIDTH] = hm_ref[...].astype(BF16)
    mix_ref[:, MLSTM_WIDTH:] = zc_ref[...].astype(BF16)

    def mod_rows(name, cs, r=slice(None)):
        i = _MOD_ROW[name]
        if isinstance(cs, slice) and cs == slice(None):
            cs = slice(0, D_MODEL)
        return mod_ref[r, i * D_MODEL + cs.start:i * D_MODEL + cs.stop]

    _run(itertools.chain(
        _out_proj_steps(x_ref, mix_ref, mod_rows, wout_ref, x1_ref),
        _norm_steps(x1_ref, h2_ref, gffn_ref, mod_rows, "sc2", "sh2", rows=rows, row_chunk=32),
        _ffn_gate_steps(h2_ref, w13_ref, act_ref),
        _ffn_out_steps(x1_ref, act_ref, lambda cs: mod_rows("ga2", cs), w2_ref, gfin_ref, y_ref,
                       rows=rows, row_chunk=32, final=final)))


def _sample_out(x, hm, zc, mod, layer, wl, g_final, *, final):
    rows, d = x.shape
    lw = functools.partial(_layer_resident, layer)
    return pl.pallas_call(
        functools.partial(_sample_out_kernel, rows=rows, final=final),
        grid=(1,),
        in_specs=[_resident((rows, d)), _resident(hm.shape), _resident(zc.shape),
                  _resident(mod.shape), lw((d, D_OUT_PAD)), lw((1, d)), lw((d, 2 * D_FF)),
                  lw((D_FF, D_OUT_PAD)),
                  _resident((1, d))],
        out_specs=pl.BlockSpec((rows, d), lambda i: (0, 0)),
        out_shape=jax.ShapeDtypeStruct((rows, D_MODEL), F32),
        scratch_shapes=[
            pltpu.VMEM((rows, D_MODEL), BF16),
            pltpu.VMEM((rows, D_MODEL), F32),
            pltpu.VMEM((rows, D_MODEL), BF16),
            pltpu.VMEM((rows, D_FF), BF16),
        ],
        compiler_params=pltpu.CompilerParams(
            dimension_semantics=("arbitrary",), vmem_limit_bytes=VMEM_LIMIT),
        name="sample_out",
    )(x, hm, zc, mod, wl["w_out"], wl["g_ffn"], wl["w_ffn13"], wl["w_ffn2"], g_final)


def _prep_weights(g_mix, w_in, b_gates, g_head, w_dw, b_dw, ln_g, ln_b, w_out, g_ffn, w_ffn13,
                  w_ffn2):
    depth, d, _ = w_in.shape
    n_gate = 2 * HEADS
    gate0 = 4 * MLSTM_WIDTH
    lane_pad = lambda w, n: jnp.pad(w.astype(BF16), ((0, 0), (0, 0), (0, n - w.shape[2])))
    return {
        "g_mix": g_mix.reshape(depth, 1, d),
        "w_qkvo": w_in[:, :, :gate0].astype(BF16),
        "w_glu": lane_pad(w_in[:, :, gate0 + n_gate:], D_OUT_PAD),
        "w_gate": lane_pad(w_in[:, :, gate0:gate0 + n_gate], LANES),
        "b_gates": jnp.pad(b_gates, ((0, 0), (0, LANES - n_gate))).reshape(depth, 1, LANES),
        "g_head": g_head.reshape(depth, 1, MLSTM_WIDTH),
        "w_dw": w_dw,
        "b_dw": b_dw.reshape(depth, 1, CONV_CH),
        "ln_g": ln_g.reshape(depth, 1, CONV_CH),
        "ln_b": ln_b.reshape(depth, 1, CONV_CH),
        "w_out": jnp.pad(w_out.astype(BF16), ((0, 0), (0, 0), (0, D_OUT_PAD - d))),
        "g_ffn": g_ffn.reshape(depth, 1, d),
        "w_ffn13": w_ffn13.astype(BF16),
        "w_ffn2": jnp.pad(w_ffn2.astype(BF16), ((0, 0), (0, 0), (0, D_OUT_PAD - d))),
    }


def _forward(x_prompt, x_sample, c_prompt, c_sample, state_C, state_n, state_m, state_conv,
             w_ada, b_ada, g_mix, w_in, b_gates, g_head, w_dw, b_dw, ln_g, ln_b, w_out,
             g_ffn, w_ffn13, w_ffn2, g_final, *, tm):
    depth = w_ada.shape[0]
    bp = x_prompt.shape[0]
    d = x_prompt.shape[-1]
    mod = _modulation(jnp.concatenate([c_prompt, c_sample], axis=0), w_ada, b_ada)
    gfin = g_final.reshape(1, d)
    xp = x_prompt
    xs = x_sample.reshape(x_sample.shape[0], d)
    outs_p = [[], [], [], []]
    ns_l, ms_l = [], []
    stacked = None
    n_seq = xs.shape[0]
    state_n2 = state_n.reshape(depth, n_seq, MLSTM_WIDTH)
    state_m_pad = jnp.pad(state_m, ((0, 0), (0, 0), (0, LANES - HEADS)))
    state_conv_t = jnp.swapaxes(state_conv, 1, 2)
    wl = _prep_weights(g_mix, w_in, b_gates, g_head, w_dw, b_dw, ln_g, ln_b, w_out, g_ffn,
                       w_ffn13, w_ffn2)
    for l in range(depth):
        final = l == depth - 1
        mod_p = mod[l, :bp].reshape(bp, 6, d)
        mod_s = mod[l, bp:]
        xp, c1, n1, m1, b1 = _prompt_layer(xp, mod_p, l, wl, gfin, tm=tm, final=final)
        for acc, val in zip(outs_p, (c1, n1, m1[:, :, 0], b1)):
            acc.append(val)
        p = _sample_in_proj(xs, mod_s, l, wl)
        hm, zc, stacked, n2, m2 = _sample_step(p, l, state_C, state_n2, state_m_pad, state_conv_t,
                                               stacked, wl)
        xs = _sample_out(xs, hm, zc, mod_s, l, wl, gfin, final=final)
        ns_l.append(n2)
        ms_l.append(m2)
    y_sample = xs.reshape(x_sample.shape)
    c_sample_new, conv_t_new = stacked
    return ((xp, y_sample) + tuple(jnp.stack(a) for a in outs_p)
            + (c_sample_new, jnp.stack(ns_l), jnp.stack(ms_l), jnp.swapaxes(conv_t_new, 1, 2)))


def kernel(x_prompt, x_sample, c_prompt, c_sample, state_C, state_n, state_m, state_conv, w_ada, b_ada, g_mix, w_in, b_gates, g_head, w_dw, b_dw, ln_g, ln_b, w_out, g_ffn, w_ffn13, w_ffn2, g_final):
    return _forward(x_prompt, x_sample, c_prompt, c_sample, state_C, state_n, state_m, state_conv,
                    w_ada, b_ada, g_mix, w_in, b_gates, g_head, w_dw, b_dw, ln_g, ln_b, w_out,
                    g_ffn, w_ffn13, w_ffn2, g_final, tm=256)
```

```python
import functools
import itertools

import jax
import jax.numpy as jnp
from jax import lax
from jax.experimental import pallas as pl
from jax.experimental.pallas import tpu as pltpu

F32 = jnp.float32
BF16 = jnp.bfloat16

D_MODEL = 1024
HEADS = 4
HEAD_DIM = 128
MLSTM_WIDTH = HEADS * HEAD_DIM
CONV_CH = D_MODEL - MLSTM_WIDTH
CONV_WIDTH = 31
HIST = CONV_WIDTH - 1
D_FF = 2816
CHUNK = 128
EPS = 1e-6

LANES = 128
SUBLANES = 8
MXU_N = 256

COL_Q = 0
COL_K = MLSTM_WIDTH
COL_V = 2 * MLSTM_WIDTH
COL_O = 3 * MLSTM_WIDTH
COL_A = 4 * MLSTM_WIDTH
COL_B = COL_A + CONV_CH
COL_G = COL_B + CONV_CH
PROJ_PAD = COL_G + LANES
HIST_PAD = 32
CONV_SLABS = CONV_CH // LANES
CONV_STRIDE = 2
D_OUT_PAD = D_MODEL + LANES
VMEM_LIMIT = 60 * 1024 * 1024


def _log_sigmoid(x):
    return jnp.minimum(x, 0.0) - jnp.log1p(jnp.exp(-jnp.abs(x)))


def _sigmoid(x):
    return 0.5 * jnp.tanh(0.5 * x) + 0.5


def _silu(x):
    half = 0.5 * x
    return half * (jnp.tanh(half) + 1.0)


def _bdot(a, b):
    return jnp.dot(a, b, preferred_element_type=F32)


def _bdot_nt(a, b):
    return lax.dot_general(a, b, (((1,), (1,)), ((), ())), preferred_element_type=F32)


def _row_slices(rows, chunk):
    chunk = min(chunk, rows)
    assert rows % chunk == 0
    return [slice(r0, r0 + chunk) for r0 in range(0, rows, chunk)]


def _rms_mod(x, g, sc, sh):
    ms = jnp.mean(x * x, axis=-1, keepdims=True)
    return (x * lax.rsqrt(ms + EPS) * g) * (1.0 + sc) + sh


def _mod_kernel(c_ref, w_ref, b_ref, o_ref):
    o_ref[...] = _bdot(c_ref[...].astype(BF16), w_ref[...].astype(BF16)) + b_ref[...]


def _modulation(c_all, w_ada, b_ada):
    depth, d, n6 = w_ada.shape
    rows = c_all.shape[0]
    tn = 2048
    return pl.pallas_call(
        _mod_kernel,
        grid=(depth, n6 // tn),
        in_specs=[
            pl.BlockSpec((rows, d), lambda l, n: (0, 0)),
            pl.BlockSpec((None, d, tn), lambda l, n: (l, 0, n)),
            pl.BlockSpec((None, 1, tn), lambda l, n: (l, 0, n)),
        ],
        out_specs=pl.BlockSpec((None, rows, tn), lambda l, n: (l, 0, n)),
        out_shape=jax.ShapeDtypeStruct((depth, rows, n6), F32),
        compiler_params=pltpu.CompilerParams(
            dimension_semantics=("arbitrary", "arbitrary"), vmem_limit_bytes=VMEM_LIMIT),
        name="adaln_mod",
    )(c_all, w_ada, b_ada.reshape(depth, 1, n6))


def _norm_steps(src_ref, dst_ref, g_ref, mod, sc, sh, *, rows, row_chunk):
    for r in _row_slices(rows, row_chunk):
        h = _rms_mod(src_ref[r, :], g_ref[...], mod(sc, slice(None), r), mod(sh, slice(None), r))
        dst_ref[r, :] = h.astype(BF16)
        yield


def _in_proj_steps(h_ref, w_refs, p_ref):
    for w_ref, col0, width in zip(w_refs, (COL_Q, COL_A, COL_G), (COL_A, 2 * CONV_CH, LANES)):
        for n0 in range(0, width, MXU_N):
            n1 = min(n0 + MXU_N, width)
            p_ref[:, col0 + n0:col0 + n1] = _bdot(h_ref[...], w_ref[:, n0:n1])
            yield


def _out_proj_steps(x_ref, mix_ref, mod, wout_ref, x1_ref):
    for n0 in range(0, D_MODEL, MXU_N):
        cs = slice(n0, n0 + MXU_N)
        mx = _bdot(mix_ref[...], wout_ref[:, cs])
        x1_ref[:, cs] = x_ref[:, cs] + mod("ga1", cs, slice(None)) * mx
        yield


def _ffn_gate_steps(h2_ref, w13_ref, act_ref):
    for n0 in range(0, D_FF, MXU_N):
        gg = _bdot(h2_ref[...], w13_ref[:, n0:n0 + MXU_N])
        gg = _silu(gg)
        yield
        uu = _bdot(h2_ref[...], w13_ref[:, D_FF + n0:D_FF + n0 + MXU_N])
        act_ref[:, n0:n0 + MXU_N] = (gg * uu).astype(BF16)
        yield


def _ffn_out_steps(x1_ref, act_ref, ga2, w2_ref, gfin_ref, y_ref, *, rows, row_chunk, final):
    dst = x1_ref if final else y_ref
    for n0 in range(0, D_MODEL, MXU_N):
        cs = slice(n0, n0 + MXU_N)
        ff = _bdot(act_ref[...], w2_ref[:, cs])
        dst[:, cs] = x1_ref[:, cs] + ga2(cs) * ff
        yield
    if final:
        for r in _row_slices(rows, row_chunk):
            xb = x1_ref[r, :]
            ms = jnp.mean(xb * xb, axis=-1, keepdims=True)
            y_ref[r, :] = xb * lax.rsqrt(ms + EPS) * gfin_ref[...]
            yield


def _run(*gens):
    live = [g if isinstance(g, tuple) else (g, 1) for g in gens]
    while live:
        for entry in list(live):
            g, k = entry
            for _ in range(k):
                try:
                    next(g)
                except StopIteration:
                    live.remove(entry)
                    break


def _layernorm_silu(z, lng, lnb):
    mu = jnp.mean(z, axis=-1, keepdims=True)
    zc = z - mu
    var = jnp.mean(zc * zc, axis=-1, keepdims=True)
    zn = zc * lax.rsqrt(var + EPS) * lng + lnb
    return _silu(zn)


_MOD_ROW = {"sh1": 0, "sc1": 1, "ga1": 2, "sh2": 3, "sc2": 4, "ga2": 5}


def _cumsum_chunk(tril_bf, x):
    hi = x.astype(BF16)
    r1 = x - hi.astype(F32)
    mid = r1.astype(BF16)
    lo = (r1 - mid.astype(F32)).astype(BF16)
    return _bdot(tril_bf, hi) + _bdot(tril_bf, mid) + _bdot(tril_bf, lo)


def _mlstm_chunk(p_ref, r0, bg_ref, ghead_ref, mix_ref, C_ref, n_ref, m_ref, tril, tril_bf):
    rs = slice(r0, r0 + CHUNK)
    gates = p_ref[rs, COL_G:COL_G + LANES] + bg_ref[...]
    fcum = _cumsum_chunk(tril_bf, _log_sigmoid(gates))
    gates_t = gates.T
    fcum_t = fcum.T
    yield
    heads = range(HEADS)
    col = lambda base, h: slice(base + h * HEAD_DIM, base + (h + 1) * HEAD_DIM)
    st = [dict() for _ in heads]
    for h in heads:
        d = st[h]
        d["q"] = p_ref[rs, col(COL_Q, h)]
        d["k"] = p_ref[rs, col(COL_K, h)] * (HEAD_DIM ** -0.5)
        d["v_bf"] = p_ref[rs, col(COL_V, h)].astype(BF16)
        d["q_bf"] = d["q"].astype(BF16)
        d["f_col"] = fcum[:, HEADS + h:HEADS + h + 1]
        f_row = fcum_t[HEADS + h:HEADS + h + 1, :]
        i_row = gates_t[h:h + 1, :]
        g = f_row[:, CHUNK - 1:CHUNK]
        d["dlog"] = jnp.where(tril, d["f_col"] - f_row + i_row, -jnp.inf)
        d["dmax"] = jnp.max(d["dlog"], axis=-1, keepdims=True)
        d["wmax"] = jnp.max(g - f_row + i_row, axis=-1, keepdims=True)
        d["qk"] = _bdot_nt(d["q_bf"], d["k"].astype(BF16))
        yield
    lane = lax.broadcasted_iota(jnp.int32, (1, LANES), 1)
    m_row = jnp.zeros((1, LANES), F32)
    wmax_row = jnp.zeros((1, LANES), F32)
    dmax_t = jnp.zeros((CHUNK, LANES), F32)
    for h in heads:
        m_row = jnp.where(lane == HEADS + h, m_ref[h:h + 1, :], m_row)
        wmax_row = jnp.where(lane == HEADS + h, st[h]["wmax"], wmax_row)
        dmax_t = jnp.where(lane == HEADS + h, st[h]["dmax"], dmax_t)
    g_row = fcum[CHUNK - 1:CHUNK, :]
    inter_t = fcum + m_row
    m_t_t = jnp.maximum(inter_t, dmax_t)
    a_inter_t = jnp.exp(inter_t - m_t_t)
    ebound_t = jnp.exp(-m_t_t)
    m_new_row = jnp.maximum(g_row + m_row, wmax_row)
    decay_row = jnp.exp(g_row + m_row - m_new_row)
    i_shift = pltpu.roll(gates, HEADS, axis=1)
    wj_t = jnp.exp(g_row - fcum + i_shift - m_new_row)
    yield
    for h in heads:
        d = st[h]
        hl = slice(HEADS + h, HEADS + h + 1)
        d["a_inter"] = a_inter_t[:, hl]
        d["ebound"] = ebound_t[:, hl]
        d["decay"] = decay_row[:, hl]
        s = d["qk"] * jnp.exp(d["dlog"] - m_t_t[:, hl])
        d["ssum"] = jnp.sum(s, axis=-1, keepdims=True)
        d["sv"] = _bdot(s.astype(BF16), d["v_bf"])
        kw = d["k"] * wj_t[:, hl]
        d["kwsum"] = jnp.sum(kw, axis=0, keepdims=True)
        d["kv"] = _bdot(kw.T.astype(BF16), d["v_bf"])
        m_ref[h:h + 1, :] = jnp.broadcast_to(m_new_row[:, hl], (1, LANES))
        yield
    for h in heads:
        d = st[h]
        n_row = n_ref[h:h + 1, :]
        c_mat = C_ref[h]
        num = d["sv"] + _bdot(d["q_bf"], c_mat.astype(BF16)) * d["a_inter"]
        qn = jnp.sum(d["q"] * n_row, axis=-1, keepdims=True)
        den = d["ssum"] + d["a_inter"] * qn
        bound = jnp.maximum(jnp.abs(den), d["ebound"])
        hh = num * (1.0 / bound)
        C_ref[h] = d["decay"] * c_mat + d["kv"]
        n_ref[h:h + 1, :] = d["decay"] * n_row + d["kwsum"]
        hn = hh * lax.rsqrt(jnp.mean(hh * hh, axis=-1, keepdims=True) + EPS) * ghead_ref[:, col(0, h)]
        mix_ref[rs, col(0, h)] = (hn * _sigmoid(p_ref[rs, col(COL_O, h)])).astype(BF16)
        yield


def _prompt_layer_kernel(x_ref, mod1_ref, mod2_ref, gmix_ref, wqkvo_ref, wglu_ref, wgate_ref, bg_ref,
                         ghead_ref, wdw_ref, bdw_ref, lng_ref, lnb_ref, wout_ref, gffn_ref, w13_ref,
                         w2_ref, gfin_ref,
                         y_ref, c_out, n_out, m_out, conv_out,
                         h_ref, p_ref, mix_ref, xres_ref, x1_ref, h2_ref, ga2_ref, act_ref, ubuf_ref,
                         zc_ref, C_ref, n_ref, m_ref, *, tm, tiles_per_seq, final):
    s = pl.program_id(0)
    n_tiles = pl.num_programs(0) - 2
    j = jnp.clip(s - 1, 0, n_tiles - 1) % tiles_per_seq
    last = tiles_per_seq - 1
    row_chunk = 32

    @pl.when(s == 0)
    def _():
        for ref in (p_ref, xres_ref, x1_ref, h2_ref, ga2_ref):
            ref[...] = jnp.zeros_like(ref)

    @pl.when(j == 0)
    def _():
        C_ref[...] = jnp.zeros_like(C_ref)
        n_ref[...] = jnp.zeros_like(n_ref)
        m_ref[...] = jnp.zeros_like(m_ref)
        ubuf_ref[:, 0:HIST_PAD, :] = jnp.zeros((CONV_SLABS, HIST_PAD, LANES), F32)

    @pl.when(j > 0)
    def _():
        ubuf_ref[:, 0:HIST_PAD, :] = ubuf_ref[:, tm:tm + HIST_PAD, :]

    def mod1(name, cs, r=None):
        i = _MOD_ROW[name]
        return mod1_ref[i:i + 1, cs]

    def mod2(name, cs, r=None):
        i = _MOD_ROW[name]
        return mod2_ref[i:i + 1, cs]

    row_id = lax.broadcasted_iota(jnp.int32, (CHUNK, CHUNK), 0)
    col_id = lax.broadcasted_iota(jnp.int32, (CHUNK, CHUNK), 1)
    tril = row_id >= col_id
    tril_bf = jnp.where(tril, 1.0, 0.0).astype(BF16)

    def mlstm_steps():
        for c in range(tm // CHUNK):
            yield from _mlstm_chunk(p_ref, c * CHUNK, bg_ref, ghead_ref, mix_ref, C_ref, n_ref,
                                    m_ref, tril, tril_bf)

    def glu_steps():
        for r in _row_slices(tm, row_chunk):
            ro = slice(r.start + HIST_PAD, r.stop + HIST_PAD)
            u = p_ref[r, COL_A:COL_A + CONV_CH] * _sigmoid(p_ref[r, COL_B:COL_B + CONV_CH])
            for g in range(CONV_SLABS):
                ubuf_ref[g, ro, :] = u[:, g * LANES:(g + 1) * LANES]
            yield

    def conv_steps(slabs):
        n_sets = row_chunk // SUBLANES
        firsts = [(t0 // CONV_STRIDE) * CONV_STRIDE * SUBLANES + t0 % CONV_STRIDE
                  for t0 in range(n_sets)]
        for g in slabs:
            gs = slice(g * LANES, (g + 1) * LANES)
            bias = jnp.broadcast_to(bdw_ref[:, gs], (SUBLANES, LANES))
            for base in range(0, tm, row_chunk):
                acc = [bias for _ in range(n_sets)]
                for w in range(CONV_WIDTH):
                    tap = wdw_ref[w:w + 1, gs]
                    for t0 in range(n_sets):
                        src = pl.ds(base + firsts[t0] + HIST_PAD - HIST + w, SUBLANES,
                                    stride=CONV_STRIDE)
                        acc[t0] = acc[t0] + ubuf_ref[g, src, :] * tap
                for t0 in range(n_sets):
                    zc_ref[g, pl.ds(base + firsts[t0], SUBLANES, stride=CONV_STRIDE), :] = acc[t0]
                yield

    def conv_norm_steps():
        for r in _row_slices(tm, row_chunk):
            z = jnp.concatenate([zc_ref[g, r, :] for g in range(CONV_SLABS)], axis=-1)
            mix_ref[r, MLSTM_WIDTH:] = _layernorm_silu(z, lng_ref[...], lnb_ref[...]).astype(BF16)
            yield

    def keep_steps():
        xres_ref[...] = x_ref[...]
        ga2_ref[...] = mod2("ga2", slice(None))
        yield

    _run(mlstm_steps(), _ffn_gate_steps(h2_ref, w13_ref, act_ref),
         (itertools.chain(glu_steps(), conv_steps(range(CONV_SLABS))), 2),
         _norm_steps(x_ref, h_ref, gmix_ref, mod1, "sc1", "sh1", rows=tm, row_chunk=row_chunk))

    @pl.when(jnp.logical_and(j == last, jnp.logical_and(s >= 1, s <= n_tiles)))
    def _():
        c_out[...] = C_ref[...]
        n_out[...] = n_ref[0:HEADS, :]
        m_out[...] = m_ref[0:HEADS, :]
        for g in range(CONV_SLABS):
            conv_out[:, g * LANES:(g + 1) * LANES] = ubuf_ref[g, tm + HIST_PAD - HIST:tm + HIST_PAD, :]

    @pl.when(s >= 0)
    def _():
        _run(_ffn_out_steps(x1_ref, act_ref, lambda cs: ga2_ref[:, cs], w2_ref, gfin_ref, y_ref,
                            rows=tm, row_chunk=row_chunk, final=final),
             (conv_norm_steps(), 2))

    @pl.when(s >= 0)
    def _():
        _run(_out_proj_steps(xres_ref, mix_ref, mod2, wout_ref, x1_ref))
        _run(_in_proj_steps(h_ref, (wqkvo_ref, wglu_ref, wgate_ref), p_ref),
             _norm_steps(x1_ref, h2_ref, gffn_ref, mod2, "sc2", "sh2", rows=tm, row_chunk=row_chunk))
        _run(keep_steps())


def _resident(shape):
    nd = len(shape)
    return pl.BlockSpec(shape, lambda *g: (0,) * nd, pipeline_mode=pl.Buffered(1))


def _layer_resident(layer, shape):
    nd = len(shape)
    return pl.BlockSpec((None,) + shape, lambda *g: (layer,) + (0,) * nd,
                        pipeline_mode=pl.Buffered(1))


def _prompt_layer(x, mod, layer, wl, g_final, *, tm, final):
    bsz, seq, d = x.shape
    lw = functools.partial(_layer_resident, layer)
    assert seq % tm == 0 and tm % CHUNK == 0 and tm >= HIST_PAD
    nt = seq // tm
    n_tiles = bsz * nt
    kern = functools.partial(_prompt_layer_kernel, tm=tm, tiles_per_seq=nt, final=final)

    def tile(s, lag):
        return jnp.clip(s - lag, 0, n_tiles - 1)

    def cur(s):
        return tile(s, 1)

    def prev(s):
        return tile(s, 2)

    in_specs = [
        pl.BlockSpec((None, tm, d), lambda s: (tile(s, 0) // nt, tile(s, 0) % nt, 0)),
        pl.BlockSpec((None, 6, d), lambda s: (tile(s, 0) // nt, 0, 0)),
        pl.BlockSpec((None, 6, d), lambda s: (cur(s) // nt, 0, 0)),
        lw((1, d)),
        lw((d, COL_A)),
        lw((d, D_OUT_PAD)),
        lw((d, LANES)),
        lw((1, LANES)),
        lw((1, MLSTM_WIDTH)),
        lw((CONV_WIDTH, CONV_CH)),
        lw((1, CONV_CH)),
        lw((1, CONV_CH)),
        lw((1, CONV_CH)),
        lw((d, D_OUT_PAD)),
        lw((1, d)),
        lw((d, 2 * D_FF)),
        lw((D_FF, D_OUT_PAD)),
        _resident((1, d)),
    ]
    out_specs = [
        pl.BlockSpec((None, tm, d), lambda s: (prev(s) // nt, prev(s) % nt, 0)),
        pl.BlockSpec((None, HEADS, HEAD_DIM, HEAD_DIM), lambda s: (cur(s) // nt, 0, 0, 0)),
        pl.BlockSpec((None, HEADS, HEAD_DIM), lambda s: (cur(s) // nt, 0, 0)),
        pl.BlockSpec((None, HEADS, LANES), lambda s: (cur(s) // nt, 0, 0)),
        pl.BlockSpec((None, HIST, CONV_CH), lambda s: (cur(s) // nt, 0, 0)),
    ]
    out_shape = [
        jax.ShapeDtypeStruct((bsz, seq, d), F32),
        jax.ShapeDtypeStruct((bsz, HEADS, HEAD_DIM, HEAD_DIM), F32),
        jax.ShapeDtypeStruct((bsz, HEADS, HEAD_DIM), F32),
        jax.ShapeDtypeStruct((bsz, HEADS, LANES), F32),
        jax.ShapeDtypeStruct((bsz, HIST, CONV_CH), F32),
    ]
    scratch = [
        pltpu.VMEM((tm, d), BF16),
        pltpu.VMEM((tm, PROJ_PAD), F32),
        pltpu.VMEM((tm, d), BF16),
        pltpu.VMEM((tm, d), F32),
        pltpu.VMEM((tm, d), F32),
        pltpu.VMEM((tm, d), BF16),
        pltpu.VMEM((1, d), F32),
        pltpu.VMEM((tm, D_FF), BF16),
        pltpu.VMEM((CONV_SLABS, tm + HIST_PAD, LANES), F32),
        pltpu.VMEM((CONV_SLABS, tm, LANES), F32),
        pltpu.VMEM((HEADS, HEAD_DIM, HEAD_DIM), F32),
        pltpu.VMEM((SUBLANES, HEAD_DIM), F32),
        pltpu.VMEM((SUBLANES, LANES), F32),
    ]
    return pl.pallas_call(
        kern,
        grid=(n_tiles + 2,),
        in_specs=in_specs,
        out_specs=out_specs,
        out_shape=out_shape,
        scratch_shapes=scratch,
        compiler_params=pltpu.CompilerParams(
            dimension_semantics=("arbitrary",), vmem_limit_bytes=VMEM_LIMIT),
        name="prompt_layer",
    )(x, mod, mod, wl["g_mix"], wl["w_qkvo"], wl["w_glu"], wl["w_gate"], wl["b_gates"], wl["g_head"], wl["w_dw"], wl["b_dw"],
      wl["ln_g"], wl["ln_b"], wl["w_out"], wl["g_ffn"], wl["w_ffn13"], wl["w_ffn2"], g_final)


def _sample_in_kernel(x_ref, mod_ref, gmix_ref, wqkvo_ref, wglu_ref, wgate_ref, p_ref, h_ref):
    h = _rms_mod(x_ref[...], gmix_ref[...], mod_ref[:, D_MODEL:2 * D_MODEL], mod_ref[:, 0:D_MODEL])
    h_ref[...] = h.astype(BF16)
    _run(_in_proj_steps(h_ref, (wqkvo_ref, wglu_ref, wgate_ref), p_ref))


def _sample_in_proj(x, mod, layer, wl):
    rows, d = x.shape
    lw = functools.partial(_layer_resident, layer)
    return pl.pallas_call(
        _sample_in_kernel,
        grid=(1,),
        in_specs=[_resident((rows, d)), _resident(mod.shape), lw((1, d)), lw((d, COL_A)),
                  lw((d, D_OUT_PAD)), lw((d, LANES))],
        out_specs=pl.BlockSpec((rows, PROJ_PAD), lambda i: (0, 0)),
        out_shape=jax.ShapeDtypeStruct((rows, PROJ_PAD), F32),
        scratch_shapes=[pltpu.VMEM((rows, D_MODEL), BF16)],
        compiler_params=pltpu.CompilerParams(
            dimension_semantics=("arbitrary",), vmem_limit_bytes=VMEM_LIMIT),
        name="sample_in_proj",
    )(x, mod, wl["g_mix"], wl["w_qkvo"], wl["w_glu"], wl["w_gate"])


def _sample_step_kernel(p_ref, c_ref, n_ref, m_ref, conv_ref, bg_ref, ghead_ref, wdw_ref, bdw_ref,
                        lng_ref, lnb_ref, *rest, bt, n_alias, fill_other):
    hm_ref, zc_ref, c_out, n_out, m_out, conv_out = rest[n_alias:]
    slot = pl.program_id(0)

    @pl.when(slot == 0)
    def _():
        scale = HEAD_DIM ** -0.5
        gates = p_ref[:, COL_G:COL_G + LANES] + bg_ref[...]
        logf = pltpu.roll(_log_sigmoid(gates), LANES - HEADS, axis=1)
        inter = logf + m_ref[...]
        m_t = jnp.maximum(inter, gates)
        dw_all = jnp.exp(gates - m_t)
        a_all = jnp.exp(inter - m_t)
        eb_all = jnp.exp(-m_t)
        m_out[...] = m_t
        for h in range(HEADS):
            hs = slice(h * HEAD_DIM, (h + 1) * HEAD_DIM)
            q = p_ref[:, COL_Q + h * HEAD_DIM:COL_Q + (h + 1) * HEAD_DIM]
            k = p_ref[:, COL_K + h * HEAD_DIM:COL_K + (h + 1) * HEAD_DIM] * scale
            v = p_ref[:, COL_V + h * HEAD_DIM:COL_V + (h + 1) * HEAD_DIM]
            o = p_ref[:, COL_O + h * HEAD_DIM:COL_O + (h + 1) * HEAD_DIM]
            dw = dw_all[:, h:h + 1]
            a_in = a_all[:, h:h + 1]
            n_h = n_ref[:, hs]
            kw = k * dw
            s = jnp.sum(q * k, axis=-1, keepdims=True) * dw
            qn = jnp.sum(q * n_h, axis=-1, keepdims=True)
            n_out[:, hs] = a_in * n_h + kw
            q_t = q.T
            pad_rows = jnp.zeros((HEAD_DIM - bt, HEAD_DIM), F32)
            kw_t = jnp.concatenate([kw, pad_rows], axis=0).T.astype(BF16)
            v_wide = jnp.concatenate([v] * bt, axis=1)
            own = (lax.broadcasted_iota(jnp.int32, v_wide.shape, 1) // HEAD_DIM
                   == lax.broadcasted_iota(jnp.int32, v_wide.shape, 0))
            v_bd = jnp.concatenate([jnp.where(own, v_wide, 0.0),
                                    jnp.zeros((HEAD_DIM - bt, bt * HEAD_DIM), F32)], axis=0)
            outer = _bdot(kw_t, v_bd.astype(BF16))
            a_rows = jnp.broadcast_to(a_in, (bt, HEAD_DIM))
            qc_rows = []
            for b in range(bt):
                c_mat = c_ref[b, h]
                qc_rows.append(jnp.sum(q_t[:, b:b + 1] * c_mat, axis=0, keepdims=True))
                c_out[b, h] = a_rows[b:b + 1, :] * c_mat + outer[:, b * HEAD_DIM:(b + 1) * HEAD_DIM]
            qc = jnp.concatenate(qc_rows, axis=0)
            num = s * v + qc * a_in
            den = s + a_in * qn
            bound = jnp.maximum(jnp.abs(den), eb_all[:, h:h + 1])
            hh = num * (1.0 / bound)
            hn = hh * lax.rsqrt(jnp.mean(hh * hh, axis=-1, keepdims=True) + EPS) * ghead_ref[:, hs]
            hm_ref[:, hs] = hn * _sigmoid(o)

        u = p_ref[:, COL_A:COL_A + CONV_CH] * _sigmoid(p_ref[:, COL_B:COL_B + CONV_CH])
        z = u * wdw_ref[HIST:HIST + 1, :] + bdw_ref[...]
        for w in range(HIST):
            z = z + conv_ref[w] * wdw_ref[w:w + 1, :]
        zc_ref[...] = _layernorm_silu(z, lng_ref[...], lnb_ref[...])
        conv_out[0:HIST - 1] = conv_ref[1:HIST]
        conv_out[HIST - 1] = u

    if fill_other:
        @pl.when(slot != 0)
        def _():
            c_out[...] = jnp.zeros_like(c_out)
            conv_out[...] = jnp.zeros_like(conv_out)


def _sample_step(p, layer, state_c, state_n2, state_m_pad, state_conv, stacked, wl, *, bt=8):
    rows = p.shape[0]
    depth = state_c.shape[0]
    assert rows % bt == 0
    nblk = rows // bt
    first = stacked is None
    n_slots = depth if first else 1

    def blk_i(slot, i):
        return jnp.where(slot == 0, i, nblk - 1)

    def lay(*s):
        return pl.BlockSpec((None, bt) + s, lambda slot, i: (layer, blk_i(slot, i)) + (0,) * len(s))

    def row(*s):
        return pl.BlockSpec((bt,) + s, lambda slot, i: (blk_i(slot, i),) + (0,) * len(s))

    def stk(*s):
        return pl.BlockSpec((None, bt) + s,
                            lambda slot, i: ((slot if first else layer), i) + (0,) * len(s))

    conv_in = pl.BlockSpec((None, HIST, bt, CONV_CH),
                           lambda slot, i: (layer, 0, blk_i(slot, i), 0))
    conv_stk = pl.BlockSpec((None, HIST, bt, CONV_CH),
                            lambda slot, i: ((slot if first else layer), 0, i, 0))
    lw = functools.partial(_layer_resident, layer)
    in_specs = [
        row(PROJ_PAD), lay(HEADS, HEAD_DIM, HEAD_DIM), lay(MLSTM_WIDTH), lay(LANES), conv_in,
        lw((1, LANES)), lw((1, MLSTM_WIDTH)), lw((CONV_WIDTH, CONV_CH)),
        lw((1, CONV_CH)), lw((1, CONV_CH)), lw((1, CONV_CH)),
    ]
    args = [p, state_c, state_n2, state_m_pad, state_conv, wl["b_gates"], wl["g_head"], wl["w_dw"],
            wl["b_dw"], wl["ln_g"], wl["ln_b"]]
    aliases = {}
    if not first:
        aliases = {len(args): 2, len(args) + 1: 5}
        in_specs += [pl.BlockSpec(memory_space=pl.ANY), pl.BlockSpec(memory_space=pl.ANY)]
        args += list(stacked)
    out_shape = [
        jax.ShapeDtypeStruct((rows, MLSTM_WIDTH), F32),
        jax.ShapeDtypeStruct((rows, CONV_CH), F32),
        jax.ShapeDtypeStruct(state_c.shape, F32),
        jax.ShapeDtypeStruct((rows, MLSTM_WIDTH), F32),
        jax.ShapeDtypeStruct((rows, LANES), F32),
        jax.ShapeDtypeStruct(state_conv.shape, F32),
    ]
    hm, zc, c_new, n_new, m_new, conv_new = pl.pallas_call(
        functools.partial(_sample_step_kernel, bt=bt, n_alias=len(aliases),
                          fill_other=first and depth > 1),
        grid=(n_slots, nblk),
        in_specs=in_specs,
        out_specs=[row(MLSTM_WIDTH), row(CONV_CH), stk(HEADS, HEAD_DIM, HEAD_DIM),
                   row(MLSTM_WIDTH), row(LANES), conv_stk],
        out_shape=out_shape,
        input_output_aliases=aliases,
        compiler_params=pltpu.CompilerParams(
            dimension_semantics=("arbitrary", "arbitrary"), vmem_limit_bytes=VMEM_LIMIT),
        name="sample_step",
    )(*args)
    return hm, zc, (c_new, conv_new), n_new.reshape(rows, HEADS, HEAD_DIM), m_new[:, :HEADS]


def _sample_out_kernel(x_ref, hm_ref, zc_ref, mod_ref, wout_ref, gffn_ref, w13_ref, w2_ref,
                       gfin_ref, y_ref, mix_ref, x1_ref, h2_ref, act_ref, *, rows, final):
    mix_ref[:, 0:MLSTM_WIDTH] = hm_ref[...].astype(BF16)
    mix_ref[:, MLSTM_WIDTH:] = zc_ref[...].astype(BF16)

    def mod_rows(name, cs, r=slice(None)):
        i = _MOD_ROW[name]
        if isinstance(cs, slice) and cs == slice(None):
            cs = slice(0, D_MODEL)
        return mod_ref[r, i * D_MODEL + cs.start:i * D_MODEL + cs.stop]

    _run(itertools.chain(
        _out_proj_steps(x_ref, mix_ref, mod_rows, wout_ref, x1_ref),
        _norm_steps(x1_ref, h2_ref, gffn_ref, mod_rows, "sc2", "sh2", rows=rows, row_chunk=32),
        _ffn_gate_steps(h2_ref, w13_ref, act_ref),
        _ffn_out_steps(x1_ref, act_ref, lambda cs: mod_rows("ga2", cs), w2_ref, gfin_ref, y_ref,
                       rows=rows, row_chunk=32, final=final)))


def _sample_out(x, hm, zc, mod, layer, wl, g_final, *, final):
    rows, d = x.shape
    lw = functools.partial(_layer_resident, layer)
    return pl.pallas_call(
        functools.partial(_sample_out_kernel, rows=rows, final=final),
        grid=(1,),
        in_specs=[_resident((rows, d)), _resident(hm.shape), _resident(zc.shape),
                  _resident(mod.shape), lw((d, D_OUT_PAD)), lw((1, d)), lw((d, 2 * D_FF)),
                  lw((D_FF, D_OUT_PAD)),
                  _resident((1, d))],
        out_specs=pl.BlockSpec((rows, d), lambda i: (0, 0)),
        out_shape=jax.ShapeDtypeStruct((rows, D_MODEL), F32),
        scratch_shapes=[
            pltpu.VMEM((rows, D_MODEL), BF16),
            pltpu.VMEM((rows, D_MODEL), F32),
            pltpu.VMEM((rows, D_MODEL), BF16),
            pltpu.VMEM((rows, D_FF), BF16),
        ],
        compiler_params=pltpu.CompilerParams(
            dimension_semantics=("arbitrary",), vmem_limit_bytes=VMEM_LIMIT),
        name="sample_out",
    )(x, hm, zc, mod, wl["w_out"], wl["g_ffn"], wl["w_ffn13"], wl["w_ffn2"], g_final)


def _prep_weights(g_mix, w_in, b_gates, g_head, w_dw, b_dw, ln_g, ln_b, w_out, g_ffn, w_ffn13,
                  w_ffn2):
    depth, d, _ = w_in.shape
    n_gate = 2 * HEADS
    gate0 = 4 * MLSTM_WIDTH
    lane_pad = lambda w, n: jnp.pad(w.astype(BF16), ((0, 0), (0, 0), (0, n - w.shape[2])))
    return {
        "g_mix": g_mix.reshape(depth, 1, d),
        "w_qkvo": w_in[:, :, :gate0].astype(BF16),
        "w_glu": lane_pad(w_in[:, :, gate0 + n_gate:], D_OUT_PAD),
        "w_gate": lane_pad(w_in[:, :, gate0:gate0 + n_gate], LANES),
        "b_gates": jnp.pad(b_gates, ((0, 0), (0, LANES - n_gate))).reshape(depth, 1, LANES),
        "g_head": g_head.reshape(depth, 1, MLSTM_WIDTH),
        "w_dw": w_dw,
        "b_dw": b_dw.reshape(depth, 1, CONV_CH),
        "ln_g": ln_g.reshape(depth, 1, CONV_CH),
        "ln_b": ln_b.reshape(depth, 1, CONV_CH),
        "w_out": jnp.pad(w_out.astype(BF16), ((0, 0), (0, 0), (0, D_OUT_PAD - d))),
        "g_ffn": g_ffn.reshape(depth, 1, d),
        "w_ffn13": w_ffn13.astype(BF16),
        "w_ffn2": jnp.pad(w_ffn2.astype(BF16), ((0, 0), (0, 0), (0, D_OUT_PAD - d))),
    }


def _forward(x_prompt, x_sample, c_prompt, c_sample, state_C, state_n, state_m, state_conv,
             w_ada, b_ada, g_mix, w_in, b_gates, g_head, w_dw, b_dw, ln_g, ln_b, w_out,
             g_ffn, w_ffn13, w_ffn2, g_final, *, tm):
    depth = w_ada.shape[0]
    bp = x_prompt.shape[0]
    d = x_prompt.shape[-1]
    mod = _modulation(jnp.concatenate([c_prompt, c_sample], axis=0), w_ada, b_ada)
    gfin = g_final.reshape(1, d)
    xp = x_prompt
    xs = x_sample.reshape(x_sample.shape[0], d)
    outs_p = [[], [], [], []]
    ns_l, ms_l = [], []
    stacked = None
    n_seq = xs.shape[0]
    state_n2 = state_n.reshape(depth, n_seq, MLSTM_WIDTH)
    state_m_pad = jnp.pad(state_m, ((0, 0), (0, 0), (0, LANES - HEADS)))
    state_conv_t = jnp.swapaxes(state_conv, 1, 2)
    wl = _prep_weights(g_mix, w_in, b_gates, g_head, w_dw, b_dw, ln_g, ln_b, w_out, g_ffn,
                       w_ffn13, w_ffn2)
    for l in range(depth):
        final = l == depth - 1
        mod_p = mod[l, :bp].reshape(bp, 6, d)
        mod_s = mod[l, bp:]
        xp, c1, n1, m1, b1 = _prompt_layer(xp, mod_p, l, wl, gfin, tm=tm, final=final)
        for acc, val in zip(outs_p, (c1, n1, m1[:, :, 0], b1)):
            acc.append(val)
        p = _sample_in_proj(xs, mod_s, l, wl)
        hm, zc, stacked, n2, m2 = _sample_step(p, l, state_C, state_n2, state_m_pad, state_conv_t,
                                               stacked, wl)
        xs = _sample_out(xs, hm, zc, mod_s, l, wl, gfin, final=final)
        ns_l.append(n2)
        ms_l.append(m2)
    y_sample = xs.reshape(x_sample.shape)
    c_sample_new, conv_t_new = stacked
    return ((xp, y_sample) + tuple(jnp.stack(a) for a in outs_p)
            + (c_sample_new, jnp.stack(ns_l), jnp.stack(ms_l), jnp.swapaxes(conv_t_new, 1, 2)))


def kernel(x_prompt, x_sample, c_prompt, c_sample, state_C, state_n, state_m, state_conv, w_ada, b_ada, g_mix, w_in, b_gates, g_head, w_dw, b_dw, ln_g, ln_b, w_out, g_ffn, w_ffn13, w_ffn2, g_final):
    return _forward(x_prompt, x_sample, c_prompt, c_sample, state_C, state_n, state_m, state_conv,
                    w_ada, b_ada, g_mix, w_in, b_gates, g_head, w_dw, b_dw, ln_g, ln_b, w_out,
                    g_ffn, w_ffn13, w_ffn2, g_final, tm=256)
```

```python
import functools
import itertools

import jax
import jax.numpy as jnp
from jax import lax
from jax.experimental import pallas as pl
from jax.experimental.pallas import tpu as pltpu

F32 = jnp.float32
BF16 = jnp.bfloat16

D_MODEL = 1024
HEADS = 4
HEAD_DIM = 128
MLSTM_WIDTH = HEADS * HEAD_DIM
CONV_CH = D_MODEL - MLSTM_WIDTH
CONV_WIDTH = 31
HIST = CONV_WIDTH - 1
D_FF = 2816
CHUNK = 128
EPS = 1e-6

LANES = 128
SUBLANES = 8
MXU_N = 256

COL_Q = 0
COL_K = MLSTM_WIDTH
COL_V = 2 * MLSTM_WIDTH
COL_O = 3 * MLSTM_WIDTH
COL_A = 4 * MLSTM_WIDTH
COL_B = COL_A + CONV_CH
COL_G = COL_B + CONV_CH
PROJ_PAD = COL_G + LANES
HIST_PAD = 32
CONV_SLABS = CONV_CH // LANES
CONV_STRIDE = 2
D_OUT_PAD = D_MODEL + LANES
VMEM_LIMIT = 60 * 1024 * 1024


def _log_sigmoid(x):
    return jnp.minimum(x, 0.0) - jnp.log1p(jnp.exp(-jnp.abs(x)))


def _sigmoid(x):
    return 0.5 * jnp.tanh(0.5 * x) + 0.5


def _silu(x):
    half = 0.5 * x
    return half * (jnp.tanh(half) + 1.0)


def _bdot(a, b):
    return jnp.dot(a, b, preferred_element_type=F32)


def _bdot_nt(a, b):
    return lax.dot_general(a, b, (((1,), (1,)), ((), ())), preferred_element_type=F32)


def _row_slices(rows, chunk):
    chunk = min(chunk, rows)
    assert rows % chunk == 0
    return [slice(r0, r0 + chunk) for r0 in range(0, rows, chunk)]


def _rms_mod(x, g, sc, sh):
    ms = jnp.mean(x * x, axis=-1, keepdims=True)
    return (x * lax.rsqrt(ms + EPS) * g) * (1.0 + sc) + sh


def _mod_kernel(c_ref, w_ref, b_ref, o_ref):
    o_ref[...] = _bdot(c_ref[...].astype(BF16), w_ref[...].astype(BF16)) + b_ref[...]


def _modulation(c_all, w_ada, b_ada):
    depth, d, n6 = w_ada.shape
    rows = c_all.shape[0]
    tn = 2048
    return pl.pallas_call(
        _mod_kernel,
        grid=(depth, n6 // tn),
        in_specs=[
            pl.BlockSpec((rows, d), lambda l, n: (0, 0)),
            pl.BlockSpec((None, d, tn), lambda l, n: (l, 0, n)),
            pl.BlockSpec((None, 1, tn), lambda l, n: (l, 0, n)),
        ],
        out_specs=pl.BlockSpec((None, rows, tn), lambda l, n: (l, 0, n)),
        out_shape=jax.ShapeDtypeStruct((depth, rows, n6), F32),
        compiler_params=pltpu.CompilerParams(
            dimension_semantics=("arbitrary", "arbitrary"), vmem_limit_bytes=VMEM_LIMIT),
        name="adaln_mod",
    )(c_all, w_ada, b_ada.reshape(depth, 1, n6))


def _norm_steps(src_ref, dst_ref, g_ref, mod, sc, sh, *, rows, row_chunk):
    for r in _row_slices(rows, row_chunk):
        h = _rms_mod(src_ref[r, :], g_ref[...], mod(sc, slice(None), r), mod(sh, slice(None), r))
        dst_ref[r, :] = h.astype(BF16)
        yield


def _in_proj_steps(h_ref, w_refs, p_ref):
    for w_ref, col0, width in zip(w_refs, (COL_Q, COL_A, COL_G), (COL_A, 2 * CONV_CH, LANES)):
        for n0 in range(0, width, MXU_N):
            n1 = min(n0 + MXU_N, width)
            p_ref[:, col0 + n0:col0 + n1] = _bdot(h_ref[...], w_ref[:, n0:n1])
            yield


def _out_proj_steps(x_ref, mix_ref, mod, wout_ref, x1_ref):
    for n0 in range(0, D_MODEL, MXU_N):
        cs = slice(n0, n0 + MXU_N)
        mx = _bdot(mix_ref[...], wout_ref[:, cs])
        x1_ref[:, cs] = x_ref[:, cs] + mod("ga1", cs, slice(None)) * mx
        yield


def _ffn_gate_steps(h2_ref, w13_ref, act_ref):
    for n0 in range(0, D_FF, MXU_N):
        gg = _bdot(h2_ref[...], w13_ref[:, n0:n0 + MXU_N])
        gg = _silu(gg)
        yield
        uu = _bdot(h2_ref[...], w13_ref[:, D_FF + n0:D_FF + n0 + MXU_N])
        act_ref[:, n0:n0 + MXU_N] = (gg * uu).astype(BF16)
        yield


def _ffn_out_steps(x1_ref, act_ref, ga2, w2_ref, gfin_ref, y_ref, *, rows, row_chunk, final):
    dst = x1_ref if final else y_ref
    for n0 in range(0, D_MODEL, MXU_N):
        cs = slice(n0, n0 + MXU_N)
        ff = _bdot(act_ref[...], w2_ref[:, cs])
        dst[:, cs] = x1_ref[:, cs] + ga2(cs) * ff
        yield
    if final:
        for r in _row_slices(rows, row_chunk):
            xb = x1_ref[r, :]
            ms = jnp.mean(xb * xb, axis=-1, keepdims=True)
            y_ref[r, :] = xb * lax.rsqrt(ms + EPS) * gfin_ref[...]
            yield


def _run(*gens):
    live = [g if isinstance(g, tuple) else (g, 1) for g in gens]
    while live:
        for entry in list(live):
            g, k = entry
            for _ in range(k):
                try:
                    next(g)
                except StopIteration:
                    live.remove(entry)
                    break


def _layernorm_silu(z, lng, lnb):
    mu = jnp.mean(z, axis=-1, keepdims=True)
    zc = z - mu
    var = jnp.mean(zc * zc, axis=-1, keepdims=True)
    zn = zc * lax.rsqrt(var + EPS) * lng + lnb
    return _silu(zn)


_MOD_ROW = {"sh1": 0, "sc1": 1, "ga1": 2, "sh2": 3, "sc2": 4, "ga2": 5}


def _cumsum_chunk(tril_bf, x):
    hi = x.astype(BF16)
    r1 = x - hi.astype(F32)
    mid = r1.astype(BF16)
    lo = (r1 - mid.astype(F32)).astype(BF16)
    return _bdot(tril_bf, hi) + _bdot(tril_bf, mid) + _bdot(tril_bf, lo)


def _mlstm_chunk(p_ref, r0, bg_ref, ghead_ref, mix_ref, C_ref, n_ref, m_ref, tril, tril_bf):
    rs = slice(r0, r0 + CHUNK)
    gates = p_ref[rs, COL_G:COL_G + LANES] + bg_ref[...]
    fcum = _cumsum_chunk(tril_bf, _log_sigmoid(gates))
    gates_t = gates.T
    fcum_t = fcum.T
    yield
    heads = range(HEADS)
    col = lambda base, h: slice(base + h * HEAD_DIM, base + (h + 1) * HEAD_DIM)
    st = [dict() for _ in heads]
    for h in heads:
        d = st[h]
        d["q"] = p_ref[rs, col(COL_Q, h)]
        d["k"] = p_ref[rs, col(COL_K, h)] * (HEAD_DIM ** -0.5)
        d["v_bf"] = p_ref[rs, col(COL_V, h)].astype(BF16)
        d["q_bf"] = d["q"].astype(BF16)
        d["f_col"] = fcum[:, HEADS + h:HEADS + h + 1]
        f_row = fcum_t[HEADS + h:HEADS + h + 1, :]
        i_row = gates_t[h:h + 1, :]
        g = f_row[:, CHUNK - 1:CHUNK]
        d["dlog"] = jnp.where(tril, d["f_col"] - f_row + i_row, -jnp.inf)
        d["dmax"] = jnp.max(d["dlog"], axis=-1, keepdims=True)
        d["wmax"] = jnp.max(g - f_row + i_row, axis=-1, keepdims=True)
        d["qk"] = _bdot_nt(d["q_bf"], d["k"].astype(BF16))
        yield
    lane = lax.broadcasted_iota(jnp.int32, (1, LANES), 1)
    m_row = jnp.zeros((1, LANES), F32)
    wmax_row = jnp.zeros((1, LANES), F32)
    dmax_t = jnp.zeros((CHUNK, LANES), F32)
    for h in heads:
        m_row = jnp.where(lane == HEADS + h, m_ref[h:h + 1, :], m_row)
        wmax_row = jnp.where(lane == HEADS + h, st[h]["wmax"], wmax_row)
        dmax_t = jnp.where(lane == HEADS + h, st[h]["dmax"], dmax_t)
    g_row = fcum[CHUNK - 1:CHUNK, :]
    inter_t = fcum + m_row
    m_t_t = jnp.maximum(inter_t, dmax_t)
    a_inter_t = jnp.exp(inter_t - m_t_t)
    ebound_t = jnp.exp(-m_t_t)
    m_new_row = jnp.maximum(g_row + m_row, wmax_row)
    decay_row = jnp.exp(g_row + m_row - m_new_row)
    i_shift = pltpu.roll(gates, HEADS, axis=1)
    wj_t = jnp.exp(g_row - fcum + i_shift - m_new_row)
    yield
    for h in heads:
        d = st[h]
        hl = slice(HEADS + h, HEADS + h + 1)
        d["a_inter"] = a_inter_t[:, hl]
        d["decay"] = decay_row[:, hl]
        s = d["qk"] * jnp.exp(d["dlog"] - m_t_t[:, hl])
        d["ssum"] = jnp.sum(s, axis=-1, keepdims=True)
        d["sv"] = _bdot(s.astype(BF16), d["v_bf"])
        kw = d["k"] * wj_t[:, hl]
        d["kwsum"] = jnp.sum(kw, axis=0, keepdims=True)
        d["kv"] = _bdot(kw.T.astype(BF16), d["v_bf"])
        m_ref[h:h + 1, :] = jnp.broadcast_to(m_new_row[:, hl], (1, LANES))
        yield
    qn_t = jnp.zeros((CHUNK, LANES), F32)
    ssum_t = jnp.zeros((CHUNK, LANES), F32)
    for h in heads:
        qn = jnp.sum(st[h]["q"] * n_ref[h:h + 1, :], axis=-1, keepdims=True)
        qn_t = jnp.where(lane == HEADS + h, qn, qn_t)
        ssum_t = jnp.where(lane == HEADS + h, st[h]["ssum"], ssum_t)
    den_t = ssum_t + a_inter_t * qn_t
    rbound_t = 1.0 / jnp.maximum(jnp.abs(den_t), ebound_t)
    yield
    for h in heads:
        d = st[h]
        n_row = n_ref[h:h + 1, :]
        c_mat = C_ref[h]
        num = d["sv"] + _bdot(d["q_bf"], c_mat.astype(BF16)) * d["a_inter"]
        hh = num * rbound_t[:, HEADS + h:HEADS + h + 1]
        C_ref[h] = d["decay"] * c_mat + d["kv"]
        n_ref[h:h + 1, :] = d["decay"] * n_row + d["kwsum"]
        hn = hh * lax.rsqrt(jnp.mean(hh * hh, axis=-1, keepdims=True) + EPS) * ghead_ref[:, col(0, h)]
        mix_ref[rs, col(0, h)] = (hn * _sigmoid(p_ref[rs, col(COL_O, h)])).astype(BF16)
        yield


def _prompt_layer_kernel(x_ref, mod1_ref, mod2_ref, gmix_ref, wqkvo_ref, wglu_ref, wgate_ref, bg_ref,
                         ghead_ref, wdw_ref, bdw_ref, lng_ref, lnb_ref, wout_ref, gffn_ref, w13_ref,
                         w2_ref, gfin_ref,
                         y_ref, c_out, n_out, m_out, conv_out,
                         h_ref, p_ref, mix_ref, xres_ref, x1_ref, h2_ref, ga2_ref, act_ref, ubuf_ref,
                         zc_ref, C_ref, n_ref, m_ref, *, tm, tiles_per_seq, final):
    s = pl.program_id(0)
    n_tiles = pl.num_programs(0) - 2
    j = jnp.clip(s - 1, 0, n_tiles - 1) % tiles_per_seq
    last = tiles_per_seq - 1
    row_chunk = 32

    @pl.when(s == 0)
    def _():
        for ref in (p_ref, xres_ref, x1_ref, h2_ref, ga2_ref):
            ref[...] = jnp.zeros_like(ref)

    @pl.when(j == 0)
    def _():
        C_ref[...] = jnp.zeros_like(C_ref)
        n_ref[...] = jnp.zeros_like(n_ref)
        m_ref[...] = jnp.zeros_like(m_ref)
        ubuf_ref[:, 0:HIST_PAD, :] = jnp.zeros((CONV_SLABS, HIST_PAD, LANES), F32)

    @pl.when(j > 0)
    def _():
        ubuf_ref[:, 0:HIST_PAD, :] = ubuf_ref[:, tm:tm + HIST_PAD, :]

    def mod1(name, cs, r=None):
        i = _MOD_ROW[name]
        return mod1_ref[i:i + 1, cs]

    def mod2(name, cs, r=None):
        i = _MOD_ROW[name]
        return mod2_ref[i:i + 1, cs]

    row_id = lax.broadcasted_iota(jnp.int32, (CHUNK, CHUNK), 0)
    col_id = lax.broadcasted_iota(jnp.int32, (CHUNK, CHUNK), 1)
    tril = row_id >= col_id
    tril_bf = jnp.where(tril, 1.0, 0.0).astype(BF16)

    def mlstm_steps():
        for c in range(tm // CHUNK):
            yield from _mlstm_chunk(p_ref, c * CHUNK, bg_ref, ghead_ref, mix_ref, C_ref, n_ref,
                                    m_ref, tril, tril_bf)

    def glu_steps():
        for r in _row_slices(tm, row_chunk):
            ro = slice(r.start + HIST_PAD, r.stop + HIST_PAD)
            u = p_ref[r, COL_A:COL_A + CONV_CH] * _sigmoid(p_ref[r, COL_B:COL_B + CONV_CH])
            for g in range(CONV_SLABS):
                ubuf_ref[g, ro, :] = u[:, g * LANES:(g + 1) * LANES]
            yield

    def conv_steps(slabs):
        n_sets = row_chunk // SUBLANES
        firsts = [(t0 // CONV_STRIDE) * CONV_STRIDE * SUBLANES + t0 % CONV_STRIDE
                  for t0 in range(n_sets)]
        for g in slabs:
            gs = slice(g * LANES, (g + 1) * LANES)
            bias = jnp.broadcast_to(bdw_ref[:, gs], (SUBLANES, LANES))
            for base in range(0, tm, row_chunk):
                acc = [bias for _ in range(n_sets)]
                for w in range(CONV_WIDTH):
                    tap = wdw_ref[w:w + 1, gs]
                    for t0 in range(n_sets):
                        src = pl.ds(base + firsts[t0] + HIST_PAD - HIST + w, SUBLANES,
                                    stride=CONV_STRIDE)
                        acc[t0] = acc[t0] + ubuf_ref[g, src, :] * tap
                for t0 in range(n_sets):
                    zc_ref[g, pl.ds(base + firsts[t0], SUBLANES, stride=CONV_STRIDE), :] = acc[t0]
                yield

    def conv_norm_steps():
        for r in _row_slices(tm, row_chunk):
            z = jnp.concatenate([zc_ref[g, r, :] for g in range(CONV_SLABS)], axis=-1)
            mix_ref[r, MLSTM_WIDTH:] = _layernorm_silu(z, lng_ref[...], lnb_ref[...]).astype(BF16)
            yield

    def keep_steps():
        xres_ref[...] = x_ref[...]
        ga2_ref[...] = mod2("ga2", slice(None))
        yield

    _run(mlstm_steps(), _ffn_gate_steps(h2_ref, w13_ref, act_ref),
         (itertools.chain(glu_steps(), conv_steps(range(CONV_SLABS))), 2),
         _norm_steps(x_ref, h_ref, gmix_ref, mod1, "sc1", "sh1", rows=tm, row_chunk=row_chunk))

    @pl.when(jnp.logical_and(j == last, jnp.logical_and(s >= 1, s <= n_tiles)))
    def _():
        c_out[...] = C_ref[...]
        n_out[...] = n_ref[0:HEADS, :]
        m_out[...] = m_ref[0:HEADS, :]
        for g in range(CONV_SLABS):
            conv_out[:, g * LANES:(g + 1) * LANES] = ubuf_ref[g, tm + HIST_PAD - HIST:tm + HIST_PAD, :]

    @pl.when(s >= 0)
    def _():
        _run(_ffn_out_steps(x1_ref, act_ref, lambda cs: ga2_ref[:, cs], w2_ref, gfin_ref, y_ref,
                            rows=tm, row_chunk=row_chunk, final=final),
             (conv_norm_steps(), 2))

    @pl.when(s >= 0)
    def _():
        _run(_out_proj_steps(xres_ref, mix_ref, mod2, wout_ref, x1_ref))
        _run(_in_proj_steps(h_ref, (wqkvo_ref, wglu_ref, wgate_ref), p_ref),
             _norm_steps(x1_ref, h2_ref, gffn_ref, mod2, "sc2", "sh2", rows=tm, row_chunk=row_chunk))
        _run(keep_steps())


def _resident(shape):
    nd = len(shape)
    return pl.BlockSpec(shape, lambda *g: (0,) * nd, pipeline_mode=pl.Buffered(1))


def _layer_resident(layer, shape):
    nd = len(shape)
    return pl.BlockSpec((None,) + shape, lambda *g: (layer,) + (0,) * nd,
                        pipeline_mode=pl.Buffered(1))


def _prompt_layer(x, mod, layer, wl, g_final, *, tm, final):
    bsz, seq, d = x.shape
    lw = functools.partial(_layer_resident, layer)
    assert seq % tm == 0 and tm % CHUNK == 0 and tm >= HIST_PAD
    nt = seq // tm
    n_tiles = bsz * nt
    kern = functools.partial(_prompt_layer_kernel, tm=tm, tiles_per_seq=nt, final=final)

    def tile(s, lag):
        return jnp.clip(s - lag, 0, n_tiles - 1)

    def cur(s):
        return tile(s, 1)

    def prev(s):
        return tile(s, 2)

    in_specs = [
        pl.BlockSpec((None, tm, d), lambda s: (tile(s, 0) // nt, tile(s, 0) % nt, 0)),
        pl.BlockSpec((None, 6, d), lambda s: (tile(s, 0) // nt, 0, 0)),
        pl.BlockSpec((None, 6, d), lambda s: (cur(s) // nt, 0, 0)),
        lw((1, d)),
        lw((d, COL_A)),
        lw((d, D_OUT_PAD)),
        lw((d, LANES)),
        lw((1, LANES)),
        lw((1, MLSTM_WIDTH)),
        lw((CONV_WIDTH, CONV_CH)),
        lw((1, CONV_CH)),
        lw((1, CONV_CH)),
        lw((1, CONV_CH)),
        lw((d, D_OUT_PAD)),
        lw((1, d)),
        lw((d, 2 * D_FF)),
        lw((D_FF, D_OUT_PAD)),
        _resident((1, d)),
    ]
    out_specs = [
        pl.BlockSpec((None, tm, d), lambda s: (prev(s) // nt, prev(s) % nt, 0)),
        pl.BlockSpec((None, HEADS, HEAD_DIM, HEAD_DIM), lambda s: (cur(s) // nt, 0, 0, 0)),
        pl.BlockSpec((None, HEADS, HEAD_DIM), lambda s: (cur(s) // nt, 0, 0)),
        pl.BlockSpec((None, HEADS, LANES), lambda s: (cur(s) // nt, 0, 0)),
        pl.BlockSpec((None, HIST, CONV_CH), lambda s: (cur(s) // nt, 0, 0)),
    ]
    out_shape = [
        jax.ShapeDtypeStruct((bsz, seq, d), F32),
        jax.ShapeDtypeStruct((bsz, HEADS, HEAD_DIM, HEAD_DIM), F32),
        jax.ShapeDtypeStruct((bsz, HEADS, HEAD_DIM), F32),
        jax.ShapeDtypeStruct((bsz, HEADS, LANES), F32),
        jax.ShapeDtypeStruct((bsz, HIST, CONV_CH), F32),
    ]
    scratch = [
        pltpu.VMEM((tm, d), BF16),
        pltpu.VMEM((tm, PROJ_PAD), F32),
        pltpu.VMEM((tm, d), BF16),
        pltpu.VMEM((tm, d), F32),
        pltpu.VMEM((tm, d), F32),
        pltpu.VMEM((tm, d), BF16),
        pltpu.VMEM((1, d), F32),
        pltpu.VMEM((tm, D_FF), BF16),
        pltpu.VMEM((CONV_SLABS, tm + HIST_PAD, LANES), F32),
        pltpu.VMEM((CONV_SLABS, tm, LANES), F32),
        pltpu.VMEM((HEADS, HEAD_DIM, HEAD_DIM), F32),
        pltpu.VMEM((SUBLANES, HEAD_DIM), F32),
        pltpu.VMEM((SUBLANES, LANES), F32),
    ]
    return pl.pallas_call(
        kern,
        grid=(n_tiles + 2,),
        in_specs=in_specs,
        out_specs=out_specs,
        out_shape=out_shape,
        scratch_shapes=scratch,
        compiler_params=pltpu.CompilerParams(
            dimension_semantics=("arbitrary",), vmem_limit_bytes=VMEM_LIMIT),
        name="prompt_layer",
    )(x, mod, mod, wl["g_mix"], wl["w_qkvo"], wl["w_glu"], wl["w_gate"], wl["b_gates"], wl["g_head"], wl["w_dw"], wl["b_dw"],
      wl["ln_g"], wl["ln_b"], wl["w_out"], wl["g_ffn"], wl["w_ffn13"], wl["w_ffn2"], g_final)


def _sample_in_kernel(x_ref, mod_ref, gmix_ref, wqkvo_ref, wglu_ref, wgate_ref, p_ref, h_ref):
    h = _rms_mod(x_ref[...], gmix_ref[...], mod_ref[:, D_MODEL:2 * D_MODEL], mod_ref[:, 0:D_MODEL])
    h_ref[...] = h.astype(BF16)
    _run(_in_proj_steps(h_ref, (wqkvo_ref, wglu_ref, wgate_ref), p_ref))


def _sample_in_proj(x, mod, layer, wl):
    rows, d = x.shape
    lw = functools.partial(_layer_resident, layer)
    return pl.pallas_call(
        _sample_in_kernel,
        grid=(1,),
        in_specs=[_resident((rows, d)), _resident(mod.shape), lw((1, d)), lw((d, COL_A)),
                  lw((d, D_OUT_PAD)), lw((d, LANES))],
        out_specs=pl.BlockSpec((rows, PROJ_PAD), lambda i: (0, 0)),
        out_shape=jax.ShapeDtypeStruct((rows, PROJ_PAD), F32),
        scratch_shapes=[pltpu.VMEM((rows, D_MODEL), BF16)],
        compiler_params=pltpu.CompilerParams(
            dimension_semantics=("arbitrary",), vmem_limit_bytes=VMEM_LIMIT),
        name="sample_in_proj",
    )(x, mod, wl["g_mix"], wl["w_qkvo"], wl["w_glu"], wl["w_gate"])


def _sample_step_kernel(p_ref, c_ref, n_ref, m_ref, conv_ref, bg_ref, ghead_ref, wdw_ref, bdw_ref,
                        lng_ref, lnb_ref, *rest, bt, n_alias, fill_other):
    hm_ref, zc_ref, c_out, n_out, m_out, conv_out = rest[n_alias:]
    slot = pl.program_id(0)

    @pl.when(slot == 0)
    def _():
        scale = HEAD_DIM ** -0.5
        gates = p_ref[:, COL_G:COL_G + LANES] + bg_ref[...]
        logf = pltpu.roll(_log_sigmoid(gates), LANES - HEADS, axis=1)
        inter = logf + m_ref[...]
        m_t = jnp.maximum(inter, gates)
        dw_all = jnp.exp(gates - m_t)
        a_all = jnp.exp(inter - m_t)
        eb_all = jnp.exp(-m_t)
        m_out[...] = m_t
        for h in range(HEADS):
            hs = slice(h * HEAD_DIM, (h + 1) * HEAD_DIM)
            q = p_ref[:, COL_Q + h * HEAD_DIM:COL_Q + (h + 1) * HEAD_DIM]
            k = p_ref[:, COL_K + h * HEAD_DIM:COL_K + (h + 1) * HEAD_DIM] * scale
            v = p_ref[:, COL_V + h * HEAD_DIM:COL_V + (h + 1) * HEAD_DIM]
            o = p_ref[:, COL_O + h * HEAD_DIM:COL_O + (h + 1) * HEAD_DIM]
            dw = dw_all[:, h:h + 1]
            a_in = a_all[:, h:h + 1]
            n_h = n_ref[:, hs]
            kw = k * dw
            s = jnp.sum(q * k, axis=-1, keepdims=True) * dw
            qn = jnp.sum(q * n_h, axis=-1, keepdims=True)
            n_out[:, hs] = a_in * n_h + kw
            q_t = q.T
            pad_rows = jnp.zeros((HEAD_DIM - bt, HEAD_DIM), F32)
            kw_t = jnp.concatenate([kw, pad_rows], axis=0).T.astype(BF16)
            v_wide = jnp.concatenate([v] * bt, axis=1)
            own = (lax.broadcasted_iota(jnp.int32, v_wide.shape, 1) // HEAD_DIM
                   == lax.broadcasted_iota(jnp.int32, v_wide.shape, 0))
            v_bd = jnp.concatenate([jnp.where(own, v_wide, 0.0),
                                    jnp.zeros((HEAD_DIM - bt, bt * HEAD_DIM), F32)], axis=0)
            outer = _bdot(kw_t, v_bd.astype(BF16))
            a_rows = jnp.broadcast_to(a_in, (bt, HEAD_DIM))
            qc_rows = []
            for b in range(bt):
                c_mat = c_ref[b, h]
                qc_rows.append(jnp.sum(q_t[:, b:b + 1] * c_mat, axis=0, keepdims=True))
                c_out[b, h] = a_rows[b:b + 1, :] * c_mat + outer[:, b * HEAD_DIM:(b + 1) * HEAD_DIM]
            qc = jnp.concatenate(qc_rows, axis=0)
            num = s * v + qc * a_in
            den = s + a_in * qn
            bound = jnp.maximum(jnp.abs(den), eb_all[:, h:h + 1])
            hh = num * (1.0 / bound)
            hn = hh * lax.rsqrt(jnp.mean(hh * hh, axis=-1, keepdims=True) + EPS) * ghead_ref[:, hs]
            hm_ref[:, hs] = hn * _sigmoid(o)

        u = p_ref[:, COL_A:COL_A + CONV_CH] * _sigmoid(p_ref[:, COL_B:COL_B + CONV_CH])
        z = u * wdw_ref[HIST:HIST + 1, :] + bdw_ref[...]
        for w in range(HIST):
            z = z + conv_ref[w] * wdw_ref[w:w + 1, :]
        zc_ref[...] = _layernorm_silu(z, lng_ref[...], lnb_ref[...])
        conv_out[0:HIST - 1] = conv_ref[1:HIST]
        conv_out[HIST - 1] = u

    if fill_other:
        @pl.when(slot != 0)
        def _():
            c_out[...] = jnp.zeros_like(c_out)
            conv_out[...] = jnp.zeros_like(conv_out)


def _sample_step(p, layer, state_c, state_n2, state_m_pad, state_conv, stacked, wl, *, bt=8):
    rows = p.shape[0]
    depth = state_c.shape[0]
    assert rows % bt == 0
    nblk = rows // bt
    first = stacked is None
    n_slots = depth if first else 1

    def blk_i(slot, i):
        return jnp.where(slot == 0, i, nblk - 1)

    def lay(*s):
        return pl.BlockSpec((None, bt) + s, lambda slot, i: (layer, blk_i(slot, i)) + (0,) * len(s))

    def row(*s):
        return pl.BlockSpec((bt,) + s, lambda slot, i: (blk_i(slot, i),) + (0,) * len(s))

    def stk(*s):
        return pl.BlockSpec((None, bt) + s,
                            lambda slot, i: ((slot if first else layer), i) + (0,) * len(s))

    conv_in = pl.BlockSpec((None, HIST, bt, CONV_CH),
                           lambda slot, i: (layer, 0, blk_i(slot, i), 0))
    conv_stk = pl.BlockSpec((None, HIST, bt, CONV_CH),
                            lambda slot, i: ((slot if first else layer), 0, i, 0))
    lw = functools.partial(_layer_resident, layer)
    in_specs = [
        row(PROJ_PAD), lay(HEADS, HEAD_DIM, HEAD_DIM), lay(MLSTM_WIDTH), lay(LANES), conv_in,
        lw((1, LANES)), lw((1, MLSTM_WIDTH)), lw((CONV_WIDTH, CONV_CH)),
        lw((1, CONV_CH)), lw((1, CONV_CH)), lw((1, CONV_CH)),
    ]
    args = [p, state_c, state_n2, state_m_pad, state_conv, wl["b_gates"], wl["g_head"], wl["w_dw"],
            wl["b_dw"], wl["ln_g"], wl["ln_b"]]
    aliases = {}
    if not first:
        aliases = {len(args): 2, len(args) + 1: 5}
        in_specs += [pl.BlockSpec(memory_space=pl.ANY), pl.BlockSpec(memory_space=pl.ANY)]
        args += list(stacked)
    out_shape = [
        jax.ShapeDtypeStruct((rows, MLSTM_WIDTH), F32),
        jax.ShapeDtypeStruct((rows, CONV_CH), F32),
        jax.ShapeDtypeStruct(state_c.shape, F32),
        jax.ShapeDtypeStruct((rows, MLSTM_WIDTH), F32),
        jax.ShapeDtypeStruct((rows, LANES), F32),
        jax.ShapeDtypeStruct(state_conv.shape, F32),
    ]
    hm, zc, c_new, n_new, m_new, conv_new = pl.pallas_call(
        functools.partial(_sample_step_kernel, bt=bt, n_alias=len(aliases),
                          fill_other=first and depth > 1),
        grid=(n_slots, nblk),
        in_specs=in_specs,
        out_specs=[row(MLSTM_WIDTH), row(CONV_CH), stk(HEADS, HEAD_DIM, HEAD_DIM),
                   row(MLSTM_WIDTH), row(LANES), conv_stk],
        out_shape=out_shape,
        input_output_aliases=aliases,
        compiler_params=pltpu.CompilerParams(
            dimension_semantics=("arbitrary", "arbitrary"), vmem_limit_bytes=VMEM_LIMIT),
        name="sample_step",
    )(*args)
    return hm, zc, (c_new, conv_new), n_new.reshape(rows, HEADS, HEAD_DIM), m_new[:, :HEADS]


def _sample_out_kernel(x_ref, hm_ref, zc_ref, mod_ref, wout_ref, gffn_ref, w13_ref, w2_ref,
                       gfin_ref, y_ref, mix_ref, x1_ref, h2_ref, act_ref, *, rows, final):
    mix_ref[:, 0:MLSTM_WIDTH] = hm_ref[...].astype(BF16)
    mix_ref[:, MLSTM_WIDTH:] = zc_ref[...].astype(BF16)

    def mod_rows(name, cs, r=slice(None)):
        i = _MOD_ROW[name]
        if isinstance(cs, slice) and cs == slice(None):
            cs = slice(0, D_MODEL)
        return mod_ref[r, i * D_MODEL + cs.start:i * D_MODEL + cs.stop]

    _run(itertools.chain(
        _out_proj_steps(x_ref, mix_ref, mod_rows, wout_ref, x1_ref),
        _norm_steps(x1_ref, h2_ref, gffn_ref, mod_rows, "sc2", "sh2", rows=rows, row_chunk=32),
        _ffn_gate_steps(h2_ref, w13_ref, act_ref),
        _ffn_out_steps(x1_ref, act_ref, lambda cs: mod_rows("ga2", cs), w2_ref, gfin_ref, y_ref,
                       rows=rows, row_chunk=32, final=final)))


def _sample_out(x, hm, zc, mod, layer, wl, g_final, *, final):
    rows, d = x.shape
    lw = functools.partial(_layer_resident, layer)
    return pl.pallas_call(
        functools.partial(_sample_out_kernel, rows=rows, final=final),
        grid=(1,),
        in_specs=[_resident((rows, d)), _resident(hm.shape), _resident(zc.shape),
                  _resident(mod.shape), lw((d, D_OUT_PAD)), lw((1, d)), lw((d, 2 * D_FF)),
                  lw((D_FF, D_OUT_PAD)),
                  _resident((1, d))],
        out_specs=pl.BlockSpec((rows, d), lambda i: (0, 0)),
        out_shape=jax.ShapeDtypeStruct((rows, D_MODEL), F32),
        scratch_shapes=[
            pltpu.VMEM((rows, D_MODEL), BF16),
            pltpu.VMEM((rows, D_MODEL), F32),
            pltpu.VMEM((rows, D_MODEL), BF16),
            pltpu.VMEM((rows, D_FF), BF16),
        ],
        compiler_params=pltpu.CompilerParams(
            dimension_semantics=("arbitrary",), vmem_limit_bytes=VMEM_LIMIT),
        name="sample_out",
    )(x, hm, zc, mod, wl["w_out"], wl["g_ffn"], wl["w_ffn13"], wl["w_ffn2"], g_final)


def _prep_weights(g_mix, w_in, b_gates, g_head, w_dw, b_dw, ln_g, ln_b, w_out, g_ffn, w_ffn13,
                  w_ffn2):
    depth, d, _ = w_in.shape
    n_gate = 2 * HEADS
    gate0 = 4 * MLSTM_WIDTH
    lane_pad = lambda w, n: jnp.pad(w.astype(BF16), ((0, 0), (0, 0), (0, n - w.shape[2])))
    return {
        "g_mix": g_mix.reshape(depth, 1, d),
        "w_qkvo": w_in[:, :, :gate0].astype(BF16),
        "w_glu": lane_pad(w_in[:, :, gate0 + n_gate:], D_OUT_PAD),
        "w_gate": lane_pad(w_in[:, :, gate0:gate0 + n_gate], LANES),
        "b_gates": jnp.pad(b_gates, ((0, 0), (0, LANES - n_gate))).reshape(depth, 1, LANES),
        "g_head": g_head.reshape(depth, 1, MLSTM_WIDTH),
        "w_dw": w_dw,
        "b_dw": b_dw.reshape(depth, 1, CONV_CH),
        "ln_g": ln_g.reshape(depth, 1, CONV_CH),
        "ln_b": ln_b.reshape(depth, 1, CONV_CH),
        "w_out": jnp.pad(w_out.astype(BF16), ((0, 0), (0, 0), (0, D_OUT_PAD - d))),
        "g_ffn": g_ffn.reshape(depth, 1, d),
        "w_ffn13": w_ffn13.astype(BF16),
        "w_ffn2": jnp.pad(w_ffn2.astype(BF16), ((0, 0), (0, 0), (0, D_OUT_PAD - d))),
    }


def _forward(x_prompt, x_sample, c_prompt, c_sample, state_C, state_n, state_m, state_conv,
             w_ada, b_ada, g_mix, w_in, b_gates, g_head, w_dw, b_dw, ln_g, ln_b, w_out,
             g_ffn, w_ffn13, w_ffn2, g_final, *, tm):
    depth = w_ada.shape[0]
    bp = x_prompt.shape[0]
    d = x_prompt.shape[-1]
    mod = _modulation(jnp.concatenate([c_prompt, c_sample], axis=0), w_ada, b_ada)
    gfin = g_final.reshape(1, d)
    xp = x_prompt
    xs = x_sample.reshape(x_sample.shape[0], d)
    outs_p = [[], [], [], []]
    ns_l, ms_l = [], []
    stacked = None
    n_seq = xs.shape[0]
    state_n2 = state_n.reshape(depth, n_seq, MLSTM_WIDTH)
    state_m_pad = jnp.pad(state_m, ((0, 0), (0, 0), (0, LANES - HEADS)))
    state_conv_t = jnp.swapaxes(state_conv, 1, 2)
    wl = _prep_weights(g_mix, w_in, b_gates, g_head, w_dw, b_dw, ln_g, ln_b, w_out, g_ffn,
                       w_ffn13, w_ffn2)
    for l in range(depth):
        final = l == depth - 1
        mod_p = mod[l, :bp].reshape(bp, 6, d)
        mod_s = mod[l, bp:]
        xp, c1, n1, m1, b1 = _prompt_layer(xp, mod_p, l, wl, gfin, tm=tm, final=final)
        for acc, val in zip(outs_p, (c1, n1, m1[:, :, 0], b1)):
            acc.append(val)
        p = _sample_in_proj(xs, mod_s, l, wl)
        hm, zc, stacked, n2, m2 = _sample_step(p, l, state_C, state_n2, state_m_pad, state_conv_t,
                                               stacked, wl)
        xs = _sample_out(xs, hm, zc, mod_s, l, wl, gfin, final=final)
        ns_l.append(n2)
        ms_l.append(m2)
    y_sample = xs.reshape(x_sample.shape)
    c_sample_new, conv_t_new = stacked
    return ((xp, y_sample) + tuple(jnp.stack(a) for a in outs_p)
            + (c_sample_new, jnp.stack(ns_l), jnp.stack(ms_l), jnp.swapaxes(conv_t_new, 1, 2)))


def kernel(x_prompt, x_sample, c_prompt, c_sample, state_C, state_n, state_m, state_conv, w_ada, b_ada, g_mix, w_in, b_gates, g_head, w_dw, b_dw, ln_g, ln_b, w_out, g_ffn, w_ffn13, w_ffn2, g_final):
    return _forward(x_prompt, x_sample, c_prompt, c_sample, state_C, state_n, state_m, state_conv,
                    w_ada, b_ada, g_mix, w_in, b_gates, g_head, w_dw, b_dw, ln_g, ln_b, w_out,
                    g_ffn, w_ffn13, w_ffn2, g_final, tm=256)
```

```python
import functools
import itertools

import jax
import jax.numpy as jnp
from jax import lax
from jax.experimental import pallas as pl
from jax.experimental.pallas import tpu as pltpu

F32 = jnp.float32
BF16 = jnp.bfloat16

D_MODEL = 1024
HEADS = 4
HEAD_DIM = 128
MLSTM_WIDTH = HEADS * HEAD_DIM
CONV_CH = D_MODEL - MLSTM_WIDTH
CONV_WIDTH = 31
HIST = CONV_WIDTH - 1
D_FF = 2816
CHUNK = 128
EPS = 1e-6

LANES = 128
SUBLANES = 8
MXU_N = 256

COL_Q = 0
COL_K = MLSTM_WIDTH
COL_V = 2 * MLSTM_WIDTH
COL_O = 3 * MLSTM_WIDTH
COL_A = 4 * MLSTM_WIDTH
COL_B = COL_A + CONV_CH
COL_G = COL_B + CONV_CH
PROJ_PAD = COL_G + LANES
HIST_PAD = 32
CONV_SLABS = CONV_CH // LANES
CONV_STRIDE = 2
D_OUT_PAD = D_MODEL + LANES
VMEM_LIMIT = 60 * 1024 * 1024


def _log_sigmoid(x):
    return jnp.minimum(x, 0.0) - jnp.log1p(jnp.exp(-jnp.abs(x)))


def _sigmoid(x):
    return 0.5 * jnp.tanh(0.5 * x) + 0.5


def _silu(x):
    half = 0.5 * x
    return half * (jnp.tanh(half) + 1.0)


def _bdot(a, b):
    return jnp.dot(a, b, preferred_element_type=F32)


def _bdot_nt(a, b):
    return lax.dot_general(a, b, (((1,), (1,)), ((), ())), preferred_element_type=F32)


def _row_slices(rows, chunk):
    chunk = min(chunk, rows)
    assert rows % chunk == 0
    return [slice(r0, r0 + chunk) for r0 in range(0, rows, chunk)]


def _rms_mod(x, g, sc, sh):
    ms = jnp.mean(x * x, axis=-1, keepdims=True)
    return (x * lax.rsqrt(ms + EPS)) * (g * (1.0 + sc)) + sh


def _mod_kernel(c_ref, w_ref, b_ref, o_ref):
    o_ref[...] = _bdot(c_ref[...].astype(BF16), w_ref[...].astype(BF16)) + b_ref[...]


def _modulation(c_all, w_ada, b_ada):
    depth, d, n6 = w_ada.shape
    rows = c_all.shape[0]
    tn = 2048
    return pl.pallas_call(
        _mod_kernel,
        grid=(depth, n6 // tn),
        in_specs=[
            pl.BlockSpec((rows, d), lambda l, n: (0, 0)),
            pl.BlockSpec((None, d, tn), lambda l, n: (l, 0, n)),
            pl.BlockSpec((None, 1, tn), lambda l, n: (l, 0, n)),
        ],
        out_specs=pl.BlockSpec((None, rows, tn), lambda l, n: (l, 0, n)),
        out_shape=jax.ShapeDtypeStruct((depth, rows, n6), F32),
        compiler_params=pltpu.CompilerParams(
            dimension_semantics=("arbitrary", "arbitrary"), vmem_limit_bytes=VMEM_LIMIT),
        name="adaln_mod",
    )(c_all, w_ada, b_ada.reshape(depth, 1, n6))


def _norm_steps(src_ref, dst_ref, g_ref, mod, sc, sh, *, rows, row_chunk):
    for r in _row_slices(rows, row_chunk):
        h = _rms_mod(src_ref[r, :], g_ref[...], mod(sc, slice(None), r), mod(sh, slice(None), r))
        dst_ref[r, :] = h.astype(BF16)
        yield


def _in_proj_steps(h_ref, w_refs, p_ref):
    for w_ref, col0, width in zip(w_refs, (COL_Q, COL_A, COL_G), (COL_A, 2 * CONV_CH, LANES)):
        for n0 in range(0, width, MXU_N):
            n1 = min(n0 + MXU_N, width)
            p_ref[:, col0 + n0:col0 + n1] = _bdot(h_ref[...], w_ref[:, n0:n1])
            yield


def _out_proj_steps(x_ref, mix_ref, mod, wout_ref, x1_ref):
    for n0 in range(0, D_MODEL, MXU_N):
        cs = slice(n0, n0 + MXU_N)
        mx = _bdot(mix_ref[...], wout_ref[:, cs])
        x1_ref[:, cs] = x_ref[:, cs] + mod("ga1", cs, slice(None)) * mx
        yield


def _ffn_gate_steps(h2_ref, w13_ref, act_ref):
    for n0 in range(0, D_FF, MXU_N):
        gg = _bdot(h2_ref[...], w13_ref[:, n0:n0 + MXU_N])
        gg = _silu(gg)
        yield
        uu = _bdot(h2_ref[...], w13_ref[:, D_FF + n0:D_FF + n0 + MXU_N])
        act_ref[:, n0:n0 + MXU_N] = (gg * uu).astype(BF16)
        yield


def _ffn_out_steps(x1_ref, act_ref, ga2, w2_ref, gfin_ref, y_ref, *, rows, row_chunk, final):
    dst = x1_ref if final else y_ref
    for n0 in range(0, D_MODEL, MXU_N):
        cs = slice(n0, n0 + MXU_N)
        ff = _bdot(act_ref[...], w2_ref[:, cs])
        dst[:, cs] = x1_ref[:, cs] + ga2(cs) * ff
        yield
    if final:
        for r in _row_slices(rows, row_chunk):
            xb = x1_ref[r, :]
            ms = jnp.mean(xb * xb, axis=-1, keepdims=True)
            y_ref[r, :] = xb * lax.rsqrt(ms + EPS) * gfin_ref[...]
            yield


def _run(*gens):
    live = [g if isinstance(g, tuple) else (g, 1) for g in gens]
    while live:
        for entry in list(live):
            g, k = entry
            for _ in range(k):
                try:
                    next(g)
                except StopIteration:
                    live.remove(entry)
                    break


def _layernorm_silu(z, lng, lnb):
    mu = jnp.mean(z, axis=-1, keepdims=True)
    zc = z - mu
    var = jnp.mean(zc * zc, axis=-1, keepdims=True)
    zn = zc * lax.rsqrt(var + EPS) * lng + lnb
    return _silu(zn)


_MOD_ROW = {"sh1": 0, "sc1": 1, "ga1": 2, "sh2": 3, "sc2": 4, "ga2": 5}


def _cumsum_chunk(tril_bf, x):
    hi = x.astype(BF16)
    r1 = x - hi.astype(F32)
    mid = r1.astype(BF16)
    lo = (r1 - mid.astype(F32)).astype(BF16)
    return _bdot(tril_bf, hi) + _bdot(tril_bf, mid) + _bdot(tril_bf, lo)


def _mlstm_chunk(p_ref, r0, bg_ref, ghead_ref, mix_ref, C_ref, n_ref, m_ref, tril, tril_bf):
    rs = slice(r0, r0 + CHUNK)
    gates = p_ref[rs, COL_G:COL_G + LANES] + bg_ref[...]
    fcum = _cumsum_chunk(tril_bf, _log_sigmoid(gates))
    gates_t = gates.T
    fcum_t = fcum.T
    yield
    heads = range(HEADS)
    col = lambda base, h: slice(base + h * HEAD_DIM, base + (h + 1) * HEAD_DIM)
    st = [dict() for _ in heads]
    for h in heads:
        d = st[h]
        d["q"] = p_ref[rs, col(COL_Q, h)]
        d["k"] = p_ref[rs, col(COL_K, h)] * (HEAD_DIM ** -0.5)
        d["v_bf"] = p_ref[rs, col(COL_V, h)].astype(BF16)
        d["q_bf"] = d["q"].astype(BF16)
        d["f_col"] = fcum[:, HEADS + h:HEADS + h + 1]
        f_row = fcum_t[HEADS + h:HEADS + h + 1, :]
        i_row = gates_t[h:h + 1, :]
        g = f_row[:, CHUNK - 1:CHUNK]
        d["dlog"] = jnp.where(tril, d["f_col"] - f_row + i_row, -jnp.inf)
        d["dmax"] = jnp.max(d["dlog"], axis=-1, keepdims=True)
        d["wmax"] = jnp.max(g - f_row + i_row, axis=-1, keepdims=True)
        d["qk"] = _bdot_nt(d["q_bf"], d["k"].astype(BF16))
        yield
    lane = lax.broadcasted_iota(jnp.int32, (1, LANES), 1)
    m_row = jnp.zeros((1, LANES), F32)
    wmax_row = jnp.zeros((1, LANES), F32)
    dmax_t = jnp.zeros((CHUNK, LANES), F32)
    for h in heads:
        m_row = jnp.where(lane == HEADS + h, m_ref[h:h + 1, :], m_row)
        wmax_row = jnp.where(lane == HEADS + h, st[h]["wmax"], wmax_row)
        dmax_t = jnp.where(lane == HEADS + h, st[h]["dmax"], dmax_t)
    g_row = fcum[CHUNK - 1:CHUNK, :]
    inter_t = fcum + m_row
    m_t_t = jnp.maximum(inter_t, dmax_t)
    a_inter_t = jnp.exp(inter_t - m_t_t)
    ebound_t = jnp.exp(-m_t_t)
    m_new_row = jnp.maximum(g_row + m_row, wmax_row)
    decay_row = jnp.exp(g_row + m_row - m_new_row)
    i_shift = pltpu.roll(gates, HEADS, axis=1)
    wj_t = jnp.exp(g_row - fcum + i_shift - m_new_row)
    yield
    for h in heads:
        d = st[h]
        hl = slice(HEADS + h, HEADS + h + 1)
        d["a_inter"] = a_inter_t[:, hl]
        d["ebound"] = ebound_t[:, hl]
        d["decay"] = decay_row[:, hl]
        s = d["qk"] * jnp.exp(d["dlog"] - m_t_t[:, hl])
        d["ssum"] = jnp.sum(s, axis=-1, keepdims=True)
        d["sv"] = _bdot(s.astype(BF16), d["v_bf"])
        kw = d["k"] * wj_t[:, hl]
        d["kwsum"] = jnp.sum(kw, axis=0, keepdims=True)
        d["kv"] = _bdot(kw.T.astype(BF16), d["v_bf"])
        m_ref[h:h + 1, :] = jnp.broadcast_to(m_new_row[:, hl], (1, LANES))
        yield
    for h in heads:
        d = st[h]
        n_row = n_ref[h:h + 1, :]
        c_mat = C_ref[h]
        num = d["sv"] + _bdot(d["q_bf"], c_mat.astype(BF16)) * d["a_inter"]
        qn = jnp.sum(d["q"] * n_row, axis=-1, keepdims=True)
        den = d["ssum"] + d["a_inter"] * qn
        bound = jnp.maximum(jnp.abs(den), d["ebound"])
        hh = num * (1.0 / bound)
        C_ref[h] = d["decay"] * c_mat + d["kv"]
        n_ref[h:h + 1, :] = d["decay"] * n_row + d["kwsum"]
        hn = hh * lax.rsqrt(jnp.mean(hh * hh, axis=-1, keepdims=True) + EPS) * ghead_ref[:, col(0, h)]
        mix_ref[rs, col(0, h)] = (hn * _sigmoid(p_ref[rs, col(COL_O, h)])).astype(BF16)
        yield


def _prompt_layer_kernel(x_ref, mod1_ref, mod2_ref, gmix_ref, wqkvo_ref, wglu_ref, wgate_ref, bg_ref,
                         ghead_ref, wdw_ref, bdw_ref, lng_ref, lnb_ref, wout_ref, gffn_ref, w13_ref,
                         w2_ref, gfin_ref,
                         y_ref, c_out, n_out, m_out, conv_out,
                         h_ref, p_ref, mix_ref, xres_ref, x1_ref, h2_ref, ga2_ref, act_ref, ubuf_ref,
                         zc_ref, C_ref, n_ref, m_ref, *, tm, tiles_per_seq, final):
    s = pl.program_id(0)
    n_tiles = pl.num_programs(0) - 2
    j = jnp.clip(s - 1, 0, n_tiles - 1) % tiles_per_seq
    last = tiles_per_seq - 1
    row_chunk = 32

    @pl.when(s == 0)
    def _():
        for ref in (p_ref, xres_ref, x1_ref, h2_ref, ga2_ref):
            ref[...] = jnp.zeros_like(ref)

    @pl.when(j == 0)
    def _():
        C_ref[...] = jnp.zeros_like(C_ref)
        n_ref[...] = jnp.zeros_like(n_ref)
        m_ref[...] = jnp.zeros_like(m_ref)
        ubuf_ref[:, 0:HIST_PAD, :] = jnp.zeros((CONV_SLABS, HIST_PAD, LANES), F32)

    @pl.when(j > 0)
    def _():
        ubuf_ref[:, 0:HIST_PAD, :] = ubuf_ref[:, tm:tm + HIST_PAD, :]

    def mod1(name, cs, r=None):
        i = _MOD_ROW[name]
        return mod1_ref[i:i + 1, cs]

    def mod2(name, cs, r=None):
        i = _MOD_ROW[name]
        return mod2_ref[i:i + 1, cs]

    row_id = lax.broadcasted_iota(jnp.int32, (CHUNK, CHUNK), 0)
    col_id = lax.broadcasted_iota(jnp.int32, (CHUNK, CHUNK), 1)
    tril = row_id >= col_id
    tril_bf = jnp.where(tril, 1.0, 0.0).astype(BF16)

    def mlstm_steps():
        for c in range(tm // CHUNK):
            yield from _mlstm_chunk(p_ref, c * CHUNK, bg_ref, ghead_ref, mix_ref, C_ref, n_ref,
                                    m_ref, tril, tril_bf)

    def glu_steps():
        for r in _row_slices(tm, row_chunk):
            ro = slice(r.start + HIST_PAD, r.stop + HIST_PAD)
            u = p_ref[r, COL_A:COL_A + CONV_CH] * _sigmoid(p_ref[r, COL_B:COL_B + CONV_CH])
            for g in range(CONV_SLABS):
                ubuf_ref[g, ro, :] = u[:, g * LANES:(g + 1) * LANES]
            yield

    def conv_steps(slabs):
        n_sets = row_chunk // SUBLANES
        firsts = [(t0 // CONV_STRIDE) * CONV_STRIDE * SUBLANES + t0 % CONV_STRIDE
                  for t0 in range(n_sets)]
        for g in slabs:
            gs = slice(g * LANES, (g + 1) * LANES)
            bias = jnp.broadcast_to(bdw_ref[:, gs], (SUBLANES, LANES))
            for base in range(0, tm, row_chunk):
                acc = [bias for _ in range(n_sets)]
                for w in range(CONV_WIDTH):
                    tap = wdw_ref[w:w + 1, gs]
                    for t0 in range(n_sets):
                        src = pl.ds(base + firsts[t0] + HIST_PAD - HIST + w, SUBLANES,
                                    stride=CONV_STRIDE)
                        acc[t0] = acc[t0] + ubuf_ref[g, src, :] * tap
                for t0 in range(n_sets):
                    zc_ref[g, pl.ds(base + firsts[t0], SUBLANES, stride=CONV_STRIDE), :] = acc[t0]
                yield

    def conv_norm_steps():
        for r in _row_slices(tm, row_chunk):
            z = jnp.concatenate([zc_ref[g, r, :] for g in range(CONV_SLABS)], axis=-1)
            mix_ref[r, MLSTM_WIDTH:] = _layernorm_silu(z, lng_ref[...], lnb_ref[...]).astype(BF16)
            yield

    def keep_steps():
        xres_ref[...] = x_ref[...]
        ga2_ref[...] = mod2("ga2", slice(None))
        yield

    _run(mlstm_steps(), _ffn_gate_steps(h2_ref, w13_ref, act_ref),
         (itertools.chain(glu_steps(), conv_steps(range(CONV_SLABS))), 2),
         _norm_steps(x_ref, h_ref, gmix_ref, mod1, "sc1", "sh1", rows=tm, row_chunk=row_chunk))

    @pl.when(jnp.logical_and(j == last, jnp.logical_and(s >= 1, s <= n_tiles)))
    def _():
        c_out[...] = C_ref[...]
        n_out[...] = n_ref[0:HEADS, :]
        m_out[...] = m_ref[0:HEADS, :]
        for g in range(CONV_SLABS):
            conv_out[:, g * LANES:(g + 1) * LANES] = ubuf_ref[g, tm + HIST_PAD - HIST:tm + HIST_PAD, :]

    @pl.when(s >= 0)
    def _():
        _run(_ffn_out_steps(x1_ref, act_ref, lambda cs: ga2_ref[:, cs], w2_ref, gfin_ref, y_ref,
                            rows=tm, row_chunk=row_chunk, final=final),
             (conv_norm_steps(), 2))

    @pl.when(s >= 0)
    def _():
        _run(_out_proj_steps(xres_ref, mix_ref, mod2, wout_ref, x1_ref))
        _run(_in_proj_steps(h_ref, (wqkvo_ref, wglu_ref, wgate_ref), p_ref),
             _norm_steps(x1_ref, h2_ref, gffn_ref, mod2, "sc2", "sh2", rows=tm, row_chunk=row_chunk))
        _run(keep_steps())


def _resident(shape):
    nd = len(shape)
    return pl.BlockSpec(shape, lambda *g: (0,) * nd, pipeline_mode=pl.Buffered(1))


def _layer_resident(layer, shape):
    nd = len(shape)
    return pl.BlockSpec((None,) + shape, lambda *g: (layer,) + (0,) * nd,
                        pipeline_mode=pl.Buffered(1))


def _prompt_layer(x, mod, layer, wl, g_final, *, tm, final):
    bsz, seq, d = x.shape
    lw = functools.partial(_layer_resident, layer)
    assert seq % tm == 0 and tm % CHUNK == 0 and tm >= HIST_PAD
    nt = seq // tm
    n_tiles = bsz * nt
    kern = functools.partial(_prompt_layer_kernel, tm=tm, tiles_per_seq=nt, final=final)

    def tile(s, lag):
        return jnp.clip(s - lag, 0, n_tiles - 1)

    def cur(s):
        return tile(s, 1)

    def prev(s):
        return tile(s, 2)

    in_specs = [
        pl.BlockSpec((None, tm, d), lambda s: (tile(s, 0) // nt, tile(s, 0) % nt, 0)),
        pl.BlockSpec((None, 6, d), lambda s: (tile(s, 0) // nt, 0, 0)),
        pl.BlockSpec((None, 6, d), lambda s: (cur(s) // nt, 0, 0)),
        lw((1, d)),
        lw((d, COL_A)),
        lw((d, D_OUT_PAD)),
        lw((d, LANES)),
        lw((1, LANES)),
        lw((1, MLSTM_WIDTH)),
        lw((CONV_WIDTH, CONV_CH)),
        lw((1, CONV_CH)),
        lw((1, CONV_CH)),
        lw((1, CONV_CH)),
        lw((d, D_OUT_PAD)),
        lw((1, d)),
        lw((d, 2 * D_FF)),
        lw((D_FF, D_OUT_PAD)),
        _resident((1, d)),
    ]
    out_specs = [
        pl.BlockSpec((None, tm, d), lambda s: (prev(s) // nt, prev(s) % nt, 0)),
        pl.BlockSpec((None, HEADS, HEAD_DIM, HEAD_DIM), lambda s: (cur(s) // nt, 0, 0, 0)),
        pl.BlockSpec((None, HEADS, HEAD_DIM), lambda s: (cur(s) // nt, 0, 0)),
        pl.BlockSpec((None, HEADS, LANES), lambda s: (cur(s) // nt, 0, 0)),
        pl.BlockSpec((None, HIST, CONV_CH), lambda s: (cur(s) // nt, 0, 0)),
    ]
    out_shape = [
        jax.ShapeDtypeStruct((bsz, seq, d), F32),
        jax.ShapeDtypeStruct((bsz, HEADS, HEAD_DIM, HEAD_DIM), F32),
        jax.ShapeDtypeStruct((bsz, HEADS, HEAD_DIM), F32),
        jax.ShapeDtypeStruct((bsz, HEADS, LANES), F32),
        jax.ShapeDtypeStruct((bsz, HIST, CONV_CH), F32),
    ]
    scratch = [
        pltpu.VMEM((tm, d), BF16),
        pltpu.VMEM((tm, PROJ_PAD), F32),
        pltpu.VMEM((tm, d), BF16),
        pltpu.VMEM((tm, d), F32),
        pltpu.VMEM((tm, d), F32),
        pltpu.VMEM((tm, d), BF16),
        pltpu.VMEM((1, d), F32),
        pltpu.VMEM((tm, D_FF), BF16),
        pltpu.VMEM((CONV_SLABS, tm + HIST_PAD, LANES), F32),
        pltpu.VMEM((CONV_SLABS, tm, LANES), F32),
        pltpu.VMEM((HEADS, HEAD_DIM, HEAD_DIM), F32),
        pltpu.VMEM((SUBLANES, HEAD_DIM), F32),
        pltpu.VMEM((SUBLANES, LANES), F32),
    ]
    return pl.pallas_call(
        kern,
        grid=(n_tiles + 2,),
        in_specs=in_specs,
        out_specs=out_specs,
        out_shape=out_shape,
        scratch_shapes=scratch,
        compiler_params=pltpu.CompilerParams(
            dimension_semantics=("arbitrary",), vmem_limit_bytes=VMEM_LIMIT),
        name="prompt_layer",
    )(x, mod, mod, wl["g_mix"], wl["w_qkvo"], wl["w_glu"], wl["w_gate"], wl["b_gates"], wl["g_head"], wl["w_dw"], wl["b_dw"],
      wl["ln_g"], wl["ln_b"], wl["w_out"], wl["g_ffn"], wl["w_ffn13"], wl["w_ffn2"], g_final)


def _sample_in_kernel(x_ref, mod_ref, gmix_ref, wqkvo_ref, wglu_ref, wgate_ref, p_ref, h_ref):
    h = _rms_mod(x_ref[...], gmix_ref[...], mod_ref[:, D_MODEL:2 * D_MODEL], mod_ref[:, 0:D_MODEL])
    h_ref[...] = h.astype(BF16)
    _run(_in_proj_steps(h_ref, (wqkvo_ref, wglu_ref, wgate_ref), p_ref))


def _sample_in_proj(x, mod, layer, wl):
    rows, d = x.shape
    lw = functools.partial(_layer_resident, layer)
    return pl.pallas_call(
        _sample_in_kernel,
        grid=(1,),
        in_specs=[_resident((rows, d)), _resident(mod.shape), lw((1, d)), lw((d, COL_A)),
                  lw((d, D_OUT_PAD)), lw((d, LANES))],
        out_specs=pl.BlockSpec((rows, PROJ_PAD), lambda i: (0, 0)),
        out_shape=jax.ShapeDtypeStruct((rows, PROJ_PAD), F32),
        scratch_shapes=[pltpu.VMEM((rows, D_MODEL), BF16)],
        compiler_params=pltpu.CompilerParams(
            dimension_semantics=("arbitrary",), vmem_limit_bytes=VMEM_LIMIT),
        name="sample_in_proj",
    )(x, mod, wl["g_mix"], wl["w_qkvo"], wl["w_glu"], wl["w_gate"])


def _sample_step_kernel(p_ref, c_ref, n_ref, m_ref, conv_ref, bg_ref, ghead_ref, wdw_ref, bdw_ref,
                        lng_ref, lnb_ref, *rest, bt, n_alias, fill_other):
    hm_ref, zc_ref, c_out, n_out, m_out, conv_out = rest[n_alias:]
    slot = pl.program_id(0)

    @pl.when(slot == 0)
    def _():
        scale = HEAD_DIM ** -0.5
        gates = p_ref[:, COL_G:COL_G + LANES] + bg_ref[...]
        logf = pltpu.roll(_log_sigmoid(gates), LANES - HEADS, axis=1)
        inter = logf + m_ref[...]
        m_t = jnp.maximum(inter, gates)
        dw_all = jnp.exp(gates - m_t)
        a_all = jnp.exp(inter - m_t)
        eb_all = jnp.exp(-m_t)
        m_out[...] = m_t
        for h in range(HEADS):
            hs = slice(h * HEAD_DIM, (h + 1) * HEAD_DIM)
            q = p_ref[:, COL_Q + h * HEAD_DIM:COL_Q + (h + 1) * HEAD_DIM]
            k = p_ref[:, COL_K + h * HEAD_DIM:COL_K + (h + 1) * HEAD_DIM] * scale
            v = p_ref[:, COL_V + h * HEAD_DIM:COL_V + (h + 1) * HEAD_DIM]
            o = p_ref[:, COL_O + h * HEAD_DIM:COL_O + (h + 1) * HEAD_DIM]
            dw = dw_all[:, h:h + 1]
            a_in = a_all[:, h:h + 1]
            n_h = n_ref[:, hs]
            kw = k * dw
            s = jnp.sum(q * k, axis=-1, keepdims=True) * dw
            qn = jnp.sum(q * n_h, axis=-1, keepdims=True)
            n_out[:, hs] = a_in * n_h + kw
            q_t = q.T
            pad_rows = jnp.zeros((HEAD_DIM - bt, HEAD_DIM), F32)
            kw_t = jnp.concatenate([kw, pad_rows], axis=0).T.astype(BF16)
            v_wide = jnp.concatenate([v] * bt, axis=1)
            own = (lax.broadcasted_iota(jnp.int32, v_wide.shape, 1) // HEAD_DIM
                   == lax.broadcasted_iota(jnp.int32, v_wide.shape, 0))
            v_bd = jnp.concatenate([jnp.where(own, v_wide, 0.0),
                                    jnp.zeros((HEAD_DIM - bt, bt * HEAD_DIM), F32)], axis=0)
            outer = _bdot(kw_t, v_bd.astype(BF16))
            a_rows = jnp.broadcast_to(a_in, (bt, HEAD_DIM))
            qc_rows = []
            for b in range(bt):
                c_mat = c_ref[b, h]
                qc_rows.append(jnp.sum(q_t[:, b:b + 1] * c_mat, axis=0, keepdims=True))
                c_out[b, h] = a_rows[b:b + 1, :] * c_mat + outer[:, b * HEAD_DIM:(b + 1) * HEAD_DIM]
            qc = jnp.concatenate(qc_rows, axis=0)
            num = s * v + qc * a_in
            den = s + a_in * qn
            bound = jnp.maximum(jnp.abs(den), eb_all[:, h:h + 1])
            hh = num * (1.0 / bound)
            hn = hh * lax.rsqrt(jnp.mean(hh * hh, axis=-1, keepdims=True) + EPS) * ghead_ref[:, hs]
            hm_ref[:, hs] = hn * _sigmoid(o)

        u = p_ref[:, COL_A:COL_A + CONV_CH] * _sigmoid(p_ref[:, COL_B:COL_B + CONV_CH])
        z = u * wdw_ref[HIST:HIST + 1, :] + bdw_ref[...]
        for w in range(HIST):
            z = z + conv_ref[w] * wdw_ref[w:w + 1, :]
        zc_ref[...] = _layernorm_silu(z, lng_ref[...], lnb_ref[...])
        conv_out[0:HIST - 1] = conv_ref[1:HIST]
        conv_out[HIST - 1] = u

    if fill_other:
        @pl.when(slot != 0)
        def _():
            c_out[...] = jnp.zeros_like(c_out)
            conv_out[...] = jnp.zeros_like(conv_out)


def _sample_step(p, layer, state_c, state_n2, state_m_pad, state_conv, stacked, wl, *, bt=8):
    rows = p.shape[0]
    depth = state_c.shape[0]
    assert rows % bt == 0
    nblk = rows // bt
    first = stacked is None
    n_slots = depth if first else 1

    def blk_i(slot, i):
        return jnp.where(slot == 0, i, nblk - 1)

    def lay(*s):
        return pl.BlockSpec((None, bt) + s, lambda slot, i: (layer, blk_i(slot, i)) + (0,) * len(s))

    def row(*s):
        return pl.BlockSpec((bt,) + s, lambda slot, i: (blk_i(slot, i),) + (0,) * len(s))

    def stk(*s):
        return pl.BlockSpec((None, bt) + s,
                            lambda slot, i: ((slot if first else layer), i) + (0,) * len(s))

    conv_in = pl.BlockSpec((None, HIST, bt, CONV_CH),
                           lambda slot, i: (layer, 0, blk_i(slot, i), 0))
    conv_stk = pl.BlockSpec((None, HIST, bt, CONV_CH),
                            lambda slot, i: ((slot if first else layer), 0, i, 0))
    lw = functools.partial(_layer_resident, layer)
    in_specs = [
        row(PROJ_PAD), lay(HEADS, HEAD_DIM, HEAD_DIM), lay(MLSTM_WIDTH), lay(LANES), conv_in,
        lw((1, LANES)), lw((1, MLSTM_WIDTH)), lw((CONV_WIDTH, CONV_CH)),
        lw((1, CONV_CH)), lw((1, CONV_CH)), lw((1, CONV_CH)),
    ]
    args = [p, state_c, state_n2, state_m_pad, state_conv, wl["b_gates"], wl["g_head"], wl["w_dw"],
            wl["b_dw"], wl["ln_g"], wl["ln_b"]]
    aliases = {}
    if not first:
        aliases = {len(args): 2, len(args) + 1: 5}
        in_specs += [pl.BlockSpec(memory_space=pl.ANY), pl.BlockSpec(memory_space=pl.ANY)]
        args += list(stacked)
    out_shape = [
        jax.ShapeDtypeStruct((rows, MLSTM_WIDTH), F32),
        jax.ShapeDtypeStruct((rows, CONV_CH), F32),
        jax.ShapeDtypeStruct(state_c.shape, F32),
        jax.ShapeDtypeStruct((rows, MLSTM_WIDTH), F32),
        jax.ShapeDtypeStruct((rows, LANES), F32),
        jax.ShapeDtypeStruct(state_conv.shape, F32),
    ]
    hm, zc, c_new, n_new, m_new, conv_new = pl.pallas_call(
        functools.partial(_sample_step_kernel, bt=bt, n_alias=len(aliases),
                          fill_other=first and depth > 1),
        grid=(n_slots, nblk),
        in_specs=in_specs,
        out_specs=[row(MLSTM_WIDTH), row(CONV_CH), stk(HEADS, HEAD_DIM, HEAD_DIM),
                   row(MLSTM_WIDTH), row(LANES), conv_stk],
        out_shape=out_shape,
        input_output_aliases=aliases,
        compiler_params=pltpu.CompilerParams(
            dimension_semantics=("arbitrary", "arbitrary"), vmem_limit_bytes=VMEM_LIMIT),
        name="sample_step",
    )(*args)
    return hm, zc, (c_new, conv_new), n_new.reshape(rows, HEADS, HEAD_DIM), m_new[:, :HEADS]


def _sample_out_kernel(x_ref, hm_ref, zc_ref, mod_ref, wout_ref, gffn_ref, w13_ref, w2_ref,
                       gfin_ref, y_ref, mix_ref, x1_ref, h2_ref, act_ref, *, rows, final):
    mix_ref[:, 0:MLSTM_WIDTH] = hm_ref[...].astype(BF16)
    mix_ref[:, MLSTM_WIDTH:] = zc_ref[...].astype(BF16)

    def mod_rows(name, cs, r=slice(None)):
        i = _MOD_ROW[name]
        if isinstance(cs, slice) and cs == slice(None):
            cs = slice(0, D_MODEL)
        return mod_ref[r, i * D_MODEL + cs.start:i * D_MODEL + cs.stop]

    _run(itertools.chain(
        _out_proj_steps(x_ref, mix_ref, mod_rows, wout_ref, x1_ref),
        _norm_steps(x1_ref, h2_ref, gffn_ref, mod_rows, "sc2", "sh2", rows=rows, row_chunk=32),
        _ffn_gate_steps(h2_ref, w13_ref, act_ref),
        _ffn_out_steps(x1_ref, act_ref, lambda cs: mod_rows("ga2", cs), w2_ref, gfin_ref, y_ref,
                       rows=rows, row_chunk=32, final=final)))


def _sample_out(x, hm, zc, mod, layer, wl, g_final, *, final):
    rows, d = x.shape
    lw = functools.partial(_layer_resident, layer)
    return pl.pallas_call(
        functools.partial(_sample_out_kernel, rows=rows, final=final),
        grid=(1,),
        in_specs=[_resident((rows, d)), _resident(hm.shape), _resident(zc.shape),
                  _resident(mod.shape), lw((d, D_OUT_PAD)), lw((1, d)), lw((d, 2 * D_FF)),
                  lw((D_FF, D_OUT_PAD)),
                  _resident((1, d))],
        out_specs=pl.BlockSpec((rows, d), lambda i: (0, 0)),
        out_shape=jax.ShapeDtypeStruct((rows, D_MODEL), F32),
        scratch_shapes=[
            pltpu.VMEM((rows, D_MODEL), BF16),
            pltpu.VMEM((rows, D_MODEL), F32),
            pltpu.VMEM((rows, D_MODEL), BF16),
            pltpu.VMEM((rows, D_FF), BF16),
        ],
        compiler_params=pltpu.CompilerParams(
            dimension_semantics=("arbitrary",), vmem_limit_bytes=VMEM_LIMIT),
        name="sample_out",
    )(x, hm, zc, mod, wl["w_out"], wl["g_ffn"], wl["w_ffn13"], wl["w_ffn2"], g_final)


def _prep_weights(g_mix, w_in, b_gates, g_head, w_dw, b_dw, ln_g, ln_b, w_out, g_ffn, w_ffn13,
                  w_ffn2):
    depth, d, _ = w_in.shape
    n_gate = 2 * HEADS
    gate0 = 4 * MLSTM_WIDTH
    lane_pad = lambda w, n: jnp.pad(w.astype(BF16), ((0, 0), (0, 0), (0, n - w.shape[2])))
    return {
        "g_mix": g_mix.reshape(depth, 1, d),
        "w_qkvo": w_in[:, :, :gate0].astype(BF16),
        "w_glu": lane_pad(w_in[:, :, gate0 + n_gate:], D_OUT_PAD),
        "w_gate": lane_pad(w_in[:, :, gate0:gate0 + n_gate], LANES),
        "b_gates": jnp.pad(b_gates, ((0, 0), (0, LANES - n_gate))).reshape(depth, 1, LANES),
        "g_head": g_head.reshape(depth, 1, MLSTM_WIDTH),
        "w_dw": w_dw,
        "b_dw": b_dw.reshape(depth, 1, CONV_CH),
        "ln_g": ln_g.reshape(depth, 1, CONV_CH),
        "ln_b": ln_b.reshape(depth, 1, CONV_CH),
        "w_out": jnp.pad(w_out.astype(BF16), ((0, 0), (0, 0), (0, D_OUT_PAD - d))),
        "g_ffn": g_ffn.reshape(depth, 1, d),
        "w_ffn13": w_ffn13.astype(BF16),
        "w_ffn2": jnp.pad(w_ffn2.astype(BF16), ((0, 0), (0, 0), (0, D_OUT_PAD - d))),
    }


def _forward(x_prompt, x_sample, c_prompt, c_sample, state_C, state_n, state_m, state_conv,
             w_ada, b_ada, g_mix, w_in, b_gates, g_head, w_dw, b_dw, ln_g, ln_b, w_out,
             g_ffn, w_ffn13, w_ffn2, g_final, *, tm):
    depth = w_ada.shape[0]
    bp = x_prompt.shape[0]
    d = x_prompt.shape[-1]
    mod = _modulation(jnp.concatenate([c_prompt, c_sample], axis=0), w_ada, b_ada)
    gfin = g_final.reshape(1, d)
    xp = x_prompt
    xs = x_sample.reshape(x_sample.shape[0], d)
    outs_p = [[], [], [], []]
    ns_l, ms_l = [], []
    stacked = None
    n_seq = xs.shape[0]
    state_n2 = state_n.reshape(depth, n_seq, MLSTM_WIDTH)
    state_m_pad = jnp.pad(state_m, ((0, 0), (0, 0), (0, LANES - HEADS)))
    state_conv_t = jnp.swapaxes(state_conv, 1, 2)
    wl = _prep_weights(g_mix, w_in, b_gates, g_head, w_dw, b_dw, ln_g, ln_b, w_out, g_ffn,
                       w_ffn13, w_ffn2)
    for l in range(depth):
        final = l == depth - 1
        mod_p = mod[l, :bp].reshape(bp, 6, d)
        mod_s = mod[l, bp:]
        xp, c1, n1, m1, b1 = _prompt_layer(xp, mod_p, l, wl, gfin, tm=tm, final=final)
        for acc, val in zip(outs_p, (c1, n1, m1[:, :, 0], b1)):
            acc.append(val)
        p = _sample_in_proj(xs, mod_s, l, wl)
        hm, zc, stacked, n2, m2 = _sample_step(p, l, state_C, state_n2, state_m_pad, state_conv_t,
                                               stacked, wl)
        xs = _sample_out(xs, hm, zc, mod_s, l, wl, gfin, final=final)
        ns_l.append(n2)
        ms_l.append(m2)
    y_sample = xs.reshape(x_sample.shape)
    c_sample_new, conv_t_new = stacked
    return ((xp, y_sample) + tuple(jnp.stack(a) for a in outs_p)
            + (c_sample_new, jnp.stack(ns_l), jnp.stack(ms_l), jnp.swapaxes(conv_t_new, 1, 2)))


def kernel(x_prompt, x_sample, c_prompt, c_sample, state_C, state_n, state_m, state_conv, w_ada, b_ada, g_mix, w_in, b_gates, g_head, w_dw, b_dw, ln_g, ln_b, w_out, g_ffn, w_ffn13, w_ffn2, g_final):
    return _forward(x_prompt, x_sample, c_prompt, c_sample, state_C, state_n, state_m, state_conv,
                    w_ada, b_ada, g_mix, w_in, b_gates, g_head, w_dw, b_dw, ln_g, ln_b, w_out,
                    g_ffn, w_ffn13, w_ffn2, g_final, tm=256)
```
